```python
import jax, jax.numpy as jnp
from jax import lax
import numpy as np

D_MODEL = 1024
BATCH = 8
SEQ = 2048
DEPTH = 2
DEC_BATCH = 32
DEC_SEQ = 4
PAST_LEN = 8192
PAGE_SIZE = 128

HEAD_DIM = 64
HEADS_PER_GROUP = 4
WINDOWS = (128, 512, 2048)
DILATIONS = (1, 4, 16)
N_GROUPS = 3
N_ATT_HEADS = N_GROUPS * HEADS_PER_GROUP
ATT_WIDTH = N_ATT_HEADS * HEAD_DIM
MERGED_ATT_WIDTH = HEADS_PER_GROUP * HEAD_DIM
CHUNK = 128
GMLP_GROUPS = 4
GMLP_WIDTH = D_MODEL // 2
GMLP_GROUP_CH = GMLP_WIDTH // GMLP_GROUPS
D_FF = 2816
IN_WIDTH = 3 * ATT_WIDTH + 2 * GMLP_WIDTH + 2 * D_MODEL
SPLITS = (ATT_WIDTH, 2 * ATT_WIDTH, 3 * ATT_WIDTH,
          3 * ATT_WIDTH + GMLP_WIDTH, 3 * ATT_WIDTH + 2 * GMLP_WIDTH,
          3 * ATT_WIDTH + 2 * GMLP_WIDTH + D_MODEL)
EPS = 1e-6
NEG_INF = -1e30

kernel_name = 'hybrid_dilated_attn_gmlp_decode_step'


def rmsnorm(x, gain):
    xf = x.astype(jnp.float32)
    r = lax.rsqrt(jnp.mean(xf * xf, axis=-1, keepdims=True) + EPS)
    return (xf * r).astype(x.dtype) * gain


def swiglu_half_step(x, norm, w_gate, w_up, w_down):
    h = rmsnorm(x, norm)
    return x + 0.5 * ((jax.nn.silu(h @ w_gate) * (h @ w_up)) @ w_down)


def alibi_slopes():
    h = np.arange(1, N_ATT_HEADS + 1, dtype=np.float32)
    return np.power(np.float32(2.0), -8.0 * h / N_ATT_HEADS).astype(np.float32).reshape(N_GROUPS, HEADS_PER_GROUP)


def dilated_attn_prompt(q, k, v, window, dilation, slopes):
    B, S, H, E = q.shape
    K = window // dilation
    span = dilation * K
    s_pad = -(-S // span) * span
    n = s_pad // dilation
    nb = n // K

    def to_blocks(t):
        t = jnp.pad(t, ((0, 0), (0, s_pad - S), (0, 0), (0, 0)))
        t = t.reshape(B, n, dilation, H, E).transpose(0, 2, 1, 3, 4)
        return t.reshape(B, dilation, nb, K, H, E)

    def with_prev(t):
        prev = jnp.pad(t[:, :, :-1], ((0, 0), (0, 0), (1, 0), (0, 0), (0, 0), (0, 0)))
        return jnp.concatenate([prev, t], axis=3)

    qb = to_blocks(q)
    kb = with_prev(to_blocks(k))
    vb = with_prev(to_blocks(v))
    a = jnp.arange(K)[:, None]
    c = jnp.arange(2 * K)[None, :]
    dist = K + a - c
    blk = jnp.arange(nb)[:, None, None]
    valid = (dist >= 0) & (dist <= K) & ((blk > 0) | (c >= K)[None])
    bias = -slopes[None, :, None] * (dilation * dist)[:, None, :]
    s = jnp.einsum('brnqhe,brnkhe->brnqhk', qb, kb).astype(jnp.float32) * (HEAD_DIM ** -0.5)
    s = jnp.where(valid[None, None, :, :, None, :], s + bias[None, None, None], NEG_INF)
    m = jnp.max(s, axis=-1, keepdims=True)
    p = jnp.exp(s - m)
    l = jnp.sum(p, axis=-1)
    lse = m[..., 0] + jnp.log(l)
    o = jnp.einsum('brnqhk,brnkhe->brnqhe', p.astype(v.dtype), vb) / l[..., None].astype(v.dtype)
    o = o.reshape(B, dilation, n, H, E).transpose(0, 2, 1, 3, 4).reshape(B, s_pad, H, E)[:, :S]
    lse = lse.reshape(B, dilation, n, H).transpose(0, 2, 1, 3).reshape(B, s_pad, H)[:, :S]
    return o, lse


def dilated_attn_sample(q, k_ext, v_ext, past_rows, window, dilation, slopes):
    T = q.shape[1]
    K = window // dilation
    steps = jnp.arange(K + 1) * dilation
    idx = past_rows + jnp.arange(T)[:, None] - steps[None, :]
    valid = idx >= 0
    idx_c = jnp.clip(idx, 0)
    kg = k_ext[:, idx_c]
    vg = v_ext[:, idx_c]
    bias = -slopes[:, None] * steps[None, :]
    s = jnp.einsum('bthe,btkhe->bthk', q, kg).astype(jnp.float32) * (HEAD_DIM ** -0.5)
    s = jnp.where(valid[None, :, None, :], s + bias[None, None], NEG_INF)
    m = jnp.max(s, axis=-1, keepdims=True)
    p = jnp.exp(s - m)
    l = jnp.sum(p, axis=-1)
    lse = m[..., 0] + jnp.log(l)
    o = jnp.einsum('bthk,btkhe->bthe', p.astype(v_ext.dtype), vg) / l[..., None].astype(v_ext.dtype)
    return o, lse


def combine_groups(outs, lses):
    o = jnp.stack(outs, axis=0)
    w = jax.nn.softmax(jnp.stack(lses, axis=0), axis=0)
    return jnp.sum(w[..., None] * o.astype(jnp.float32), axis=0).astype(o.dtype)


def chunk_gmlp(u, vs, w_s, b_s):
    B, T, _ = u.shape
    t_pad = -(-T // CHUNK) * CHUNK
    mask = jnp.tril(jnp.ones((CHUNK, CHUNK), dtype=bool))
    ws = jnp.where(mask[None], w_s, jnp.zeros((), w_s.dtype))
    vb = jnp.pad(vs, ((0, 0), (0, t_pad - T), (0, 0)))
    vb = vb.reshape(B, t_pad // CHUNK, CHUNK, GMLP_GROUPS, GMLP_GROUP_CH)
    z = jnp.einsum('gts,bnsgc->bntgc', ws, vb) + b_s.T[None, None, :, :, None]
    z = z.reshape(B, t_pad, GMLP_WIDTH)[:, :T]
    return u * z


def mixer_inputs(h, w_in, v_norm):
    z = h @ w_in
    q, k, v, u, vs, ga, gb = jnp.split(z, SPLITS, axis=-1)
    shp = h.shape[:-1] + (N_GROUPS, HEADS_PER_GROUP, HEAD_DIM)
    return (q.reshape(shp), k.reshape(shp), v.reshape(shp),
            jax.nn.gelu(u), rmsnorm(jax.nn.gelu(vs), v_norm),
            jax.nn.sigmoid(ga), jax.nn.sigmoid(gb))


def token_mixing(x, mix_norm, w_in, v_norm, w_s, b_s, proj_att, proj_spatial, w_out, past_kv):
    h = rmsnorm(x, mix_norm)
    q, k, v, u, vs, ga, gb = mixer_inputs(h, w_in, v_norm)
    slopes = alibi_slopes()
    outs, lses, new_kv = [], [], []
    for g in range(N_GROUPS):
        qg, kg, vg = q[:, :, g], k[:, :, g], v[:, :, g]
        if past_kv is None:
            o, lse = dilated_attn_prompt(qg, kg, vg, WINDOWS[g], DILATIONS[g], slopes[g])
            keep = min(WINDOWS[g], x.shape[1])
            new_kv.append(jnp.stack([kg[:, -keep:], vg[:, -keep:]], axis=2))
        else:
            buf = past_kv[g]
            k_ext = jnp.concatenate([buf[:, :, 0], kg], axis=1)
            v_ext = jnp.concatenate([buf[:, :, 1], vg], axis=1)
            o, lse = dilated_attn_sample(qg, k_ext, v_ext, buf.shape[1], WINDOWS[g], DILATIONS[g], slopes[g])
            new_kv.append(jnp.stack([kg, vg], axis=2))
        outs.append(o)
        lses.append(lse)
    att = combine_groups(outs, lses)
    att = att.reshape(att.shape[:2] + (MERGED_ATT_WIDTH,))
    spat = chunk_gmlp(u, vs, w_s, b_s)
    out = (ga * (att @ proj_att) + gb * (spat @ proj_spatial)) @ w_out
    return x + out, new_kv, vs


def setup_inputs(seed: int = 0) -> dict:
    key = jax.random.key(seed)
    ks = iter(jax.random.split(key, 32))

    def nrm(shape, scale):
        return scale * jax.random.normal(next(ks), shape, jnp.float32)

    def gain(shape):
        return 1.0 + 0.01 * jax.random.normal(next(ks), shape, jnp.float32)

    inputs = {
        'x_prompt': nrm((BATCH, SEQ, D_MODEL), 1.0),
        'x_sample': nrm((DEC_BATCH, DEC_SEQ, D_MODEL), 1.0),
    }
    for w in WINDOWS:
        rows = min(w, PAST_LEN)
        inputs['cache_kv_w%d' % w] = nrm((DEPTH, DEC_BATCH, rows, 2, HEADS_PER_GROUP, HEAD_DIM), 1.0)
    inputs['ffn1_norm'] = gain((DEPTH, D_MODEL))
    inputs['ffn1_gate'] = nrm((DEPTH, D_MODEL, D_FF), D_MODEL ** -0.5)
    inputs['ffn1_up'] = nrm((DEPTH, D_MODEL, D_FF), D_MODEL ** -0.5)
    inputs['ffn1_down'] = nrm((DEPTH, D_FF, D_MODEL), D_FF ** -0.5)
    inputs['mix_norm'] = gain((DEPTH, D_MODEL))
    inputs['w_in'] = nrm((DEPTH, D_MODEL, IN_WIDTH), D_MODEL ** -0.5)
    inputs['gmlp_v_norm'] = gain((DEPTH, GMLP_WIDTH))
    inputs['gmlp_ws'] = nrm((DEPTH, GMLP_GROUPS, CHUNK, CHUNK), CHUNK ** -0.5)
    inputs['gmlp_bias'] = 1.0 + nrm((DEPTH, GMLP_GROUPS, CHUNK), 0.1)
    inputs['proj_att'] = nrm((DEPTH, MERGED_ATT_WIDTH, D_MODEL), MERGED_ATT_WIDTH ** -0.5)
    inputs['proj_spatial'] = nrm((DEPTH, GMLP_WIDTH, D_MODEL), GMLP_WIDTH ** -0.5)
    inputs['w_out'] = nrm((DEPTH, D_MODEL, D_MODEL), D_MODEL ** -0.5)
    inputs['ffn2_norm'] = gain((DEPTH, D_MODEL))
    inputs['ffn2_gate'] = nrm((DEPTH, D_MODEL, D_FF), D_MODEL ** -0.5)
    inputs['ffn2_up'] = nrm((DEPTH, D_MODEL, D_FF), D_MODEL ** -0.5)
    inputs['ffn2_down'] = nrm((DEPTH, D_FF, D_MODEL), D_FF ** -0.5)
    inputs['final_norm'] = gain((D_MODEL,))
    return inputs


def reference(x_prompt, x_sample, cache_kv_w128, cache_kv_w512, cache_kv_w2048,
              ffn1_norm, ffn1_gate, ffn1_up, ffn1_down, mix_norm, w_in, gmlp_v_norm,
              gmlp_ws, gmlp_bias, proj_att, proj_spatial, w_out,
              ffn2_norm, ffn2_gate, ffn2_up, ffn2_down, final_norm):
    caches = (cache_kv_w128, cache_kv_w512, cache_kv_w2048)
    xp, xs = x_prompt, x_sample
    kv_p = [[] for _ in range(N_GROUPS)]
    kv_s = [[] for _ in range(N_GROUPS)]
    gv_s = []
    for l in range(DEPTH):
        mix_w = (mix_norm[l], w_in[l], gmlp_v_norm[l], gmlp_ws[l], gmlp_bias[l],
                 proj_att[l], proj_spatial[l], w_out[l])
        xp = swiglu_half_step(xp, ffn1_norm[l], ffn1_gate[l], ffn1_up[l], ffn1_down[l])
        xp, new_p, _ = token_mixing(xp, *mix_w, None)
        xp = swiglu_half_step(xp, ffn2_norm[l], ffn2_gate[l], ffn2_up[l], ffn2_down[l])
        xs = swiglu_half_step(xs, ffn1_norm[l], ffn1_gate[l], ffn1_up[l], ffn1_down[l])
        xs, new_s, vs_s = token_mixing(xs, *mix_w, (caches[0][l], caches[1][l], caches[2][l]))
        xs = swiglu_half_step(xs, ffn2_norm[l], ffn2_gate[l], ffn2_up[l], ffn2_down[l])
        for g in range(N_GROUPS):
            kv_p[g].append(new_p[g])
            kv_s[g].append(new_s[g])
        gv_s.append(vs_s)
    y_prompt = rmsnorm(xp, final_norm)
    y_sample = rmsnorm(xs, final_norm)
    kv_w128_prompt = jnp.stack(kv_p[0], axis=0)
    kv_w512_prompt = jnp.stack(kv_p[1], axis=0)
    kv_w2048_prompt = jnp.stack(kv_p[2], axis=0)
    kv_w128_sample = jnp.stack(kv_s[0], axis=0)
    kv_w512_sample = jnp.stack(kv_s[1], axis=0)
    kv_w2048_sample = jnp.stack(kv_s[2], axis=0)
    gmlp_v_sample = jnp.stack(gv_s, axis=0)
    return (y_prompt, y_sample, kv_w128_prompt, kv_w512_prompt, kv_w2048_prompt,
            kv_w128_sample, kv_w512_sample, kv_w2048_sample, gmlp_v_sample)
```

```python
import functools

import numpy as np
import jax
import jax.numpy as jnp
from jax import lax
from jax.experimental import pallas as pl
from jax.experimental.pallas import tpu as pltpu

F32 = jnp.float32
BF16 = jnp.bfloat16

D = 1024
DEPTH = 2
B_P, S_P = 8, 2048
B_S, T_S = 32, 4
N_P = B_P * S_P
N_S = B_S * T_S
HD, NH, NG = 64, 4, 3
GW = NH * HD
ATT = NG * GW
WINDOWS = (128, 512, 2048)
DILS = (1, 4, 16)
KB = 128
CHUNK = 128
GG, GC = 4, 128
GMW = GG * GC
DFF = 2816
FC = 256
QKVUV = 3 * ATT + 2 * GMW
EPS = 1e-6
NEG = -1e30
TM = 512
SPAD = 16
VMEM_LIMIT = 56 * 1024 * 1024

_NT = (((1,), (1,)), ((), ()))


def _rms(x, gain):
    r = lax.rsqrt(jnp.mean(x * x, axis=-1, keepdims=True) + EPS)
    return (x * r) * gain


def _dot(a, b):
    return jnp.dot(a, b, preferred_element_type=F32)


def _dot_nt(a, b):
    return lax.dot_general(a, b, _NT, preferred_element_type=F32)


def _params(n_axes):
    return pltpu.CompilerParams(dimension_semantics=("arbitrary",) * n_axes,
                                vmem_limit_bytes=VMEM_LIMIT)


def _resident(shape, layer):
    nd = len(shape)
    return pl.BlockSpec((None,) + shape, lambda *_: (layer,) + (0,) * nd,
                        pipeline_mode=pl.Buffered(1))


def _const(shape):
    nd = len(shape)
    return pl.BlockSpec(shape, lambda *_: (0,) * nd, pipeline_mode=pl.Buffered(1))


def _cast_body(x_ref, o_ref):
    o_ref[...] = x_ref[...].astype(o_ref.dtype)


def _cast(w, col_blk, first_blk, n_blk):
    depth, rows, _ = w.shape
    return pl.pallas_call(
        _cast_body,
        out_shape=jax.ShapeDtypeStruct((depth, rows, n_blk * col_blk), BF16),
        grid=(depth, n_blk),
        in_specs=[pl.BlockSpec((None, rows, col_blk), lambda l, j: (l, 0, j + first_blk))],
        out_specs=pl.BlockSpec((None, rows, col_blk), lambda l, j: (l, 0, j)),
        compiler_params=_params(2),
        name="cast_bf16",
    )(w)


def _ffn_body(*refs, final):
    if final:
        x_ref, g_ref, wg_ref, wu_ref, wd_ref, fg_ref, o_ref, a_scr = refs
    else:
        x_ref, g_ref, wg_ref, wu_ref, wd_ref, o_ref, a_scr = refs
    x = x_ref[...]
    h = _rms(x, g_ref[...]).astype(BF16)
    for c in range(DFF // FC):
        cs = slice(c * FC, (c + 1) * FC)
        gate = _dot(h, wg_ref[:, cs])
        up = _dot(h, wu_ref[:, cs])
        a_scr[:, cs] = (jax.nn.silu(gate) * up).astype(BF16)
    y = x + 0.5 * _dot(a_scr[...], wd_ref[...])
    if final:
        y = _rms(y, fg_ref[...])
    o_ref[...] = y


def _ffn(x, gain, wg, wu, wd, layer, final_gain=None):
    n = x.shape[0]
    tm = min(TM, n)
    final = final_gain is not None
    in_specs = [pl.BlockSpec((tm, D), lambda i: (i, 0)),
                _resident((1, D), layer),
                _resident((D, DFF), layer), _resident((D, DFF), layer), _resident((DFF, D), layer)]
    args = [x, gain, wg, wu, wd]
    if final:
        in_specs.append(_const((1, D)))
        args.append(final_gain)
    return pl.pallas_call(
        functools.partial(_ffn_body, final=final),
        out_shape=jax.ShapeDtypeStruct((n, D), F32),
        grid=(n // tm,),
        in_specs=in_specs,
        out_specs=pl.BlockSpec((tm, D), lambda i: (i, 0)),
        scratch_shapes=[pltpu.VMEM((tm, DFF), BF16)],
        compiler_params=_params(1),
        name="ffn_final" if final else "ffn",
    )(*args)


def _inproj_body(*refs, prompt, n_alias):
    x_ref, g_ref, w_ref, vn_ref = refs[:4]
    refs = refs[4 + n_alias:]
    qkv_ref, kv1_ref, kv2_ref, kv3_ref, u_ref, vs_ref = refs
    kv_refs = (kv1_ref, kv2_ref, kv3_ref)
    tm = x_ref.shape[0]
    h = _rms(x_ref[...], g_ref[...]).astype(BF16)
    for sec in range(3):
        for g in range(NG):
            c0 = sec * ATT + g * GW
            z = _dot(h, w_ref[:, c0:c0 + GW])
            if sec == 0:
                z = z * (HD ** -0.5)
            qkv_ref[:, c0:c0 + GW] = z.astype(qkv_ref.dtype)
            if sec > 0:
                keep = min(WINDOWS[g], tm) if prompt else tm
                kv_refs[g][:, (sec - 1) * GW:sec * GW] = z[tm - keep:, :]
    u_ref[...] = jax.nn.gelu(_dot(h, w_ref[:, 3 * ATT:3 * ATT + GMW]))
    gv = jax.nn.gelu(_dot(h, w_ref[:, 3 * ATT + GMW:QKVUV]))
    vs_ref[...] = _rms(gv, vn_ref[...]).astype(vs_ref.dtype)


def _inproj_prompt(x, gain, w, vnorm, layer, prev):
    tiles_per_seq = S_P // TM
    n_alias = 0 if prev is None else 3
    in_specs = [pl.BlockSpec((TM, D), lambda i: (i, 0)),
                _resident((1, D), layer), _resident((D, QKVUV), layer), _resident((1, GMW), layer)]
    in_specs += [pl.BlockSpec(memory_space=pl.ANY)] * n_alias
    out_shape = [jax.ShapeDtypeStruct((N_P, 3 * ATT), BF16),
                 jax.ShapeDtypeStruct((DEPTH * B_P, WINDOWS[0], 2 * GW), F32),
                 jax.ShapeDtypeStruct((DEPTH * B_P, WINDOWS[1], 2 * GW), F32),
                 jax.ShapeDtypeStruct((DEPTH * N_P, 2 * GW), F32),
                 jax.ShapeDtypeStruct((N_P, GMW), F32),
                 jax.ShapeDtypeStruct((N_P, GMW), BF16)]
    seq = lambda i: (layer * B_P + i // tiles_per_seq, 0, 0)
    out_specs = [pl.BlockSpec((TM, 3 * ATT), lambda i: (i, 0)),
                 pl.BlockSpec((None, WINDOWS[0], 2 * GW), seq),
                 pl.BlockSpec((None, WINDOWS[1], 2 * GW), seq),
                 pl.BlockSpec((TM, 2 * GW), lambda i: (layer * (N_P // TM) + i, 0)),
                 pl.BlockSpec((TM, GMW), lambda i: (i, 0)),
                 pl.BlockSpec((TM, GMW), lambda i: (i, 0))]
    args = [x, gain, w, vnorm] + ([] if prev is None else list(prev))
    return pl.pallas_call(
        functools.partial(_inproj_body, prompt=True, n_alias=n_alias),
        out_shape=out_shape,
        grid=(N_P // TM,),
        in_specs=in_specs,
        out_specs=out_specs,
        input_output_aliases={4 + k: 1 + k for k in range(n_alias)},
        compiler_params=_params(1),
        name="inproj_prompt",
    )(*args)


def _inproj_sample(x, gain, w, vnorm, layer, prev):
    n_alias = 0 if prev is None else 4
    in_specs = [pl.BlockSpec((N_S, D), lambda i: (0, 0)),
                _resident((1, D), layer), _resident((D, QKVUV), layer), _resident((1, GMW), layer)]
    in_specs += [pl.BlockSpec(memory_space=pl.ANY)] * n_alias
    stacked = jax.ShapeDtypeStruct((DEPTH * N_S, 2 * GW), F32)
    out_shape = [jax.ShapeDtypeStruct((N_S, 3 * ATT), F32), stacked, stacked, stacked,
                 jax.ShapeDtypeStruct((N_S, GMW), F32),
                 jax.ShapeDtypeStruct((DEPTH * N_S, GMW), F32)]
    lay = lambda i: (layer, 0)
    out_specs = [pl.BlockSpec((N_S, 3 * ATT), lambda i: (0, 0)),
                 pl.BlockSpec((N_S, 2 * GW), lay), pl.BlockSpec((N_S, 2 * GW), lay),
                 pl.BlockSpec((N_S, 2 * GW), lay),
                 pl.BlockSpec((N_S, GMW), lambda i: (0, 0)),
                 pl.BlockSpec((N_S, GMW), lay)]
    args = [x, gain, w, vnorm] + ([] if prev is None else list(prev))
    aliases = {4: 1, 5: 2, 6: 3, 7: 5} if prev is not None else {}
    return pl.pallas_call(
        functools.partial(_inproj_body, prompt=False, n_alias=n_alias),
        out_shape=out_shape,
        grid=(1,),
        in_specs=in_specs,
        out_specs=out_specs,
        input_output_aliases=aliases,
        compiler_params=_params(1),
        name="inproj_sample",
    )(*args)


def _alibi_slopes():
    h = np.arange(1, NG * NH + 1, dtype=np.float32)
    return np.power(np.float32(2.0), -8.0 * h / (NG * NH)).astype(np.float32).reshape(NG, NH)


def _head_masks():
    lane = lax.broadcasted_iota(jnp.int32, (1, GW), 1)
    return [(lane >= h * HD) & (lane < (h + 1) * HD) for h in range(NH)]


def _stack_heads(q, masks):
    return jnp.concatenate([jnp.where(m, q, jnp.zeros_like(q)) for m in masks], axis=0)


def _unstack_heads(x, masks, rows):
    out = jnp.zeros((rows, GW), F32)
    for h, m in enumerate(masks):
        out = jnp.where(m, x[h * rows:(h + 1) * rows], out)
    return out


def _softmax_parts(parts, values):
    m = functools.reduce(jnp.maximum, [jnp.max(s, axis=-1, keepdims=True) for s in parts])
    ps = [jnp.exp(s - m) for s in parts]
    l = functools.reduce(jnp.add, [jnp.sum(p, axis=-1, keepdims=True) for p in ps])
    o = functools.reduce(jnp.add, [_dot(p.astype(BF16), v) for p, v in zip(ps, values)])
    return o / l, m + jnp.log(l)


def _merge_groups(outs, lses):
    m = jnp.maximum(jnp.maximum(lses[0], lses[1]), lses[2])
    es = [jnp.exp(l - m) for l in lses]
    tot = es[0] + es[1] + es[2]
    return (es[0] / tot) * outs[0] + (es[1] / tot) * outs[1] + (es[2] / tot) * outs[2]


def _prompt_bias_table():
    slopes = _alibi_slopes()
    a = np.arange(KB)[:, None]
    c = np.arange(KB)[None, :]
    tab = np.empty((NG, 2, NH * KB, KB), np.float32)
    for g in range(NG):
        for part, dist in enumerate((KB + a - c, a - c)):
            valid = (dist >= 0) & (dist <= KB)
            for h in range(NH):
                bias = -slopes[g, h] * (DILS[g] * dist).astype(np.float32)
                tab[g, part, h * KB:(h + 1) * KB] = np.where(valid, bias, np.float32(NEG))
    return tab


def _sample_bias_tables():
    slopes = _alibi_slopes()
    j = np.arange(SPAD)[:, None]
    tabs = []
    for g in range(NG):
        w, d = WINDOWS[g], DILS[g]
        if g == 0:
            c = np.arange(w)[None, :]
            dist = w + j - c
            valid = (j < T_S) & (dist >= 0) & (dist <= w)
        elif g == 1:
            c = np.arange(w)[None, :]
            dist = w + j - c
            valid = (j < T_S) & (dist % d == 0) & (dist >= 0) & (dist <= w)
        else:
            col = np.arange(T_S * KB)[None, :]
            dist = w + j - (col // KB + d * (col % KB))
            valid = (j < T_S) & (col // KB == j) & (dist >= 0) & (dist <= w)
        cn = np.arange(KB)[None, :]
        dist_n = j - cn
        valid_n = (j < T_S) & (cn < T_S) & (dist_n >= 0) & (dist_n % d == 0) & (dist_n <= w)
        dist = np.concatenate([np.broadcast_to(dist, (SPAD, dist.shape[1])), dist_n], axis=1)
        valid = np.concatenate([np.broadcast_to(valid, (SPAD, valid.shape[1])), valid_n], axis=1)
        pad_row = np.broadcast_to(j >= T_S, dist.shape)
        tab = np.empty((NH * SPAD, dist.shape[1]), np.float32)
        for h in range(NH):
            bias = -slopes[g, h] * dist.astype(np.float32)
            blk = np.where(valid, bias, np.float32(NEG))
            tab[h * SPAD:(h + 1) * SPAD] = np.where(pad_row, np.float32(0.0), blk)
        tabs.append(tab)
    return tabs


def _attn_prompt_body(q1, k1c, k1p, v1c, v1p, q2, k2c, k2p, v2c, v2p, q3, k3, v3, bias_ref,
                      o1, l1, o2, l2, o3, l3):
    s = pl.program_id(1)
    masks = _head_masks()

    def band(g, q_ref, kc, kp, vc, vp, has_prev, o_ref, l_ref):
        qs = _stack_heads(q_ref[...], masks)
        parts = [_dot_nt(qs, kc[...]) + bias_ref[g, 1]]
        values = [vc[...]]
        if kp is not None:
            sp = _dot_nt(qs, kp[...]) + bias_ref[g, 0]
            parts.append(jnp.where(has_prev, sp, NEG))
            values.append(vp[...])
        o, lse = _softmax_parts(parts, values)
        o_ref[...] = _unstack_heads(o, masks, KB)
        l_ref[...] = _unstack_heads(lse, masks, KB)

    band(0, q1, k1c, k1p, v1c, v1p, s > 0, o1, l1)
    band(1, q2, k2c, k2p, v2c, v2p, lax.rem(s, 4) > 0, o2, l2)
    band(2, q3, k3, None, v3, None, None, o3, l3)


def _attn_prompt(qkv, bias):
    n_blk = S_P // KB
    v1 = qkv.reshape(B_P, S_P, 3 * ATT)
    v2 = qkv.reshape(B_P, S_P // DILS[1], DILS[1] * 3 * ATT)
    v3 = qkv.reshape(B_P, S_P // DILS[2], DILS[2] * 3 * ATT)
    cols = 3 * NG
    blk = (None, KB, GW)

    def g1(sec, prev):
        return pl.BlockSpec(blk, lambda b, s: (b, jnp.maximum(s - 1, 0) if prev else s, sec * NG))

    def g2(sec, prev):
        def imap(b, s):
            r, j = s // 4, lax.rem(s, 4)
            return (b, jnp.maximum(j - 1, 0) if prev else j, r * cols + sec * NG + 1)
        return pl.BlockSpec(blk, imap)

    def g3(sec):
        return pl.BlockSpec(blk, lambda b, s: (b, 0, s * cols + sec * NG + 2))

    in_specs = [g1(0, False), g1(1, False), g1(1, True), g1(2, False), g1(2, True),
                g2(0, False), g2(1, False), g2(1, True), g2(2, False), g2(2, True),
                g3(0), g3(1), g3(2), _const(bias.shape)]
    args = [v1] * 5 + [v2] * 5 + [v3] * 3 + [bias]
    o_specs = [pl.BlockSpec(blk, lambda b, s: (b, s, 0)),
               pl.BlockSpec(blk, lambda b, s: (b, lax.rem(s, 4), s // 4)),
               pl.BlockSpec(blk, lambda b, s: (b, 0, s))]
    o_shapes = [jax.ShapeDtypeStruct((B_P, S_P // d, d * GW), F32) for d in DILS]
    outs = pl.pallas_call(
        _attn_prompt_body,
        out_shape=[o_shapes[0], o_shapes[0], o_shapes[1], o_shapes[1], o_shapes[2], o_shapes[2]],
        grid=(B_P, n_blk),
        in_specs=in_specs,
        out_specs=[o_specs[0], o_specs[0], o_specs[1], o_specs[1], o_specs[2], o_specs[2]],
        compiler_params=_params(2),
        name="attn_prompt",
    )(*args)
    return [o.reshape(N_P, GW) for o in outs]


def _attn_sample_body(z_ref, c1_ref, c2_ref, c3_ref, b1_ref, b2_ref, b3_ref, att_ref):
    masks = _head_masks()
    z = z_ref[...]
    pad = jnp.zeros((KB - SPAD, GW), F32)
    outs, lses = [], []
    for g, (c_ref, b_ref) in enumerate(((c1_ref, b1_ref), (c2_ref, b2_ref), (c3_ref, b3_ref))):
        q = z[:, g * GW:(g + 1) * GW]
        k_new = jnp.concatenate([z[:, ATT + g * GW:ATT + (g + 1) * GW], pad], axis=0)
        v_new = jnp.concatenate([z[:, 2 * ATT + g * GW:2 * ATT + (g + 1) * GW], pad], axis=0)
        if g < 2:
            k_old, v_old = [c_ref[:, :GW]], [c_ref[:, GW:]]
        else:
            k_old = [c_ref[:, r * 2 * GW:r * 2 * GW + GW] for r in range(T_S)]
            v_old = [c_ref[:, r * 2 * GW + GW:(r + 1) * 2 * GW] for r in range(T_S)]
        keys = jnp.concatenate(k_old + [k_new], axis=0).astype(BF16)
        vals = jnp.concatenate(v_old + [v_new], axis=0).astype(BF16)
        qs = _stack_heads(q, masks).astype(BF16)
        o, lse = _softmax_parts([_dot_nt(qs, keys) + b_ref[...]], [vals])
        outs.append(_unstack_heads(o, masks, SPAD))
        lses.append(_unstack_heads(lse, masks, SPAD))
    att_ref[...] = _merge_groups(outs, lses)


def _attn_sample(z, caches, tabs, layer):
    zp = jnp.pad(z.reshape(B_S, T_S, 3 * ATT), ((0, 0), (0, SPAD - T_S), (0, 0)))
    c1 = caches[0].reshape(DEPTH, B_S, WINDOWS[0], 2 * GW)
    c2 = caches[1].reshape(DEPTH, B_S, WINDOWS[1], 2 * GW)
    c3 = caches[2].reshape(DEPTH, B_S, WINDOWS[2] // DILS[2], DILS[2] * 2 * GW)
    seq = lambda b: (layer, b, 0, 0)
    in_specs = [pl.BlockSpec((None, SPAD, 3 * ATT), lambda b: (b, 0, 0)),
                pl.BlockSpec((None, None, WINDOWS[0], 2 * GW), seq),
                pl.BlockSpec((None, None, WINDOWS[1], 2 * GW), seq),
                pl.BlockSpec((None, None, KB, T_S * 2 * GW), seq),
                _const(tabs[0].shape), _const(tabs[1].shape), _const(tabs[2].shape)]
    att = pl.pallas_call(
        _attn_sample_body,
        out_shape=jax.ShapeDtypeStruct((B_S, SPAD, GW), F32),
        grid=(B_S,),
        in_specs=in_specs,
        out_specs=pl.BlockSpec((None, SPAD, GW), lambda b: (b, 0, 0)),
        compiler_params=_params(1),
        name="attn_sample",
    )(zp, c1, c2, c3, *tabs)
    return att[:, :T_S].reshape(N_S, GW)


def _mix_out(x, h, att, spat, wgate_ref, pa_ref, ps_ref, wo_ref):
    ga = jax.nn.sigmoid(_dot(h, wgate_ref[:, :D]))
    gb = jax.nn.sigmoid(_dot(h, wgate_ref[:, D:]))
    mix = ga * _dot(att.astype(BF16), pa_ref[...]) + gb * _dot(spat, ps_ref[...])
    return x + _dot(mix.astype(BF16), wo_ref[...])


def _outproj_prompt_body(x_ref, g_ref, wgate_ref, o1, l1, o2, l2, o3, l3, u_ref, vs_ref,
                         ws_ref, b_ref, pa_ref, ps_ref, wo_ref, out_ref, spat_scr):
    x = x_ref[...]
    h = _rms(x, g_ref[...]).astype(BF16)
    att = _merge_groups([o1[...], o2[...], o3[...]], [l1[...], l2[...], l3[...]])
    row = lax.broadcasted_iota(jnp.int32, (CHUNK, CHUNK), 0)
    col = lax.broadcasted_iota(jnp.int32, (CHUNK, CHUNK), 1)
    for g in range(GG):
        w = jnp.where(col <= row, ws_ref[g], 0.0).astype(BF16)
        gs = slice(g * GC, (g + 1) * GC)
        for c in range(TM // CHUNK):
            rs = slice(c * CHUNK, (c + 1) * CHUNK)
            zc = _dot(w, vs_ref[rs, gs]) + b_ref[g]
            spat_scr[rs, gs] = (u_ref[rs, gs] * zc).astype(BF16)
    out_ref[...] = _mix_out(x, h, att, spat_scr[...], wgate_ref, pa_ref, ps_ref, wo_ref)


def _outproj_prompt(x, gain, wgate, attn_outs, u, vs, ws, bias, pa, ps, wo, layer):
    row = lambda i: (i, 0)
    in_specs = ([pl.BlockSpec((TM, D), row), _resident((1, D), layer), _resident((D, 2 * D), layer)]
                + [pl.BlockSpec((TM, GW), row)] * 6
                + [pl.BlockSpec((TM, GMW), row), pl.BlockSpec((TM, GMW), row),
                   _resident((GG, CHUNK, CHUNK), layer), _resident((GG, CHUNK, 1), layer),
                   _resident((GW, D), layer), _resident((GMW, D), layer), _resident((D, D), layer)])
    return pl.pallas_call(
        _outproj_prompt_body,
        out_shape=jax.ShapeDtypeStruct((N_P, D), F32),
        grid=(N_P // TM,),
        in_specs=in_specs,
        out_specs=pl.BlockSpec((TM, D), row),
        scratch_shapes=[pltpu.VMEM((TM, GMW), BF16)],
        compiler_params=_params(1),
        name="outproj_prompt",
    )(x, gain, wgate, *attn_outs, u, vs, ws, bias, pa, ps, wo)


def _outproj_sample_body(x_ref, g_ref, wgate_ref, att_ref, u_ref, vs_ref, wt_ref, b_ref,
                         pa_ref, ps_ref, wo_ref, out_ref):
    x = x_ref[...]
    h = _rms(x, g_ref[...]).astype(BF16)
    row = lax.broadcasted_iota(jnp.int32, (N_S, N_S), 0)
    col = lax.broadcasted_iota(jnp.int32, (N_S, N_S), 1)
    keep = (row // T_S == col // T_S) & (col <= row)
    u = u_ref[...]
    vs = vs_ref[...].astype(BF16)
    parts = []
    for g in range(GG):
        w = jnp.where(keep, wt_ref[g], 0.0).astype(BF16)
        gs = slice(g * GC, (g + 1) * GC)
        parts.append(u[:, gs] * (_dot(w, vs[:, gs]) + b_ref[g]))
    spat = jnp.concatenate(parts, axis=1).astype(BF16)
    out_ref[...] = _mix_out(x, h, att_ref[...], spat, wgate_ref, pa_ref, ps_ref, wo_ref)


def _outproj_sample(x, gain, wgate, att, u, vs_all, wt, bias, pa, ps, wo, layer):
    zero = lambda i: (0, 0)
    in_specs = [pl.BlockSpec((N_S, D), zero), _resident((1, D), layer), _resident((D, 2 * D), layer),
                pl.BlockSpec((N_S, GW), zero), pl.BlockSpec((N_S, GMW), zero),
                pl.BlockSpec((N_S, GMW), lambda i: (layer, 0)),
                _resident((GG, N_S, N_S), layer), _resident((GG, N_S, 1), layer),
                _resident((GW, D), layer), _resident((GMW, D), layer), _resident((D, D), layer)]
    return pl.pallas_call(
        _outproj_sample_body,
        out_shape=jax.ShapeDtypeStruct((N_S, D), F32),
        grid=(1,),
        in_specs=in_specs,
        out_specs=pl.BlockSpec((N_S, D), zero),
        compiler_params=_params(1),
        name="outproj_sample",
    )(x, gain, wgate, att, u, vs_all, wt, bias, pa, ps, wo)


def kernel(x_prompt, x_sample, cache_kv_w128, cache_kv_w512, cache_kv_w2048, ffn1_norm, ffn1_gate, ffn1_up, ffn1_down, mix_norm, w_in, gmlp_v_norm, gmlp_ws, gmlp_bias, proj_att, proj_spatial, w_out, ffn2_norm, ffn2_gate, ffn2_up, ffn2_down, final_norm):
    caches = (cache_kv_w128, cache_kv_w512, cache_kv_w2048)
    xp = x_prompt.reshape(N_P, D)
    xs = x_sample.reshape(N_S, D)

    cb = 256
    f1g, f1u = _cast(ffn1_gate, cb, 0, DFF // cb), _cast(ffn1_up, cb, 0, DFF // cb)
    f2g, f2u = _cast(ffn2_gate, cb, 0, DFF // cb), _cast(ffn2_up, cb, 0, DFF // cb)
    f1d, f2d = _cast(ffn1_down, cb, 0, D // cb), _cast(ffn2_down, cb, 0, D // cb)
    w_qkvuv = _cast(w_in, cb, 0, QKVUV // cb)
    w_gate = _cast(w_in, cb, QKVUV // cb, 2 * D // cb)
    pa, ps, wo = _cast(proj_att, cb, 0, D // cb), _cast(proj_spatial, cb, 0, D // cb), _cast(w_out, cb, 0, D // cb)

    n1 = ffn1_norm.reshape(DEPTH, 1, D)
    n2 = ffn2_norm.reshape(DEPTH, 1, D)
    nm = mix_norm.reshape(DEPTH, 1, D)
    nv = gmlp_v_norm.reshape(DEPTH, 1, GMW)
    nf = final_norm.reshape(1, D)
    bias_p = gmlp_bias.reshape(DEPTH, GG, CHUNK, 1)
    wt_s = jnp.tile(gmlp_ws[:, :, :T_S, :T_S], (1, 1, B_S, B_S))
    bias_s = jnp.tile(gmlp_bias[:, :, :T_S], (1, 1, B_S)).reshape(DEPTH, GG, N_S, 1)

    attn_bias = jnp.asarray(_prompt_bias_table())
    sample_tabs = [jnp.asarray(t) for t in _sample_bias_tables()]

    kv_p = kv_s = None
    for l in range(DEPTH):
        last = l == DEPTH - 1
        xp = _ffn(xp, n1, f1g, f1u, f1d, l)
        qkv, kv1, kv2, kv3, u, vs = _inproj_prompt(xp, nm, w_qkvuv, nv, l, kv_p)
        kv_p = (kv1, kv2, kv3)
        attn_outs = _attn_prompt(qkv, attn_bias)
        xp = _outproj_prompt(xp, nm, w_gate, attn_outs, u, vs, gmlp_ws, bias_p, pa, ps, wo, l)
        xp = _ffn(xp, n2, f2g, f2u, f2d, l, nf if last else None)
        xs = _ffn(xs, n1, f1g, f1u, f1d, l)
        z_s, ks1, ks2, ks3, u_s, vs_s = _inproj_sample(xs, nm, w_qkvuv, nv, l, kv_s)
        kv_s = (ks1, ks2, ks3, vs_s)
        att_s = _attn_sample(z_s, caches, sample_tabs, l)
        xs = _outproj_sample(xs, nm, w_gate, att_s, u_s, vs_s, wt_s, bias_s, pa, ps, wo, l)
        xs = _ffn(xs, n2, f2g, f2u, f2d, l, nf if last else None)

    kv_shape = (DEPTH, B_P, -1, 2, NH, HD)
    ks_shape = (DEPTH, B_S, T_S, 2, NH, HD)
    return (xp.reshape(B_P, S_P, D), xs.reshape(B_S, T_S, D),
            kv_p[0].reshape(kv_shape), kv_p[1].reshape(kv_shape), kv_p[2].reshape(kv_shape),
            kv_s[0].reshape(ks_shape), kv_s[1].reshape(ks_shape), kv_s[2].reshape(ks_shape),
            kv_s[3].reshape(DEPTH, B_S, T_S, GMW))
```

```python
import functools

import numpy as np
import jax
import jax.numpy as jnp
from jax import lax
from jax.experimental import pallas as pl
from jax.experimental.pallas import tpu as pltpu

F32 = jnp.float32
BF16 = jnp.bfloat16

D = 1024
DEPTH = 2
B_P, S_P = 8, 2048
B_S, T_S = 32, 4
N_P = B_P * S_P
N_S = B_S * T_S
HD, NH, NG = 64, 4, 3
GW = NH * HD
ATT = NG * GW
WINDOWS = (128, 512, 2048)
DILS = (1, 4, 16)
KB = 128
CHUNK = 128
GG, GC = 4, 128
GMW = GG * GC
DFF = 2816
FC = 256
QKVUV = 3 * ATT + 2 * GMW
EPS = 1e-6
NEG = -1e30
TM = 512
SPAD = 16
LANES = 128
SLABS = GW // LANES
N_SLAB = 3 * ATT // LANES
VMEM_LIMIT = 56 * 1024 * 1024

_NT = (((1,), (1,)), ((), ()))


def _rms(x, gain):
    r = lax.rsqrt(jnp.mean(x * x, axis=-1, keepdims=True) + EPS)
    return (x * r) * gain


def _dot(a, b):
    return jnp.dot(a, b, preferred_element_type=F32)


def _dot_nt(a, b):
    return lax.dot_general(a, b, _NT, preferred_element_type=F32)


def _params(n_axes):
    return pltpu.CompilerParams(dimension_semantics=("arbitrary",) * n_axes,
                                vmem_limit_bytes=VMEM_LIMIT)


def _resident(shape, layer):
    nd = len(shape)
    return pl.BlockSpec((None,) + shape, lambda *_: (layer,) + (0,) * nd,
                        pipeline_mode=pl.Buffered(1))


def _const(shape):
    nd = len(shape)
    return pl.BlockSpec(shape, lambda *_: (0,) * nd, pipeline_mode=pl.Buffered(1))


def _cast_body(x_ref, o_ref):
    o_ref[...] = x_ref[...].astype(o_ref.dtype)


def _cast(w, col_blk, first_blk, n_blk):
    depth, rows, _ = w.shape
    return pl.pallas_call(
        _cast_body,
        out_shape=jax.ShapeDtypeStruct((depth, rows, n_blk * col_blk), BF16),
        grid=(depth, n_blk),
        in_specs=[pl.BlockSpec((None, rows, col_blk), lambda l, j: (l, 0, j + first_blk))],
        out_specs=pl.BlockSpec((None, rows, col_blk), lambda l, j: (l, 0, j)),
        compiler_params=_params(2),
        name="cast_bf16",
    )(w)


def _ffn_body(*refs, final):
    if final:
        x_ref, g_ref, wg_ref, wu_ref, wd_ref, fg_ref, o_ref, a_scr = refs
    else:
        x_ref, g_ref, wg_ref, wu_ref, wd_ref, o_ref, a_scr = refs
    x = x_ref[...]
    h = _rms(x, g_ref[...]).astype(BF16)
    for c in range(DFF // FC):
        cs = slice(c * FC, (c + 1) * FC)
        gate = _dot(h, wg_ref[:, cs])
        up = _dot(h, wu_ref[:, cs])
        a_scr[:, cs] = (jax.nn.silu(gate) * up).astype(BF16)
    y = x + 0.5 * _dot(a_scr[...], wd_ref[...])
    if final:
        y = _rms(y, fg_ref[...])
    o_ref[...] = y


def _ffn(x, gain, wg, wu, wd, layer, final_gain=None):
    n = x.shape[0]
    tm = min(TM, n)
    final = final_gain is not None
    in_specs = [pl.BlockSpec((tm, D), lambda i: (i, 0)),
                _resident((1, D), layer),
                _resident((D, DFF), layer), _resident((D, DFF), layer), _resident((DFF, D), layer)]
    args = [x, gain, wg, wu, wd]
    if final:
        in_specs.append(_const((1, D)))
        args.append(final_gain)
    return pl.pallas_call(
        functools.partial(_ffn_body, final=final),
        out_shape=jax.ShapeDtypeStruct((n, D), F32),
        grid=(n // tm,),
        in_specs=in_specs,
        out_specs=pl.BlockSpec((tm, D), lambda i: (i, 0)),
        scratch_shapes=[pltpu.VMEM((tm, DFF), BF16)],
        compiler_params=_params(1),
        name="ffn_final" if final else "ffn",
    )(*args)


def _project(h, w_ref, sec, g):
    c0 = sec * ATT + g * GW
    z = _dot(h, w_ref[:, c0:c0 + GW])
    if sec == 0:
        z = z * (HD ** -0.5)
    return z


def _gmlp_inputs(h, w_ref, vn_ref, u_ref, vs_ref):
    u_ref[...] = jax.nn.gelu(_dot(h, w_ref[:, 3 * ATT:3 * ATT + GMW]))
    gv = jax.nn.gelu(_dot(h, w_ref[:, 3 * ATT + GMW:QKVUV]))
    vs_ref[...] = _rms(gv, vn_ref[...]).astype(vs_ref.dtype)


def _inproj_prompt_body(*refs, n_alias):
    x_ref, g_ref, w_ref, vn_ref = refs[:4]
    qkv_ref, kv1_ref, kv2_ref, kv3_ref, u_ref, vs_ref = refs[4 + n_alias:]
    last_tile = lax.rem(pl.program_id(0), S_P // TM) == S_P // TM - 1
    h = _rms(x_ref[...], g_ref[...]).astype(BF16)
    for sec in range(3):
        for g in range(NG):
            z = _project(h, w_ref, sec, g)
            for sl in range(SLABS):
                qkv_ref[(sec * NG + g) * SLABS + sl] = z[:, sl * LANES:(sl + 1) * LANES]
            if sec == 0:
                continue
            rows = slice((sec - 1) * GW, sec * GW)
            if g == 2:
                kv3_ref[rows, :] = z.T
            elif g == 1:
                @pl.when(last_tile)
                def _():
                    kv2_ref[rows, :] = z.T
            else:
                @pl.when(last_tile)
                def _():
                    kv1_ref[rows, :] = z[TM - WINDOWS[0]:, :].T
    _gmlp_inputs(h, w_ref, vn_ref, u_ref, vs_ref)


def _inproj_prompt(x, gain, w, vnorm, layer, prev):
    tiles = S_P // TM
    n_alias = 0 if prev is None else 3
    in_specs = [pl.BlockSpec((TM, D), lambda i: (i, 0)),
                _resident((1, D), layer), _resident((D, QKVUV), layer), _resident((1, GMW), layer)]
    in_specs += [pl.BlockSpec(memory_space=pl.ANY)] * n_alias
    out_shape = [jax.ShapeDtypeStruct((B_P, N_SLAB, S_P, LANES), F32),
                 jax.ShapeDtypeStruct((DEPTH * B_P, 2 * GW, WINDOWS[0]), F32),
                 jax.ShapeDtypeStruct((DEPTH * B_P, 2 * GW, WINDOWS[1]), F32),
                 jax.ShapeDtypeStruct((DEPTH * B_P, 2 * GW, WINDOWS[2]), F32),
                 jax.ShapeDtypeStruct((N_P, GMW), F32),
                 jax.ShapeDtypeStruct((N_P, GMW), BF16)]
    seq = lambda i: (layer * B_P + i // tiles, 0, 0)
    out_specs = [pl.BlockSpec((None, N_SLAB, TM, LANES), lambda i: (i // tiles, 0, lax.rem(i, tiles), 0)),
                 pl.BlockSpec((None, 2 * GW, WINDOWS[0]), seq),
                 pl.BlockSpec((None, 2 * GW, WINDOWS[1]), seq),
                 pl.BlockSpec((None, 2 * GW, TM), lambda i: (layer * B_P + i // tiles, 0, lax.rem(i, tiles))),
                 pl.BlockSpec((TM, GMW), lambda i: (i, 0)),
                 pl.BlockSpec((TM, GMW), lambda i: (i, 0))]
    args = [x, gain, w, vnorm] + ([] if prev is None else list(prev))
    return pl.pallas_call(
        functools.partial(_inproj_prompt_body, n_alias=n_alias),
        out_shape=out_shape,
        grid=(N_P // TM,),
        in_specs=in_specs,
        out_specs=out_specs,
        input_output_aliases={4 + k: 1 + k for k in range(n_alias)},
        compiler_params=_params(1),
        name="inproj_prompt",
    )(*args)


def _inproj_sample_body(*refs, n_alias):
    x_ref, g_ref, w_ref, vn_ref = refs[:4]
    qkv_ref, kv1_ref, kv2_ref, kv3_ref, u_ref, vs_ref = refs[4 + n_alias:]
    kv_refs = (kv1_ref, kv2_ref, kv3_ref)
    h = _rms(x_ref[...], g_ref[...]).astype(BF16)
    for sec in range(3):
        for g in range(NG):
            z = _project(h, w_ref, sec, g)
            qkv_ref[:, sec * ATT + g * GW:sec * ATT + (g + 1) * GW] = z
            if sec > 0:
                kv_refs[g][:, (sec - 1) * GW:sec * GW] = z
    _gmlp_inputs(h, w_ref, vn_ref, u_ref, vs_ref)


def _inproj_sample(x, gain, w, vnorm, layer, prev):
    n_alias = 0 if prev is None else 4
    in_specs = [pl.BlockSpec((N_S, D), lambda i: (0, 0)),
                _resident((1, D), layer), _resident((D, QKVUV), layer), _resident((1, GMW), layer)]
    in_specs += [pl.BlockSpec(memory_space=pl.ANY)] * n_alias
    stacked = jax.ShapeDtypeStruct((DEPTH * N_S, 2 * GW), F32)
    out_shape = [jax.ShapeDtypeStruct((N_S, 3 * ATT), F32), stacked, stacked, stacked,
                 jax.ShapeDtypeStruct((N_S, GMW), F32),
                 jax.ShapeDtypeStruct((DEPTH * N_S, GMW), F32)]
    lay = lambda i: (layer, 0)
    out_specs = [pl.BlockSpec((N_S, 3 * ATT), lambda i: (0, 0)),
                 pl.BlockSpec((N_S, 2 * GW), lay), pl.BlockSpec((N_S, 2 * GW), lay),
                 pl.BlockSpec((N_S, 2 * GW), lay),
                 pl.BlockSpec((N_S, GMW), lambda i: (0, 0)),
                 pl.BlockSpec((N_S, GMW), lay)]
    args = [x, gain, w, vnorm] + ([] if prev is None else list(prev))
    aliases = {4: 1, 5: 2, 6: 3, 7: 5} if prev is not None else {}
    return pl.pallas_call(
        functools.partial(_inproj_sample_body, n_alias=n_alias),
        out_shape=out_shape,
        grid=(1,),
        in_specs=in_specs,
        out_specs=out_specs,
        input_output_aliases=aliases,
        compiler_params=_params(1),
        name="inproj_sample",
    )(*args)


def _alibi_slopes():
    h = np.arange(1, NG * NH + 1, dtype=np.float32)
    return np.power(np.float32(2.0), -8.0 * h / (NG * NH)).astype(np.float32).reshape(NG, NH)


def _head_masks():
    lane = lax.broadcasted_iota(jnp.int32, (1, GW), 1)
    return [(lane >= h * HD) & (lane < (h + 1) * HD) for h in range(NH)]


def _stack_heads(q, masks):
    return jnp.concatenate([jnp.where(m, q, jnp.zeros_like(q)) for m in masks], axis=0)


def _unstack_heads(x, masks, rows):
    out = jnp.zeros((rows, GW), F32)
    for h, m in enumerate(masks):
        out = jnp.where(m, x[h * rows:(h + 1) * rows], out)
    return out


def _softmax_parts(parts, values, value_dots):
    m = functools.reduce(jnp.maximum, [jnp.max(s, axis=-1, keepdims=True) for s in parts])
    ps = [jnp.exp(s - m) for s in parts]
    l = functools.reduce(jnp.add, [jnp.sum(p, axis=-1, keepdims=True) for p in ps])
    o = functools.reduce(jnp.add, [dot(p.astype(BF16), v) for p, v, dot in zip(ps, values, value_dots)])
    return o / l, m + jnp.log(l)


def _merge_groups(outs, lses):
    m = jnp.maximum(jnp.maximum(lses[0], lses[1]), lses[2])
    es = [jnp.exp(l - m) for l in lses]
    tot = es[0] + es[1] + es[2]
    return (es[0] / tot) * outs[0] + (es[1] / tot) * outs[1] + (es[2] / tot) * outs[2]


def _prompt_bias_table():
    slopes = _alibi_slopes()
    a = np.arange(KB)[:, None]
    c = np.arange(KB)[None, :]
    tab = np.empty((NG, 2, NH * KB, KB), np.float32)
    for g in range(NG):
        for part, dist in enumerate((KB + a - c, a - c)):
            valid = (dist >= 0) & (dist <= KB)
            for h in range(NH):
                bias = -slopes[g, h] * (DILS[g] * dist).astype(np.float32)
                tab[g, part, h * KB:(h + 1) * KB] = np.where(valid, bias, np.float32(NEG))
    return tab


def _sample_bias_tables():
    slopes = _alibi_slopes()
    j = np.arange(SPAD)[:, None]
    tabs = []
    for g in range(NG):
        w, d = WINDOWS[g], DILS[g]
        c = np.arange(w)[None, :]
        cn = np.arange(KB)[None, :]
        dist = np.concatenate([np.broadcast_to(w + j - c, (SPAD, w)),
                               np.broadcast_to(j - cn, (SPAD, KB))], axis=1)
        real = np.concatenate([np.ones((SPAD, w), bool), np.broadcast_to(cn < T_S, (SPAD, KB))], axis=1)
        valid = (j < T_S) & real & (dist >= 0) & (dist <= w) & (dist % d == 0)
        pad_row = np.broadcast_to(j >= T_S, dist.shape)
        tab = np.empty((NH * SPAD, dist.shape[1]), np.float32)
        for h in range(NH):
            bias = -slopes[g, h] * dist.astype(np.float32)
            blk = np.where(valid, bias, np.float32(NEG))
            tab[h * SPAD:(h + 1) * SPAD] = np.where(pad_row, np.float32(0.0), blk)
        tabs.append(tab)
    return tabs


def _attn_prompt_body(q1, k1c, k1p, v1c, v1p, q2, k2c, k2p, v2c, v2p, q3, k3, v3, bias_ref,
                      o1, l1, o2, l2, o3, l3):
    s = pl.program_id(1)
    masks = _head_masks()

    def rows(ref, start, stride):
        if stride == 1:
            parts = [ref[sl] for sl in range(SLABS)]
        else:
            parts = [ref[sl, pl.ds(start, KB, stride=stride), :] for sl in range(SLABS)]
        return jnp.concatenate(parts, axis=1).astype(BF16)

    def put(ref, val, start, stride):
        for sl in range(SLABS):
            piece = val[:, sl * LANES:(sl + 1) * LANES]
            if stride == 1:
                ref[sl] = piece
            else:
                ref[sl, pl.ds(start, KB, stride=stride), :] = piece

    def band(g, q_ref, kc, kp, vc, vp, start, has_prev, o_ref, l_ref):
        d = DILS[g]
        qs = _stack_heads(rows(q_ref, start, d), masks)
        parts = [_dot_nt(qs, rows(kc, start, d)) + bias_ref[g, 1]]
        values = [rows(vc, start, d)]
        if kp is not None:
            sp = _dot_nt(qs, rows(kp, start, d)) + bias_ref[g, 0]
            parts.append(jnp.where(has_prev, sp, NEG))
            values.append(rows(vp, start, d))
        o, lse = _softmax_parts(parts, values, [_dot] * len(parts))
        put(o_ref, _unstack_heads(o, masks, KB), start, d)
        put(l_ref, _unstack_heads(lse, masks, KB), start, d)

    band(0, q1, k1c, k1p, v1c, v1p, 0, s > 0, o1, l1)
    band(1, q2, k2c, k2p, v2c, v2p, lax.rem(s, DILS[1]), s >= DILS[1], o2, l2)
    band(2, q3, k3, None, v3, None, s, None, o3, l3)


def _attn_prompt(qkv, bias):
    n_steps = S_P // KB
    assert n_steps == DILS[2] and TM == KB * DILS[1]

    def spec(rows, sec, g, row_blk):
        return pl.BlockSpec((None, SLABS, rows, LANES), lambda b, s: (b, sec * NG + g, row_blk(s), 0))

    prev = lambda f: (lambda s: jnp.maximum(f(s) - 1, 0))
    blk1 = lambda s: s
    blk2 = lambda s: s // DILS[1]
    blk3 = lambda s: 0
    in_specs = [spec(KB, 0, 0, blk1), spec(KB, 1, 0, blk1), spec(KB, 1, 0, prev(blk1)),
                spec(KB, 2, 0, blk1), spec(KB, 2, 0, prev(blk1)),
                spec(TM, 0, 1, blk2), spec(TM, 1, 1, blk2), spec(TM, 1, 1, prev(blk2)),
                spec(TM, 2, 1, blk2), spec(TM, 2, 1, prev(blk2)),
                spec(S_P, 0, 2, blk3), spec(S_P, 1, 2, blk3), spec(S_P, 2, 2, blk3),
                _const(bias.shape)]
    out = jax.ShapeDtypeStruct((B_P, SLABS, S_P, LANES), F32)
    ospec = lambda rows, row_blk: pl.BlockSpec((None, SLABS, rows, LANES), lambda b, s: (b, 0, row_blk(s), 0))
    out_specs = [ospec(KB, blk1)] * 2 + [ospec(TM, blk2)] * 2 + [ospec(S_P, blk3)] * 2
    return pl.pallas_call(
        _attn_prompt_body,
        out_shape=[out] * 6,
        grid=(B_P, n_steps),
        in_specs=in_specs,
        out_specs=out_specs,
        compiler_params=_params(2),
        name="attn_prompt",
    )(*([qkv] * 13), bias)


def _attn_sample_body(z_ref, c1_ref, c2_ref, c3_ref, b1_ref, b2_ref, b3_ref, att_ref):
    masks = _head_masks()
    z = z_ref[...]
    pad = jnp.zeros((KB - SPAD, GW), F32)
    outs, lses = [], []
    for g, (c_ref, b_ref) in enumerate(((c1_ref, b1_ref), (c2_ref, b2_ref), (c3_ref, b3_ref))):
        w = WINDOWS[g]
        q = z[:, g * GW:(g + 1) * GW]
        k_new = jnp.concatenate([z[:, ATT + g * GW:ATT + (g + 1) * GW], pad], axis=0).astype(BF16)
        v_new = jnp.concatenate([z[:, 2 * ATT + g * GW:2 * ATT + (g + 1) * GW], pad], axis=0).astype(BF16)
        k_old = c_ref[:GW, :].astype(BF16)
        v_old = c_ref[GW:, :].astype(BF16)
        qs = _stack_heads(q, masks).astype(BF16)
        parts = [_dot(qs, k_old) + b_ref[:, :w], _dot_nt(qs, k_new) + b_ref[:, w:]]
        o, lse = _softmax_parts(parts, [v_old, v_new], [_dot_nt, _dot])
        outs.append(_unstack_heads(o, masks, SPAD))
        lses.append(_unstack_heads(lse, masks, SPAD))
    att_ref[...] = _merge_groups(outs, lses)


def _attn_sample(z, caches, tabs, layer):
    zp = jnp.pad(z.reshape(B_S, T_S, 3 * ATT), ((0, 0), (0, SPAD - T_S), (0, 0)))
    cs = [c.transpose(0, 1, 3, 4, 5, 2).reshape(DEPTH, B_S, 2 * GW, w) for c, w in zip(caches, WINDOWS)]
    seq = lambda b: (layer, b, 0, 0)
    in_specs = [pl.BlockSpec((None, SPAD, 3 * ATT), lambda b: (b, 0, 0))]
    in_specs += [pl.BlockSpec((None, None, 2 * GW, w), seq) for w in WINDOWS]
    in_specs += [_const(t.shape) for t in tabs]
    att = pl.pallas_call(
        _attn_sample_body,
        out_shape=jax.ShapeDtypeStruct((B_S, SPAD, GW), F32),
        grid=(B_S,),
        in_specs=in_specs,
        out_specs=pl.BlockSpec((None, SPAD, GW), lambda b: (b, 0, 0)),
        compiler_params=_params(1),
        name="attn_sample",
    )(zp, *cs, *tabs)
    return att[:, :T_S].reshape(N_S, GW)


def _mix_out(x, h, att, spat, wgate_ref, pa_ref, ps_ref, wo_ref):
    ga = jax.nn.sigmoid(_dot(h, wgate_ref[:, :D]))
    gb = jax.nn.sigmoid(_dot(h, wgate_ref[:, D:]))
    mix = ga * _dot(att.astype(BF16), pa_ref[...]) + gb * _dot(spat, ps_ref[...])
    return x + _dot(mix.astype(BF16), wo_ref[...])


def _outproj_prompt_body(x_ref, g_ref, wgate_ref, o1, l1, o2, l2, o3, l3, u_ref, vs_ref,
                         ws_ref, b_ref, pa_ref, ps_ref, wo_ref, out_ref, spat_scr):
    x = x_ref[...]
    h = _rms(x, g_ref[...]).astype(BF16)
    wide = lambda ref: jnp.concatenate([ref[sl] for sl in range(SLABS)], axis=1)
    att = _merge_groups([wide(o1), wide(o2), wide(o3)], [wide(l1), wide(l2), wide(l3)])
    row = lax.broadcasted_iota(jnp.int32, (CHUNK, CHUNK), 0)
    col = lax.broadcasted_iota(jnp.int32, (CHUNK, CHUNK), 1)
    for g in range(GG):
        w = jnp.where(col <= row, ws_ref[g], 0.0).astype(BF16)
        gs = slice(g * GC, (g + 1) * GC)
        for c in range(TM // CHUNK):
            rs = slice(c * CHUNK, (c + 1) * CHUNK)
            zc = _dot(w, vs_ref[rs, gs]) + b_ref[g]
            spat_scr[rs, gs] = (u_ref[rs, gs] * zc).astype(BF16)
    out_ref[...] = _mix_out(x, h, att, spat_scr[...], wgate_ref, pa_ref, ps_ref, wo_ref)


def _outproj_prompt(x, gain, wgate, attn_outs, u, vs, ws, bias, pa, ps, wo, layer):
    row = lambda i: (i, 0)
    tiles = S_P // TM
    slab = pl.BlockSpec((None, SLABS, TM, LANES), lambda i: (i // tiles, 0, lax.rem(i, tiles), 0))
    in_specs = ([pl.BlockSpec((TM, D), row), _resident((1, D), layer), _resident((D, 2 * D), layer)]
                + [slab] * 6
                + [pl.BlockSpec((TM, GMW), row), pl.BlockSpec((TM, GMW), row),
                   _resident((GG, CHUNK, CHUNK), layer), _resident((GG, CHUNK, 1), layer),
                   _resident((GW, D), layer), _resident((GMW, D), layer), _resident((D, D), layer)])
    return pl.pallas_call(
        _outproj_prompt_body,
        out_shape=jax.ShapeDtypeStruct((N_P, D), F32),
        grid=(N_P // TM,),
        in_specs=in_specs,
        out_specs=pl.BlockSpec((TM, D), row),
        scratch_shapes=[pltpu.VMEM((TM, GMW), BF16)],
        compiler_params=_params(1),
        name="outproj_prompt",
    )(x, gain, wgate, *attn_outs, u, vs, ws, bias, pa, ps, wo)


def _outproj_sample_body(x_ref, g_ref, wgate_ref, att_ref, u_ref, vs_ref, wt_ref, b_ref,
                         pa_ref, ps_ref, wo_ref, out_ref):
    x = x_ref[...]
    h = _rms(x, g_ref[...]).astype(BF16)
    row = lax.broadcasted_iota(jnp.int32, (N_S, N_S), 0)
    col = lax.broadcasted_iota(jnp.int32, (N_S, N_S), 1)
    keep = (row // T_S == col // T_S) & (col <= row)
    u = u_ref[...]
    vs = vs_ref[...].astype(BF16)
    parts = []
    for g in range(GG):
        w = jnp.where(keep, wt_ref[g], 0.0).astype(BF16)
        gs = slice(g * GC, (g + 1) * GC)
        parts.append(u[:, gs] * (_dot(w, vs[:, gs]) + b_ref[g]))
    spat = jnp.concatenate(parts, axis=1).astype(BF16)
    out_ref[...] = _mix_out(x, h, att_ref[...], spat, wgate_ref, pa_ref, ps_ref, wo_ref)


def _outproj_sample(x, gain, wgate, att, u, vs_all, wt, bias, pa, ps, wo, layer):
    zero = lambda i: (0, 0)
    in_specs = [pl.BlockSpec((N_S, D), zero), _resident((1, D), layer), _resident((D, 2 * D), layer),
                pl.BlockSpec((N_S, GW), zero), pl.BlockSpec((N_S, GMW), zero),
                pl.BlockSpec((N_S, GMW), lambda i: (layer, 0)),
                _resident((GG, N_S, N_S), layer), _resident((GG, N_S, 1), layer),
                _resident((GW, D), layer), _resident((GMW, D), layer), _resident((D, D), layer)]
    return pl.pallas_call(
        _outproj_sample_body,
        out_shape=jax.ShapeDtypeStruct((N_S, D), F32),
        grid=(1,),
        in_specs=in_specs,
        out_specs=pl.BlockSpec((N_S, D), zero),
        compiler_params=_params(1),
        name="outproj_sample",
    )(x, gain, wgate, att, u, vs_all, wt, bias, pa, ps, wo)


def kernel(x_prompt, x_sample, cache_kv_w128, cache_kv_w512, cache_kv_w2048, ffn1_norm, ffn1_gate, ffn1_up, ffn1_down, mix_norm, w_in, gmlp_v_norm, gmlp_ws, gmlp_bias, proj_att, proj_spatial, w_out, ffn2_norm, ffn2_gate, ffn2_up, ffn2_down, final_norm):
    caches = (cache_kv_w128, cache_kv_w512, cache_kv_w2048)
    xp = x_prompt.reshape(N_P, D)
    xs = x_sample.reshape(N_S, D)

    cb = 256
    f1g, f1u = _cast(ffn1_gate, cb, 0, DFF // cb), _cast(ffn1_up, cb, 0, DFF // cb)
    f2g, f2u = _cast(ffn2_gate, cb, 0, DFF // cb), _cast(ffn2_up, cb, 0, DFF // cb)
    f1d, f2d = _cast(ffn1_down, cb, 0, D // cb), _cast(ffn2_down, cb, 0, D // cb)
    w_qkvuv = _cast(w_in, cb, 0, QKVUV // cb)
    w_gate = _cast(w_in, cb, QKVUV // cb, 2 * D // cb)
    pa, ps, wo = _cast(proj_att, cb, 0, D // cb), _cast(proj_spatial, cb, 0, D // cb), _cast(w_out, cb, 0, D // cb)

    n1 = ffn1_norm.reshape(DEPTH, 1, D)
    n2 = ffn2_norm.reshape(DEPTH, 1, D)
    nm = mix_norm.reshape(DEPTH, 1, D)
    nv = gmlp_v_norm.reshape(DEPTH, 1, GMW)
    nf = final_norm.reshape(1, D)
    bias_p = gmlp_bias.reshape(DEPTH, GG, CHUNK, 1)
    wt_s = jnp.tile(gmlp_ws[:, :, :T_S, :T_S], (1, 1, B_S, B_S))
    bias_s = jnp.tile(gmlp_bias[:, :, :T_S], (1, 1, B_S)).reshape(DEPTH, GG, N_S, 1)

    attn_bias = jnp.asarray(_prompt_bias_table())
    sample_tabs = [jnp.asarray(t) for t in _sample_bias_tables()]

    kv_p = kv_s = None
    for l in range(DEPTH):
        last = l == DEPTH - 1
        xp = _ffn(xp, n1, f1g, f1u, f1d, l)
        qkv, kv1, kv2, kv3, u, vs = _inproj_prompt(xp, nm, w_qkvuv, nv, l, kv_p)
        kv_p = (kv1, kv2, kv3)
        attn_outs = _attn_prompt(qkv, attn_bias)
        xp = _outproj_prompt(xp, nm, w_gate, attn_outs, u, vs, gmlp_ws, bias_p, pa, ps, wo, l)
        xp = _ffn(xp, n2, f2g, f2u, f2d, l, nf if last else None)
        xs = _ffn(xs, n1, f1g, f1u, f1d, l)
        z_s, ks1, ks2, ks3, u_s, vs_s = _inproj_sample(xs, nm, w_qkvuv, nv, l, kv_s)
        kv_s = (ks1, ks2, ks3, vs_s)
        att_s = _attn_sample(z_s, caches, sample_tabs, l)
        xs = _outproj_sample(xs, nm, w_gate, att_s, u_s, vs_s, wt_s, bias_s, pa, ps, wo, l)
        xs = _ffn(xs, n2, f2g, f2u, f2d, l, nf if last else None)

    kv_out = [k.reshape(DEPTH, B_P, 2, NH, HD, w).transpose(0, 1, 5, 2, 3, 4) for k, w in zip(kv_p, WINDOWS)]
    ks_shape = (DEPTH, B_S, T_S, 2, NH, HD)
    return (xp.reshape(B_P, S_P, D), xs.reshape(B_S, T_S, D), kv_out[0], kv_out[1], kv_out[2],
            kv_s[0].reshape(ks_shape), kv_s[1].reshape(ks_shape), kv_s[2].reshape(ks_shape),
            kv_s[3].reshape(DEPTH, B_S, T_S, GMW))
```

```python
import functools

import numpy as np
import jax
import jax.numpy as jnp
from jax import lax
from jax.experimental import pallas as pl
from jax.experimental.pallas import tpu as pltpu

F32 = jnp.float32
BF16 = jnp.bfloat16

D = 1024
DEPTH = 2
B_P, S_P = 8, 2048
B_S, T_S = 32, 4
N_P = B_P * S_P
N_S = B_S * T_S
HD, NH, NG = 64, 4, 3
GW = NH * HD
ATT = NG * GW
WINDOWS = (128, 512, 2048)
DILS = (1, 4, 16)
KB = 128
CHUNK = 128
GG, GC = 4, 128
GMW = GG * GC
DFF = 2816
FC = 256
QKVUV = 3 * ATT + 2 * GMW
EPS = 1e-6
NEG = -1e30
TM = 512
SPAD = 16
LANES = 128
SLABS = GW // LANES
N_SLAB = 3 * ATT // LANES
VMEM_LIMIT = 56 * 1024 * 1024

_NT = (((1,), (1,)), ((), ()))


def _rms(x, gain):
    r = lax.rsqrt(jnp.mean(x * x, axis=-1, keepdims=True) + EPS)
    return (x * r) * gain


def _dot(a, b):
    return jnp.dot(a, b, preferred_element_type=F32)


def _dot_nt(a, b):
    return lax.dot_general(a, b, _NT, preferred_element_type=F32)


def _params(n_axes):
    return pltpu.CompilerParams(dimension_semantics=("arbitrary",) * n_axes,
                                vmem_limit_bytes=VMEM_LIMIT)


def _resident(shape, layer):
    nd = len(shape)
    return pl.BlockSpec((None,) + shape, lambda *_: (layer,) + (0,) * nd,
                        pipeline_mode=pl.Buffered(1))


def _const(shape):
    nd = len(shape)
    return pl.BlockSpec(shape, lambda *_: (0,) * nd, pipeline_mode=pl.Buffered(1))


def _cast_body(x_ref, o_ref):
    o_ref[...] = x_ref[...].astype(o_ref.dtype)


def _cast(w, col_blk, first_blk, n_blk):
    depth, rows, _ = w.shape
    return pl.pallas_call(
        _cast_body,
        out_shape=jax.ShapeDtypeStruct((depth, rows, n_blk * col_blk), BF16),
        grid=(depth, n_blk),
        in_specs=[pl.BlockSpec((None, rows, col_blk), lambda l, j: (l, 0, j + first_blk))],
        out_specs=pl.BlockSpec((None, rows, col_blk), lambda l, j: (l, 0, j)),
        compiler_params=_params(2),
        name="cast_bf16",
    )(w)


def _ffn_body(*refs, final):
    if final:
        x_ref, g_ref, wg_ref, wu_ref, wd_ref, fg_ref, o_ref, a_scr = refs
    else:
        x_ref, g_ref, wg_ref, wu_ref, wd_ref, o_ref, a_scr = refs
    x = x_ref[...]
    h = _rms(x, g_ref[...]).astype(BF16)
    for c in range(DFF // FC):
        cs = slice(c * FC, (c + 1) * FC)
        gate = _dot(h, wg_ref[:, cs])
        up = _dot(h, wu_ref[:, cs])
        a_scr[:, cs] = (jax.nn.silu(gate) * up).astype(BF16)
    y = x + 0.5 * _dot(a_scr[...], wd_ref[...])
    if final:
        y = _rms(y, fg_ref[...])
    o_ref[...] = y


def _ffn(x, gain, wg, wu, wd, layer, final_gain=None):
    n = x.shape[0]
    tm = min(TM, n)
    final = final_gain is not None
    in_specs = [pl.BlockSpec((tm, D), lambda i: (i, 0)),
                _resident((1, D), layer),
                _resident((D, DFF), layer), _resident((D, DFF), layer), _resident((DFF, D), layer)]
    args = [x, gain, wg, wu, wd]
    if final:
        in_specs.append(_const((1, D)))
        args.append(final_gain)
    return pl.pallas_call(
        functools.partial(_ffn_body, final=final),
        out_shape=jax.ShapeDtypeStruct((n, D), F32),
        grid=(n // tm,),
        in_specs=in_specs,
        out_specs=pl.BlockSpec((tm, D), lambda i: (i, 0)),
        scratch_shapes=[pltpu.VMEM((tm, DFF), BF16)],
        compiler_params=_params(1),
        name="ffn_final" if final else "ffn",
    )(*args)


def _project(h, w_ref, sec, g):
    c0 = sec * ATT + g * GW
    z = _dot(h, w_ref[:, c0:c0 + GW])
    if sec == 0:
        z = z * (HD ** -0.5)
    return z


def _gmlp_inputs(h, w_ref, vn_ref, u_ref, vs_ref):
    u_ref[...] = jax.nn.gelu(_dot(h, w_ref[:, 3 * ATT:3 * ATT + GMW]))
    gv = jax.nn.gelu(_dot(h, w_ref[:, 3 * ATT + GMW:QKVUV]))
    vs_ref[...] = _rms(gv, vn_ref[...]).astype(vs_ref.dtype)


def _inproj_prompt_body(*refs, n_alias):
    x_ref, g_ref, w_ref, vn_ref = refs[:4]
    qkv_ref, kv1_ref, kv2_ref, kv3_ref, u_ref, vs_ref = refs[4 + n_alias:]
    last_tile = lax.rem(pl.program_id(0), S_P // TM) == S_P // TM - 1
    h = _rms(x_ref[...], g_ref[...]).astype(BF16)
    for sec in range(3):
        for g in range(NG):
            z = _project(h, w_ref, sec, g)
            for sl in range(SLABS):
                qkv_ref[(sec * NG + g) * SLABS + sl] = z[:, sl * LANES:(sl + 1) * LANES]
            if sec == 0:
                continue
            rows = slice((sec - 1) * GW, sec * GW)
            if g == 2:
                kv3_ref[rows, :] = z.T
            elif g == 1:
                @pl.when(last_tile)
                def _():
                    kv2_ref[rows, :] = z.T
            else:
                @pl.when(last_tile)
                def _():
                    kv1_ref[rows, :] = z[TM - WINDOWS[0]:, :].T
    _gmlp_inputs(h, w_ref, vn_ref, u_ref, vs_ref)


def _inproj_prompt(x, gain, w, vnorm, layer, prev):
    tiles = S_P // TM
    n_alias = 0 if prev is None else 3
    in_specs = [pl.BlockSpec((TM, D), lambda i: (i, 0)),
                _resident((1, D), layer), _resident((D, QKVUV), layer), _resident((1, GMW), layer)]
    in_specs += [pl.BlockSpec(memory_space=pl.ANY)] * n_alias
    out_shape = [jax.ShapeDtypeStruct((B_P, N_SLAB, S_P, LANES), F32),
                 jax.ShapeDtypeStruct((DEPTH * B_P, 2 * GW, WINDOWS[0]), F32),
                 jax.ShapeDtypeStruct((DEPTH * B_P, 2 * GW, WINDOWS[1]), F32),
                 jax.ShapeDtypeStruct((DEPTH * B_P, 2 * GW, WINDOWS[2]), F32),
                 jax.ShapeDtypeStruct((N_P, GMW), F32),
                 jax.ShapeDtypeStruct((N_P, GMW), BF16)]
    seq = lambda i: (layer * B_P + i // tiles, 0, 0)
    out_specs = [pl.BlockSpec((None, N_SLAB, TM, LANES), lambda i: (i // tiles, 0, lax.rem(i, tiles), 0)),
                 pl.BlockSpec((None, 2 * GW, WINDOWS[0]), seq),
                 pl.BlockSpec((None, 2 * GW, WINDOWS[1]), seq),
                 pl.BlockSpec((None, 2 * GW, TM), lambda i: (layer * B_P + i // tiles, 0, lax.rem(i, tiles))),
                 pl.BlockSpec((TM, GMW), lambda i: (i, 0)),
                 pl.BlockSpec((TM, GMW), lambda i: (i, 0))]
    args = [x, gain, w, vnorm] + ([] if prev is None else list(prev))
    return pl.pallas_call(
        functools.partial(_inproj_prompt_body, n_alias=n_alias),
        out_shape=out_shape,
        grid=(N_P // TM,),
        in_specs=in_specs,
        out_specs=out_specs,
        input_output_aliases={4 + k: 1 + k for k in range(n_alias)},
        compiler_params=_params(1),
        name="inproj_prompt",
    )(*args)


def _inproj_sample_body(*refs, n_alias):
    x_ref, g_ref, w_ref, vn_ref = refs[:4]
    qkv_ref, kv1_ref, kv2_ref, kv3_ref, u_ref, vs_ref = refs[4 + n_alias:]
    kv_refs = (kv1_ref, kv2_ref, kv3_ref)
    h = _rms(x_ref[...], g_ref[...]).astype(BF16)
    for sec in range(3):
        for g in range(NG):
            z = _project(h, w_ref, sec, g)
            qkv_ref[:, sec * ATT + g * GW:sec * ATT + (g + 1) * GW] = z
            if sec > 0:
                kv_refs[g][:, (sec - 1) * GW:sec * GW] = z
    _gmlp_inputs(h, w_ref, vn_ref, u_ref, vs_ref)


def _inproj_sample(x, gain, w, vnorm, layer, prev):
    n_alias = 0 if prev is None else 4
    in_specs = [pl.BlockSpec((N_S, D), lambda i: (0, 0)),
                _resident((1, D), layer), _resident((D, QKVUV), layer), _resident((1, GMW), layer)]
    in_specs += [pl.BlockSpec(memory_space=pl.ANY)] * n_alias
    stacked = jax.ShapeDtypeStruct((DEPTH * N_S, 2 * GW), F32)
    out_shape = [jax.ShapeDtypeStruct((N_S, 3 * ATT), F32), stacked, stacked, stacked,
                 jax.ShapeDtypeStruct((N_S, GMW), F32),
                 jax.ShapeDtypeStruct((DEPTH * N_S, GMW), F32)]
    lay = lambda i: (layer, 0)
    out_specs = [pl.BlockSpec((N_S, 3 * ATT), lambda i: (0, 0)),
                 pl.BlockSpec((N_S, 2 * GW), lay), pl.BlockSpec((N_S, 2 * GW), lay),
                 pl.BlockSpec((N_S, 2 * GW), lay),
                 pl.BlockSpec((N_S, GMW), lambda i: (0, 0)),
                 pl.BlockSpec((N_S, GMW), lay)]
    args = [x, gain, w, vnorm] + ([] if prev is None else list(prev))
    aliases = {4: 1, 5: 2, 6: 3, 7: 5} if prev is not None else {}
    return pl.pallas_call(
        functools.partial(_inproj_sample_body, n_alias=n_alias),
        out_shape=out_shape,
        grid=(1,),
        in_specs=in_specs,
        out_specs=out_specs,
        input_output_aliases=aliases,
        compiler_params=_params(1),
        name="inproj_sample",
    )(*args)


def _alibi_slopes():
    h = np.arange(1, NG * NH + 1, dtype=np.float32)
    return np.power(np.float32(2.0), -8.0 * h / (NG * NH)).astype(np.float32).reshape(NG, NH)


def _head_masks():
    lane = lax.broadcasted_iota(jnp.int32, (1, GW), 1)
    return [(lane >= h * HD) & (lane < (h + 1) * HD) for h in range(NH)]


def _stack_heads(q, masks):
    return jnp.concatenate([jnp.where(m, q, jnp.zeros_like(q)) for m in masks], axis=0)


def _unstack_heads(x, masks, rows):
    out = jnp.zeros((rows, GW), F32)
    for h, m in enumerate(masks):
        out = jnp.where(m, x[h * rows:(h + 1) * rows], out)
    return out


def _softmax_parts(parts, values, value_dots):
    m = functools.reduce(jnp.maximum, [jnp.max(s, axis=-1, keepdims=True) for s in parts])
    ps = [jnp.exp(s - m) for s in parts]
    l = functools.reduce(jnp.add, [jnp.sum(p, axis=-1, keepdims=True) for p in ps])
    o = functools.reduce(jnp.add, [dot(p.astype(BF16), v) for p, v, dot in zip(ps, values, value_dots)])
    return o / l, m + jnp.log(l)


def _merge_groups(outs, lses):
    m = jnp.maximum(jnp.maximum(lses[0], lses[1]), lses[2])
    es = [jnp.exp(l - m) for l in lses]
    tot = es[0] + es[1] + es[2]
    return (es[0] / tot) * outs[0] + (es[1] / tot) * outs[1] + (es[2] / tot) * outs[2]


def _prompt_bias_table():
    slopes = _alibi_slopes()
    a = np.arange(KB)[:, None]
    c = np.arange(2 * KB)[None, :]
    dist = KB + a - c
    valid = (dist >= 0) & (dist <= KB)
    tab = np.empty((NG, NH * KB, 2 * KB), np.float32)
    for g in range(NG):
        for h in range(NH):
            bias = -slopes[g, h] * (DILS[g] * dist).astype(np.float32)
            tab[g, h * KB:(h + 1) * KB] = np.where(valid, bias, np.float32(NEG))
    return tab


def _sample_bias_tables():
    slopes = _alibi_slopes()
    j = np.arange(SPAD)[:, None]
    tabs = []
    for g in range(NG):
        w, d = WINDOWS[g], DILS[g]
        c = np.arange(w)[None, :]
        cn = np.arange(KB)[None, :]
        dist = np.concatenate([np.broadcast_to(w + j - c, (SPAD, w)),
                               np.broadcast_to(j - cn, (SPAD, KB))], axis=1)
        real = np.concatenate([np.ones((SPAD, w), bool), np.broadcast_to(cn < T_S, (SPAD, KB))], axis=1)
        valid = (j < T_S) & real & (dist >= 0) & (dist <= w) & (dist % d == 0)
        pad_row = np.broadcast_to(j >= T_S, dist.shape)
        tab = np.empty((NH * SPAD, dist.shape[1]), np.float32)
        for h in range(NH):
            bias = -slopes[g, h] * dist.astype(np.float32)
            blk = np.where(valid, bias, np.float32(NEG))
            tab[h * SPAD:(h + 1) * SPAD] = np.where(pad_row, np.float32(0.0), blk)
        tabs.append(tab)
    return tabs


def _attn_sample_body(z_ref, c1_ref, c2_ref, c3_ref, b1_ref, b2_ref, b3_ref, att_ref):
    masks = _head_masks()
    z = z_ref[...]
    pad = jnp.zeros((KB - SPAD, GW), F32)
    outs, lses = [], []
    for g, (c_ref, b_ref) in enumerate(((c1_ref, b1_ref), (c2_ref, b2_ref), (c3_ref, b3_ref))):
        w = WINDOWS[g]
        q = z[:, g * GW:(g + 1) * GW]
        k_new = jnp.concatenate([z[:, ATT + g * GW:ATT + (g + 1) * GW], pad], axis=0).astype(BF16)
        v_new = jnp.concatenate([z[:, 2 * ATT + g * GW:2 * ATT + (g + 1) * GW], pad], axis=0).astype(BF16)
        k_old = c_ref[:GW, :].astype(BF16)
        v_old = c_ref[GW:, :].astype(BF16)
        qs = _stack_heads(q, masks).astype(BF16)
        parts = [_dot(qs, k_old) + b_ref[:, :w], _dot_nt(qs, k_new) + b_ref[:, w:]]
        o, lse = _softmax_parts(parts, [v_old, v_new], [_dot_nt, _dot])
        outs.append(_unstack_heads(o, masks, SPAD))
        lses.append(_unstack_heads(lse, masks, SPAD))
    att_ref[...] = _merge_groups(outs, lses)


def _attn_sample(z, caches, tabs, layer):
    zp = jnp.pad(z.reshape(B_S, T_S, 3 * ATT), ((0, 0), (0, SPAD - T_S), (0, 0)))
    cs = [c.transpose(0, 1, 3, 4, 5, 2).reshape(DEPTH, B_S, 2 * GW, w) for c, w in zip(caches, WINDOWS)]
    seq = lambda b: (layer, b, 0, 0)
    in_specs = [pl.BlockSpec((None, SPAD, 3 * ATT), lambda b: (b, 0, 0))]
    in_specs += [pl.BlockSpec((None, None, 2 * GW, w), seq) for w in WINDOWS]
    in_specs += [_const(t.shape) for t in tabs]
    att = pl.pallas_call(
        _attn_sample_body,
        out_shape=jax.ShapeDtypeStruct((B_S, SPAD, GW), F32),
        grid=(B_S,),
        in_specs=in_specs,
        out_specs=pl.BlockSpec((None, SPAD, GW), lambda b: (b, 0, 0)),
        compiler_params=_params(1),
        name="attn_sample",
    )(zp, *cs, *tabs)
    return att[:, :T_S].reshape(N_S, GW)


def _mix_out(x, h, att, spat, wgate_ref, pa_ref, ps_ref, wo_ref):
    ga = jax.nn.sigmoid(_dot(h, wgate_ref[:, :D]))
    gb = jax.nn.sigmoid(_dot(h, wgate_ref[:, D:]))
    mix = ga * _dot(att.astype(BF16), pa_ref[...]) + gb * _dot(spat, ps_ref[...])
    return x + _dot(mix.astype(BF16), wo_ref[...])


def _mixer_out_body(x_ref, g_ref, wgate_ref, qkv_ref, kp1_ref, vp1_ref, kp2_ref, vp2_ref, kh3_ref, vh3_ref,
                    bias_ref, u_ref, vs_ref, ws_ref, b_ref, pa_ref, ps_ref, wo_ref, out_ref,
                    o_scr, l_scr, spat_scr):
    j = pl.program_id(1)
    masks = _head_masks()
    lane = lax.broadcasted_iota(jnp.int32, (1, 2 * KB), 1)
    first_tile_mask = (j > 0) | (lane >= KB)

    def wide(ref, base, rows):
        return jnp.concatenate([ref[base + sl, rows, :] for sl in range(SLABS)], axis=1)

    def tile(sec, g, rows):
        return wide(qkv_ref, (sec * NG + g) * SLABS, rows)

    def band(g, q, k, v, bias, rows, n):
        qs = _stack_heads(q.astype(BF16), masks)
        s = _dot_nt(qs, k.astype(BF16)) + bias
        m = jnp.max(s, axis=-1, keepdims=True)
        p = jnp.exp(s - m)
        l = jnp.sum(p, axis=-1, keepdims=True)
        o = _unstack_heads(_dot(p.astype(BF16), v.astype(BF16)) / l, masks, n)
        lse = _unstack_heads(m + jnp.log(l), masks, n)
        for sl in range(SLABS):
            o_scr[g * SLABS + sl, rows, :] = o[:, sl * LANES:(sl + 1) * LANES]
            l_scr[g * SLABS + sl, rows, :] = lse[:, sl * LANES:(sl + 1) * LANES]

    bias1 = bias_ref[0]
    for blk in range(TM // KB):
        own = pl.ds(blk * KB, KB)
        if blk == 0:
            k = jnp.concatenate([wide(kp1_ref, 0, slice(None)), tile(1, 0, own)], axis=0)
            v = jnp.concatenate([wide(vp1_ref, 0, slice(None)), tile(2, 0, own)], axis=0)
            bias = jnp.where(first_tile_mask, bias1, NEG)
        else:
            both = pl.ds((blk - 1) * KB, 2 * KB)
            k, v, bias = tile(1, 0, both), tile(2, 0, both), bias1
        band(0, tile(0, 0, own), k, v, bias, own, KB)

    bias2 = jnp.where(first_tile_mask, bias_ref[1], NEG)
    for r in range(DILS[1]):
        rows = pl.ds(r, KB, stride=DILS[1])
        k = jnp.concatenate([wide(kp2_ref, 0, rows), tile(1, 1, rows)], axis=0)
        v = jnp.concatenate([wide(vp2_ref, 0, rows), tile(2, 1, rows)], axis=0)
        band(1, tile(0, 1, rows), k, v, bias2, rows, KB)

    nq = TM // DILS[2]
    q0 = pl.multiple_of(j * nq, nq)
    bias3 = jnp.concatenate([bias_ref[2, pl.ds(h * KB + q0, nq), KB:] for h in range(NH)], axis=0)
    for r in range(DILS[2]):
        rows = pl.ds(r, nq, stride=DILS[2])
        keys = pl.ds(r, KB, stride=DILS[2])
        band(2, tile(0, 2, rows), wide(kh3_ref, 0, keys), wide(vh3_ref, 0, keys), bias3, rows, nq)

    full = slice(None)
    att = _merge_groups([wide(o_scr, g * SLABS, full) for g in range(NG)],
                        [wide(l_scr, g * SLABS, full) for g in range(NG)])

    x = x_ref[...]
    h = _rms(x, g_ref[...]).astype(BF16)
    row = lax.broadcasted_iota(jnp.int32, (CHUNK, CHUNK), 0)
    col = lax.broadcasted_iota(jnp.int32, (CHUNK, CHUNK), 1)
    for g in range(GG):
        w = jnp.where(col <= row, ws_ref[g], 0.0).astype(BF16)
        gs = slice(g * GC, (g + 1) * GC)
        for c in range(TM // CHUNK):
            rs = slice(c * CHUNK, (c + 1) * CHUNK)
            zc = _dot(w, vs_ref[rs, gs]) + b_ref[g]
            spat_scr[rs, gs] = (u_ref[rs, gs] * zc).astype(BF16)
    out_ref[...] = _mix_out(x, h, att, spat_scr[...], wgate_ref, pa_ref, ps_ref, wo_ref)


def _mixer_out(x, gain, wgate, qkv, bias_tab, u, vs, ws, bias, pa, ps, wo, layer):
    tiles = S_P // TM
    assert TM == KB * DILS[1] and S_P == KB * DILS[2]
    row = lambda b, j: (b * tiles + j, 0)

    def slabs(rows, sec, g, row_blk):
        return pl.BlockSpec((None, SLABS, rows, LANES), lambda b, j: (b, sec * NG + g, row_blk(j), 0))

    before1 = lambda j: jnp.maximum(j * (TM // KB) - 1, 0)
    before2 = lambda j: jnp.maximum(j - 1, 0)
    whole = lambda j: 0
    in_specs = [pl.BlockSpec((TM, D), row), _resident((1, D), layer), _resident((D, 2 * D), layer),
                pl.BlockSpec((None, N_SLAB, TM, LANES), lambda b, j: (b, 0, j, 0)),
                slabs(KB, 1, 0, before1), slabs(KB, 2, 0, before1),
                slabs(TM, 1, 1, before2), slabs(TM, 2, 1, before2),
                slabs(S_P, 1, 2, whole), slabs(S_P, 2, 2, whole),
                _const(bias_tab.shape),
                pl.BlockSpec((TM, GMW), row), pl.BlockSpec((TM, GMW), row),
                _resident((GG, CHUNK, CHUNK), layer), _resident((GG, CHUNK, 1), layer),
                _resident((GW, D), layer), _resident((GMW, D), layer), _resident((D, D), layer)]
    return pl.pallas_call(
        _mixer_out_body,
        out_shape=jax.ShapeDtypeStruct((N_P, D), F32),
        grid=(B_P, tiles),
        in_specs=in_specs,
        out_specs=pl.BlockSpec((TM, D), row),
        scratch_shapes=[pltpu.VMEM((NG * SLABS, TM, LANES), F32), pltpu.VMEM((NG * SLABS, TM, LANES), F32),
                        pltpu.VMEM((TM, GMW), BF16)],
        compiler_params=_params(2),
        name="mixer_out",
    )(x, gain, wgate, qkv, qkv, qkv, qkv, qkv, qkv, qkv, bias_tab, u, vs, ws, bias, pa, ps, wo)


def _outproj_sample_body(x_ref, g_ref, wgate_ref, att_ref, u_ref, vs_ref, wt_ref, b_ref,
                         pa_ref, ps_ref, wo_ref, out_ref):
    x = x_ref[...]
    h = _rms(x, g_ref[...]).astype(BF16)
    row = lax.broadcasted_iota(jnp.int32, (N_S, N_S), 0)
    col = lax.broadcasted_iota(jnp.int32, (N_S, N_S), 1)
    keep = (row // T_S == col // T_S) & (col <= row)
    u = u_ref[...]
    vs = vs_ref[...].astype(BF16)
    parts = []
    for g in range(GG):
        w = jnp.where(keep, wt_ref[g], 0.0).astype(BF16)
        gs = slice(g * GC, (g + 1) * GC)
        parts.append(u[:, gs] * (_dot(w, vs[:, gs]) + b_ref[g]))
    spat = jnp.concatenate(parts, axis=1).astype(BF16)
    out_ref[...] = _mix_out(x, h, att_ref[...], spat, wgate_ref, pa_ref, ps_ref, wo_ref)


def _outproj_sample(x, gain, wgate, att, u, vs_all, wt, bias, pa, ps, wo, layer):
    zero = lambda i: (0, 0)
    in_specs = [pl.BlockSpec((N_S, D), zero), _resident((1, D), layer), _resident((D, 2 * D), layer),
                pl.BlockSpec((N_S, GW), zero), pl.BlockSpec((N_S, GMW), zero),
                pl.BlockSpec((N_S, GMW), lambda i: (layer, 0)),
                _resident((GG, N_S, N_S), layer), _resident((GG, N_S, 1), layer),
                _resident((GW, D), layer), _resident((GMW, D), layer), _resident((D, D), layer)]
    return pl.pallas_call(
        _outproj_sample_body,
        out_shape=jax.ShapeDtypeStruct((N_S, D), F32),
        grid=(1,),
        in_specs=in_specs,
        out_specs=pl.BlockSpec((N_S, D), zero),
        compiler_params=_params(1),
        name="outproj_sample",
    )(x, gain, wgate, att, u, vs_all, wt, bias, pa, ps, wo)


def kernel(x_prompt, x_sample, cache_kv_w128, cache_kv_w512, cache_kv_w2048, ffn1_norm, ffn1_gate, ffn1_up, ffn1_down, mix_norm, w_in, gmlp_v_norm, gmlp_ws, gmlp_bias, proj_att, proj_spatial, w_out, ffn2_norm, ffn2_gate, ffn2_up, ffn2_down, final_norm):
    caches = (cache_kv_w128, cache_kv_w512, cache_kv_w2048)
    xp = x_prompt.reshape(N_P, D)
    xs = x_sample.reshape(N_S, D)

    cb = 256
    f1g, f1u = _cast(ffn1_gate, cb, 0, DFF // cb), _cast(ffn1_up, cb, 0, DFF // cb)
    f2g, f2u = _cast(ffn2_gate, cb, 0, DFF // cb), _cast(ffn2_up, cb, 0, DFF // cb)
    f1d, f2d = _cast(ffn1_down, cb, 0, D // cb), _cast(ffn2_down, cb, 0, D // cb)
    w_qkvuv = _cast(w_in, cb, 0, QKVUV // cb)
    w_gate = _cast(w_in, cb, QKVUV // cb, 2 * D // cb)
    pa, ps, wo = _cast(proj_att, cb, 0, D // cb), _cast(proj_spatial, cb, 0, D // cb), _cast(w_out, cb, 0, D // cb)

    n1 = ffn1_norm.reshape(DEPTH, 1, D)
    n2 = ffn2_norm.reshape(DEPTH, 1, D)
    nm = mix_norm.reshape(DEPTH, 1, D)
    nv = gmlp_v_norm.reshape(DEPTH, 1, GMW)
    nf = final_norm.reshape(1, D)
    bias_p = gmlp_bias.reshape(DEPTH, GG, CHUNK, 1)
    wt_s = jnp.tile(gmlp_ws[:, :, :T_S, :T_S], (1, 1, B_S, B_S))
    bias_s = jnp.tile(gmlp_bias[:, :, :T_S], (1, 1, B_S)).reshape(DEPTH, GG, N_S, 1)

    attn_bias = jnp.asarray(_prompt_bias_table())
    sample_tabs = [jnp.asarray(t) for t in _sample_bias_tables()]

    kv_p = kv_s = None
    for l in range(DEPTH):
        last = l == DEPTH - 1
        xp = _ffn(xp, n1, f1g, f1u, f1d, l)
        qkv, kv1, kv2, kv3, u, vs = _inproj_prompt(xp, nm, w_qkvuv, nv, l, kv_p)
        kv_p = (kv1, kv2, kv3)
        xp = _mixer_out(xp, nm, w_gate, qkv, attn_bias, u, vs, gmlp_ws, bias_p, pa, ps, wo, l)
        xp = _ffn(xp, n2, f2g, f2u, f2d, l, nf if last else None)
        xs = _ffn(xs, n1, f1g, f1u, f1d, l)
        z_s, ks1, ks2, ks3, u_s, vs_s = _inproj_sample(xs, nm, w_qkvuv, nv, l, kv_s)
        kv_s = (ks1, ks2, ks3, vs_s)
        att_s = _attn_sample(z_s, caches, sample_tabs, l)
        xs = _outproj_sample(xs, nm, w_gate, att_s, u_s, vs_s, wt_s, bias_s, pa, ps, wo, l)
        xs = _ffn(xs, n2, f2g, f2u, f2d, l, nf if last else None)

    kv_out = [k.reshape(DEPTH, B_P, 2, NH, HD, w).transpose(0, 1, 5, 2, 3, 4) for k, w in zip(kv_p, WINDOWS)]
    ks_shape = (DEPTH, B_S, T_S, 2, NH, HD)
    return (xp.reshape(B_P, S_P, D), xs.reshape(B_S, T_S, D), kv_out[0], kv_out[1], kv_out[2],
            kv_s[0].reshape(ks_shape), kv_s[1].reshape(ks_shape), kv_s[2].reshape(ks_shape),
            kv_s[3].reshape(DEPTH, B_S, T_S, GMW))
```

```python
import functools

import numpy as np
import jax
import jax.numpy as jnp
from jax import lax
from jax.experimental import pallas as pl
from jax.experimental.pallas import tpu as pltpu

F32 = jnp.float32
BF16 = jnp.bfloat16

D = 1024
DEPTH = 2
B_P, S_P = 8, 2048
B_S, T_S = 32, 4
N_P = B_P * S_P
N_S = B_S * T_S
HD, NH, NG = 64, 4, 3
GW = NH * HD
ATT = NG * GW
WINDOWS = (128, 512, 2048)
DILS = (1, 4, 16)
KB = 128
CHUNK = 128
GG, GC = 4, 128
GMW = GG * GC
DFF = 2816
FC = 256
QKVUV = 3 * ATT + 2 * GMW
EPS = 1e-6
NEG = -1e30
TM = 512
SPAD = 16
LANES = 128
SLABS = GW // LANES
N_SLAB = 3 * ATT // LANES
VMEM_LIMIT = 56 * 1024 * 1024

_NT = (((1,), (1,)), ((), ()))


def _rms(x, gain):
    r = lax.rsqrt(jnp.mean(x * x, axis=-1, keepdims=True) + EPS)
    return (x * r) * gain


def _dot(a, b):
    return jnp.dot(a, b, preferred_element_type=F32)


def _dot_nt(a, b):
    return lax.dot_general(a, b, _NT, preferred_element_type=F32)


def _params(n_axes):
    return pltpu.CompilerParams(dimension_semantics=("arbitrary",) * n_axes,
                                vmem_limit_bytes=VMEM_LIMIT)


def _resident(shape, layer):
    nd = len(shape)
    return pl.BlockSpec((None,) + shape, lambda *_: (layer,) + (0,) * nd,
                        pipeline_mode=pl.Buffered(1))


def _const(shape):
    nd = len(shape)
    return pl.BlockSpec(shape, lambda *_: (0,) * nd, pipeline_mode=pl.Buffered(1))


def _cast_body(x_ref, o_ref):
    o_ref[...] = x_ref[...].astype(o_ref.dtype)


def _cast(w, col_blk, first_blk, n_blk):
    depth, rows, _ = w.shape
    return pl.pallas_call(
        _cast_body,
        out_shape=jax.ShapeDtypeStruct((depth, rows, n_blk * col_blk), BF16),
        grid=(depth, n_blk),
        in_specs=[pl.BlockSpec((None, rows, col_blk), lambda l, j: (l, 0, j + first_blk))],
        out_specs=pl.BlockSpec((None, rows, col_blk), lambda l, j: (l, 0, j)),
        compiler_params=_params(2),
        name="cast_bf16",
    )(w)


def _ffn_body(*refs, final):
    if final:
        x_ref, g_ref, wg_ref, wu_ref, wd_ref, fg_ref, o_ref, a_scr = refs
    else:
        x_ref, g_ref, wg_ref, wu_ref, wd_ref, o_ref, a_scr = refs
    x = x_ref[...]
    h = _rms(x, g_ref[...]).astype(BF16)
    for c in range(DFF // FC):
        cs = slice(c * FC, (c + 1) * FC)
        gate = _dot(h, wg_ref[:, cs])
        up = _dot(h, wu_ref[:, cs])
        a_scr[:, cs] = (jax.nn.silu(gate) * up).astype(BF16)
    y = x + 0.5 * _dot(a_scr[...], wd_ref[...])
    if final:
        y = _rms(y, fg_ref[...])
    o_ref[...] = y


def _ffn(x, gain, wg, wu, wd, layer, final_gain=None):
    n = x.shape[0]
    tm = min(TM, n)
    final = final_gain is not None
    in_specs = [pl.BlockSpec((tm, D), lambda i: (i, 0)),
                _resident((1, D), layer),
                _resident((D, DFF), layer), _resident((D, DFF), layer), _resident((DFF, D), layer)]
    args = [x, gain, wg, wu, wd]
    if final:
        in_specs.append(_const((1, D)))
        args.append(final_gain)
    return pl.pallas_call(
        functools.partial(_ffn_body, final=final),
        out_shape=jax.ShapeDtypeStruct((n, D), F32),
        grid=(n // tm,),
        in_specs=in_specs,
        out_specs=pl.BlockSpec((tm, D), lambda i: (i, 0)),
        scratch_shapes=[pltpu.VMEM((tm, DFF), BF16)],
        compiler_params=_params(1),
        name="ffn_final" if final else "ffn",
    )(*args)


def _project(h, w_ref, sec, g):
    c0 = sec * ATT + g * GW
    z = _dot(h, w_ref[:, c0:c0 + GW])
    if sec == 0:
        z = z * (HD ** -0.5)
    return z


def _gmlp_inputs(h, w_ref, vn_ref, u_ref, vs_ref):
    u_ref[...] = jax.nn.gelu(_dot(h, w_ref[:, 3 * ATT:3 * ATT + GMW]))
    gv = jax.nn.gelu(_dot(h, w_ref[:, 3 * ATT + GMW:QKVUV]))
    vs_ref[...] = _rms(gv, vn_ref[...]).astype(vs_ref.dtype)


def _inproj_prompt_body(*refs, n_alias):
    x_ref, g_ref, w_ref, vn_ref = refs[:4]
    qkv_ref, kv1_ref, kv2_ref, kv3_ref, u_ref, vs_ref = refs[4 + n_alias:]
    h = _rms(x_ref[...], g_ref[...]).astype(BF16)
    _gmlp_inputs(h, w_ref, vn_ref, u_ref, vs_ref)
    for sec in range(3):
        for g in range(NG):
            z = _project(h, w_ref, sec, g)
            for sl in range(SLABS):
                qkv_ref[(sec * NG + g) * SLABS + sl] = z[:, sl * LANES:(sl + 1) * LANES]
            if sec == 0:
                continue
            rows = slice((sec - 1) * GW, sec * GW)
            if g == 2:
                kv3_ref[rows, :] = z.T
            elif g == 1:
                kv2_ref[rows, :] = z.T
            else:
                kv1_ref[rows, :] = z[TM - WINDOWS[0]:, :].T


def _inproj_prompt(x, gain, w, vnorm, layer, prev):
    tiles = S_P // TM
    n_alias = 0 if prev is None else 3
    in_specs = [pl.BlockSpec((TM, D), lambda i: (i, 0)),
                _resident((1, D), layer), _resident((D, QKVUV), layer), _resident((1, GMW), layer)]
    in_specs += [pl.BlockSpec(memory_space=pl.ANY)] * n_alias
    out_shape = [jax.ShapeDtypeStruct((B_P, N_SLAB, S_P, LANES), F32),
                 jax.ShapeDtypeStruct((DEPTH * B_P, 2 * GW, WINDOWS[0]), F32),
                 jax.ShapeDtypeStruct((DEPTH * B_P, 2 * GW, WINDOWS[1]), F32),
                 jax.ShapeDtypeStruct((DEPTH * B_P, 2 * GW, WINDOWS[2]), F32),
                 jax.ShapeDtypeStruct((N_P, GMW), F32),
                 jax.ShapeDtypeStruct((N_P, GMW), BF16)]
    seq = lambda i: (layer * B_P + i // tiles, 0, 0)
    out_specs = [pl.BlockSpec((None, N_SLAB, TM, LANES), lambda i: (i // tiles, 0, lax.rem(i, tiles), 0)),
                 pl.BlockSpec((None, 2 * GW, WINDOWS[0]), seq),
                 pl.BlockSpec((None, 2 * GW, WINDOWS[1]), seq),
                 pl.BlockSpec((None, 2 * GW, TM), lambda i: (layer * B_P + i // tiles, 0, lax.rem(i, tiles))),
                 pl.BlockSpec((TM, GMW), lambda i: (i, 0)),
                 pl.BlockSpec((TM, GMW), lambda i: (i, 0))]
    args = [x, gain, w, vnorm] + ([] if prev is None else list(prev))
    return pl.pallas_call(
        functools.partial(_inproj_prompt_body, n_alias=n_alias),
        out_shape=out_shape,
        grid=(N_P // TM,),
        in_specs=in_specs,
        out_specs=out_specs,
        input_output_aliases={4 + k: 1 + k for k in range(n_alias)},
        compiler_params=_params(1),
        name="inproj_prompt",
    )(*args)


def _inproj_sample_body(*refs, n_alias):
    x_ref, g_ref, w_ref, vn_ref = refs[:4]
    qkv_ref, kv1_ref, kv2_ref, kv3_ref, u_ref, vs_ref = refs[4 + n_alias:]
    kv_refs = (kv1_ref, kv2_ref, kv3_ref)
    h = _rms(x_ref[...], g_ref[...]).astype(BF16)
    for sec in range(3):
        for g in range(NG):
            z = _project(h, w_ref, sec, g)
            qkv_ref[:, sec * ATT + g * GW:sec * ATT + (g + 1) * GW] = z
            if sec > 0:
                kv_refs[g][:, (sec - 1) * GW:sec * GW] = z
    _gmlp_inputs(h, w_ref, vn_ref, u_ref, vs_ref)


def _inproj_sample(x, gain, w, vnorm, layer, prev):
    n_alias = 0 if prev is None else 4
    in_specs = [pl.BlockSpec((N_S, D), lambda i: (0, 0)),
                _resident((1, D), layer), _resident((D, QKVUV), layer), _resident((1, GMW), layer)]
    in_specs += [pl.BlockSpec(memory_space=pl.ANY)] * n_alias
    stacked = jax.ShapeDtypeStruct((DEPTH * N_S, 2 * GW), F32)
    out_shape = [jax.ShapeDtypeStruct((N_S, 3 * ATT), F32), stacked, stacked, stacked,
                 jax.ShapeDtypeStruct((N_S, GMW), F32),
                 jax.ShapeDtypeStruct((DEPTH * N_S, GMW), F32)]
    lay = lambda i: (layer, 0)
    out_specs = [pl.BlockSpec((N_S, 3 * ATT), lambda i: (0, 0)),
                 pl.BlockSpec((N_S, 2 * GW), lay), pl.BlockSpec((N_S, 2 * GW), lay),
                 pl.BlockSpec((N_S, 2 * GW), lay),
                 pl.BlockSpec((N_S, GMW), lambda i: (0, 0)),
                 pl.BlockSpec((N_S, GMW), lay)]
    args = [x, gain, w, vnorm] + ([] if prev is None else list(prev))
    aliases = {4: 1, 5: 2, 6: 3, 7: 5} if prev is not None else {}
    return pl.pallas_call(
        functools.partial(_inproj_sample_body, n_alias=n_alias),
        out_shape=out_shape,
        grid=(1,),
        in_specs=in_specs,
        out_specs=out_specs,
        input_output_aliases=aliases,
        compiler_params=_params(1),
        name="inproj_sample",
    )(*args)


def _alibi_slopes():
    h = np.arange(1, NG * NH + 1, dtype=np.float32)
    return np.power(np.float32(2.0), -8.0 * h / (NG * NH)).astype(np.float32).reshape(NG, NH)


def _head_masks():
    lane = lax.broadcasted_iota(jnp.int32, (1, GW), 1)
    return [(lane >= h * HD) & (lane < (h + 1) * HD) for h in range(NH)]


def _stack_heads(q, masks):
    return jnp.concatenate([jnp.where(m, q, jnp.zeros_like(q)) for m in masks], axis=0)


def _unstack_heads(x, masks, rows):
    out = jnp.zeros((rows, GW), F32)
    for h, m in enumerate(masks):
        out = jnp.where(m, x[h * rows:(h + 1) * rows], out)
    return out


def _softmax_parts(parts, values, value_dots):
    m = functools.reduce(jnp.maximum, [jnp.max(s, axis=-1, keepdims=True) for s in parts])
    ps = [jnp.exp(s - m) for s in parts]
    l = functools.reduce(jnp.add, [jnp.sum(p, axis=-1, keepdims=True) for p in ps])
    o = functools.reduce(jnp.add, [dot(p.astype(BF16), v) for p, v, dot in zip(ps, values, value_dots)])
    return o / l, m + jnp.log(l)


def _merge_groups(outs, lses):
    m = jnp.maximum(jnp.maximum(lses[0], lses[1]), lses[2])
    es = [jnp.exp(l - m) for l in lses]
    tot = es[0] + es[1] + es[2]
    return (es[0] / tot) * outs[0] + (es[1] / tot) * outs[1] + (es[2] / tot) * outs[2]


def _prompt_bias_table():
    slopes = _alibi_slopes()
    a = np.arange(KB)[:, None]
    c = np.arange(2 * KB)[None, :]
    dist = KB + a - c
    valid = (dist >= 0) & (dist <= KB)
    tab = np.empty((NG, NH * KB, 2 * KB), np.float32)
    for g in range(NG):
        for h in range(NH):
            bias = -slopes[g, h] * (DILS[g] * dist).astype(np.float32)
            tab[g, h * KB:(h + 1) * KB] = np.where(valid, bias, np.float32(NEG))
    return tab


def _sample_bias_tables():
    slopes = _alibi_slopes()
    j = np.arange(SPAD)[:, None]
    tabs = []
    for g in range(NG):
        w, d = WINDOWS[g], DILS[g]
        c = np.arange(w)[None, :]
        cn = np.arange(KB)[None, :]
        dist = np.concatenate([np.broadcast_to(w + j - c, (SPAD, w)),
                               np.broadcast_to(j - cn, (SPAD, KB))], axis=1)
        real = np.concatenate([np.ones((SPAD, w), bool), np.broadcast_to(cn < T_S, (SPAD, KB))], axis=1)
        valid = (j < T_S) & real & (dist >= 0) & (dist <= w) & (dist % d == 0)
        pad_row = np.broadcast_to(j >= T_S, dist.shape)
        tab = np.empty((NH * SPAD, dist.shape[1]), np.float32)
        for h in range(NH):
            bias = -slopes[g, h] * dist.astype(np.float32)
            blk = np.where(valid, bias, np.float32(NEG))
            tab[h * SPAD:(h + 1) * SPAD] = np.where(pad_row, np.float32(0.0), blk)
        tabs.append(tab)
    return tabs


def _attn_sample_body(z_ref, c1_ref, c2_ref, c3_ref, b1_ref, b2_ref, b3_ref, att_ref):
    masks = _head_masks()
    z = z_ref[...]
    pad = jnp.zeros((KB - SPAD, GW), F32)
    outs, lses = [], []
    for g, (c_ref, b_ref) in enumerate(((c1_ref, b1_ref), (c2_ref, b2_ref), (c3_ref, b3_ref))):
        w = WINDOWS[g]
        q = z[:, g * GW:(g + 1) * GW]
        k_new = jnp.concatenate([z[:, ATT + g * GW:ATT + (g + 1) * GW], pad], axis=0).astype(BF16)
        v_new = jnp.concatenate([z[:, 2 * ATT + g * GW:2 * ATT + (g + 1) * GW], pad], axis=0).astype(BF16)
        k_old = c_ref[:GW, :].astype(BF16)
        v_old = c_ref[GW:, :].astype(BF16)
        qs = _stack_heads(q, masks).astype(BF16)
        parts = [_dot(qs, k_old) + b_ref[:, :w], _dot_nt(qs, k_new) + b_ref[:, w:]]
        o, lse = _softmax_parts(parts, [v_old, v_new], [_dot_nt, _dot])
        outs.append(_unstack_heads(o, masks, SPAD))
        lses.append(_unstack_heads(lse, masks, SPAD))
    att_ref[...] = _merge_groups(outs, lses)


def _attn_sample(z, caches, tabs, layer):
    zp = jnp.pad(z.reshape(B_S, T_S, 3 * ATT), ((0, 0), (0, SPAD - T_S), (0, 0)))
    cs = [c.transpose(0, 1, 3, 4, 5, 2).reshape(DEPTH, B_S, 2 * GW, w) for c, w in zip(caches, WINDOWS)]
    seq = lambda b: (layer, b, 0, 0)
    in_specs = [pl.BlockSpec((None, SPAD, 3 * ATT), lambda b: (b, 0, 0))]
    in_specs += [pl.BlockSpec((None, None, 2 * GW, w), seq) for w in WINDOWS]
    in_specs += [_const(t.shape) for t in tabs]
    att = pl.pallas_call(
        _attn_sample_body,
        out_shape=jax.ShapeDtypeStruct((B_S, SPAD, GW), F32),
        grid=(B_S,),
        in_specs=in_specs,
        out_specs=pl.BlockSpec((None, SPAD, GW), lambda b: (b, 0, 0)),
        compiler_params=_params(1),
        name="attn_sample",
    )(zp, *cs, *tabs)
    return att[:, :T_S].reshape(N_S, GW)


def _mix_out(x, h, att, spat, wgate_ref, pa_ref, ps_ref, wo_ref):
    ga = jax.nn.sigmoid(_dot(h, wgate_ref[:, :D]))
    gb = jax.nn.sigmoid(_dot(h, wgate_ref[:, D:]))
    mix = ga * _dot(att.astype(BF16), pa_ref[...]) + gb * _dot(spat, ps_ref[...])
    return x + _dot(mix.astype(BF16), wo_ref[...])


def _mixer_out_body(x_ref, g_ref, wgate_ref, qkv_ref, kp1_ref, vp1_ref, kp2_ref, vp2_ref, kh3_ref, vh3_ref,
                    bias_ref, u_ref, vs_ref, ws_ref, b_ref, pa_ref, ps_ref, wo_ref, out_ref,
                    o_scr, l_scr, spat_scr):
    j = pl.program_id(1)
    masks = _head_masks()
    lane = lax.broadcasted_iota(jnp.int32, (1, 2 * KB), 1)
    first_tile_mask = (j > 0) | (lane >= KB)

    def wide(ref, base, rows):
        return jnp.concatenate([ref[base + sl, rows, :] for sl in range(SLABS)], axis=1)

    def tile(sec, g, rows):
        return wide(qkv_ref, (sec * NG + g) * SLABS, rows)

    def band(g, q, k, v, bias, rows, n):
        qs = _stack_heads(q.astype(BF16), masks)
        s = _dot_nt(qs, k.astype(BF16)) + bias
        m = jnp.max(s, axis=-1, keepdims=True)
        p = jnp.exp(s - m)
        l = jnp.sum(p, axis=-1, keepdims=True)
        o = _unstack_heads(_dot(p.astype(BF16), v.astype(BF16)) / l, masks, n)
        lse = _unstack_heads(m + jnp.log(l), masks, n)
        for sl in range(SLABS):
            o_scr[g * SLABS + sl, rows, :] = o[:, sl * LANES:(sl + 1) * LANES]
            l_scr[g * SLABS + sl, rows, :] = lse[:, sl * LANES:(sl + 1) * LANES]

    bias1 = bias_ref[0]
    for blk in range(TM // KB):
        own = pl.ds(blk * KB, KB)
        if blk == 0:
            k = jnp.concatenate([wide(kp1_ref, 0, slice(None)), tile(1, 0, own)], axis=0)
            v = jnp.concatenate([wide(vp1_ref, 0, slice(None)), tile(2, 0, own)], axis=0)
            bias = jnp.where(first_tile_mask, bias1, NEG)
        else:
            both = pl.ds((blk - 1) * KB, 2 * KB)
            k, v, bias = tile(1, 0, both), tile(2, 0, both), bias1
        band(0, tile(0, 0, own), k, v, bias, own, KB)

    bias2 = jnp.where(first_tile_mask, bias_ref[1], NEG)
    for r in range(DILS[1]):
        rows = pl.ds(r, KB, stride=DILS[1])
        k = jnp.concatenate([wide(kp2_ref, 0, rows), tile(1, 1, rows)], axis=0)
        v = jnp.concatenate([wide(vp2_ref, 0, rows), tile(2, 1, rows)], axis=0)
        band(1, tile(0, 1, rows), k, v, bias2, rows, KB)

    nq = TM // DILS[2]
    q0 = pl.multiple_of(j * nq, nq)
    bias3 = jnp.concatenate([bias_ref[2, pl.ds(h * KB + q0, nq), KB:] for h in range(NH)], axis=0)
    for r in range(DILS[2]):
        rows = pl.ds(r, nq, stride=DILS[2])
        keys = pl.ds(r, KB, stride=DILS[2])
        band(2, tile(0, 2, rows), wide(kh3_ref, 0, keys), wide(vh3_ref, 0, keys), bias3, rows, nq)

    full = slice(None)
    att = _merge_groups([wide(o_scr, g * SLABS, full) for g in range(NG)],
                        [wide(l_scr, g * SLABS, full) for g in range(NG)])

    x = x_ref[...]
    h = _rms(x, g_ref[...]).astype(BF16)
    row = lax.broadcasted_iota(jnp.int32, (CHUNK, CHUNK), 0)
    col = lax.broadcasted_iota(jnp.int32, (CHUNK, CHUNK), 1)
    for g in range(GG):
        w = jnp.where(col <= row, ws_ref[g], 0.0).astype(BF16)
        gs = slice(g * GC, (g + 1) * GC)
        for c in range(TM // CHUNK):
            rs = slice(c * CHUNK, (c + 1) * CHUNK)
            zc = _dot(w, vs_ref[rs, gs]) + b_ref[g]
            spat_scr[rs, gs] = (u_ref[rs, gs] * zc).astype(BF16)
    out_ref[...] = _mix_out(x, h, att, spat_scr[...], wgate_ref, pa_ref, ps_ref, wo_ref)


def _mixer_out(x, gain, wgate, qkv, bias_tab, u, vs, ws, bias, pa, ps, wo, layer):
    tiles = S_P // TM
    assert TM == KB * DILS[1] and S_P == KB * DILS[2]
    row = lambda b, j: (b * tiles + j, 0)

    def slabs(rows, sec, g, row_blk):
        return pl.BlockSpec((None, SLABS, rows, LANES), lambda b, j: (b, sec * NG + g, row_blk(j), 0))

    before1 = lambda j: jnp.maximum(j * (TM // KB) - 1, 0)
    before2 = lambda j: jnp.maximum(j - 1, 0)
    whole = lambda j: 0
    in_specs = [pl.BlockSpec((TM, D), row), _resident((1, D), layer), _resident((D, 2 * D), layer),
                pl.BlockSpec((None, N_SLAB, TM, LANES), lambda b, j: (b, 0, j, 0)),
                slabs(KB, 1, 0, before1), slabs(KB, 2, 0, before1),
                slabs(TM, 1, 1, before2), slabs(TM, 2, 1, before2),
                slabs(S_P, 1, 2, whole), slabs(S_P, 2, 2, whole),
                _const(bias_tab.shape),
                pl.BlockSpec((TM, GMW), row), pl.BlockSpec((TM, GMW), row),
                _resident((GG, CHUNK, CHUNK), layer), _resident((GG, CHUNK, 1), layer),
                _resident((GW, D), layer), _resident((GMW, D), layer), _resident((D, D), layer)]
    return pl.pallas_call(
        _mixer_out_body,
        out_shape=jax.ShapeDtypeStruct((N_P, D), F32),
        grid=(B_P, tiles),
        in_specs=in_specs,
        out_specs=pl.BlockSpec((TM, D), row),
        scratch_shapes=[pltpu.VMEM((NG * SLABS, TM, LANES), F32), pltpu.VMEM((NG * SLABS, TM, LANES), F32),
                        pltpu.VMEM((TM, GMW), BF16)],
        compiler_params=_params(2),
        name="mixer_out",
    )(x, gain, wgate, qkv, qkv, qkv, qkv, qkv, qkv, qkv, bias_tab, u, vs, ws, bias, pa, ps, wo)


def _outproj_sample_body(x_ref, g_ref, wgate_ref, att_ref, u_ref, vs_ref, wt_ref, b_ref,
                         pa_ref, ps_ref, wo_ref, out_ref):
    x = x_ref[...]
    h = _rms(x, g_ref[...]).astype(BF16)
    row = lax.broadcasted_iota(jnp.int32, (N_S, N_S), 0)
    col = lax.broadcasted_iota(jnp.int32, (N_S, N_S), 1)
    keep = (row // T_S == col // T_S) & (col <= row)
    u = u_ref[...]
    vs = vs_ref[...].astype(BF16)
    parts = []
    for g in range(GG):
        corner = wt_ref[g]
        w = jnp.zeros((N_S, N_S), F32)
        for t in range(T_S):
            w = jnp.where(keep & (lax.rem(row, T_S) == t), corner[t:t + 1, :], w)
        w = w.astype(BF16)
        gs = slice(g * GC, (g + 1) * GC)
        parts.append(u[:, gs] * (_dot(w, vs[:, gs]) + b_ref[g]))
    spat = jnp.concatenate(parts, axis=1).astype(BF16)
    out_ref[...] = _mix_out(x, h, att_ref[...], spat, wgate_ref, pa_ref, ps_ref, wo_ref)


def _outproj_sample(x, gain, wgate, att, u, vs_all, wt, bias, pa, ps, wo, layer):
    zero = lambda i: (0, 0)
    in_specs = [pl.BlockSpec((N_S, D), zero), _resident((1, D), layer), _resident((D, 2 * D), layer),
                pl.BlockSpec((N_S, GW), zero), pl.BlockSpec((N_S, GMW), zero),
                pl.BlockSpec((N_S, GMW), lambda i: (layer, 0)),
                _resident((GG, T_S, N_S), layer), _resident((GG, N_S, 1), layer),
                _resident((GW, D), layer), _resident((GMW, D), layer), _resident((D, D), layer)]
    return pl.pallas_call(
        _outproj_sample_body,
        out_shape=jax.ShapeDtypeStruct((N_S, D), F32),
        grid=(1,),
        in_specs=in_specs,
        out_specs=pl.BlockSpec((N_S, D), zero),
        compiler_params=_params(1),
        name="outproj_sample",
    )(x, gain, wgate, att, u, vs_all, wt, bias, pa, ps, wo)


def kernel(x_prompt, x_sample, cache_kv_w128, cache_kv_w512, cache_kv_w2048, ffn1_norm, ffn1_gate, ffn1_up, ffn1_down, mix_norm, w_in, gmlp_v_norm, gmlp_ws, gmlp_bias, proj_att, proj_spatial, w_out, ffn2_norm, ffn2_gate, ffn2_up, ffn2_down, final_norm):
    caches = (cache_kv_w128, cache_kv_w512, cache_kv_w2048)
    xp = x_prompt.reshape(N_P, D)
    xs = x_sample.reshape(N_S, D)

    cb = 256
    f1g, f1u = _cast(ffn1_gate, cb, 0, DFF // cb), _cast(ffn1_up, cb, 0, DFF // cb)
    f2g, f2u = _cast(ffn2_gate, cb, 0, DFF // cb), _cast(ffn2_up, cb, 0, DFF // cb)
    f1d, f2d = _cast(ffn1_down, cb, 0, D // cb), _cast(ffn2_down, cb, 0, D // cb)
    w_qkvuv = _cast(w_in, cb, 0, QKVUV // cb)
    w_gate = _cast(w_in, cb, QKVUV // cb, 2 * D // cb)
    pa, ps, wo = _cast(proj_att, cb, 0, D // cb), _cast(proj_spatial, cb, 0, D // cb), _cast(w_out, cb, 0, D // cb)

    n1 = ffn1_norm.reshape(DEPTH, 1, D)
    n2 = ffn2_norm.reshape(DEPTH, 1, D)
    nm = mix_norm.reshape(DEPTH, 1, D)
    nv = gmlp_v_norm.reshape(DEPTH, 1, GMW)
    nf = final_norm.reshape(1, D)
    bias_p = gmlp_bias.reshape(DEPTH, GG, CHUNK, 1)
    wt_s = jnp.tile(gmlp_ws[:, :, :T_S, :T_S], (1, 1, 1, B_S))
    bias_s = jnp.tile(gmlp_bias[:, :, :T_S], (1, 1, B_S)).reshape(DEPTH, GG, N_S, 1)

    attn_bias = jnp.asarray(_prompt_bias_table())
    sample_tabs = [jnp.asarray(t) for t in _sample_bias_tables()]

    kv_p = kv_s = None
    for l in range(DEPTH):
        last = l == DEPTH - 1
        xp = _ffn(xp, n1, f1g, f1u, f1d, l)
        qkv, kv1, kv2, kv3, u, vs = _inproj_prompt(xp, nm, w_qkvuv, nv, l, kv_p)
        kv_p = (kv1, kv2, kv3)
        xp = _mixer_out(xp, nm, w_gate, qkv, attn_bias, u, vs, gmlp_ws, bias_p, pa, ps, wo, l)
        xp = _ffn(xp, n2, f2g, f2u, f2d, l, nf if last else None)
        xs = _ffn(xs, n1, f1g, f1u, f1d, l)
        z_s, ks1, ks2, ks3, u_s, vs_s = _inproj_sample(xs, nm, w_qkvuv, nv, l, kv_s)
        kv_s = (ks1, ks2, ks3, vs_s)
        att_s = _attn_sample(z_s, caches, sample_tabs, l)
        xs = _outproj_sample(xs, nm, w_gate, att_s, u_s, vs_s, wt_s, bias_s, pa, ps, wo, l)
        xs = _ffn(xs, n2, f2g, f2u, f2d, l, nf if last else None)

    kv_out = [k.reshape(DEPTH, B_P, 2, NH, HD, w).transpose(0, 1, 5, 2, 3, 4) for k, w in zip(kv_p, WINDOWS)]
    ks_shape = (DEPTH, B_S, T_S, 2, NH, HD)
    return (xp.reshape(B_P, S_P, D), xs.reshape(B_S, T_S, D), kv_out[0], kv_out[1], kv_out[2],
            kv_s[0].reshape(ks_shape), kv_s[1].reshape(ks_shape), kv_s[2].reshape(ks_shape),
            kv_s[3].reshape(DEPTH, B_S, T_S, GMW))
```

```python
import functools

import numpy as np
import jax
import jax.numpy as jnp
from jax import lax
from jax.experimental import pallas as pl
from jax.experimental.pallas import tpu as pltpu

F32 = jnp.float32
BF16 = jnp.bfloat16

D = 1024
DEPTH = 2
B_P, S_P = 8, 2048
B_S, T_S = 32, 4
N_P = B_P * S_P
N_S = B_S * T_S
HD, NH, NG = 64, 4, 3
GW = NH * HD
ATT = NG * GW
WINDOWS = (128, 512, 2048)
DILS = (1, 4, 16)
KB = 128
CHUNK = 128
GG, GC = 4, 128
GMW = GG * GC
DFF = 2816
FC = 256
QKVUV = 3 * ATT + 2 * GMW
EPS = 1e-6
NEG = -1e30
TM = 512
SPAD = 16
LANES = 128
SUBLANES_BF16 = 16
SLABS = GW // LANES
N_SLAB = 3 * ATT // LANES
VMEM_LIMIT = 56 * 1024 * 1024

_NT = (((1,), (1,)), ((), ()))


def _rms(x, gain):
    r = lax.rsqrt(jnp.mean(x * x, axis=-1, keepdims=True) + EPS)
    return (x * r) * gain


def _dot(a, b):
    return jnp.dot(a, b, preferred_element_type=F32)


def _dot_nt(a, b):
    return lax.dot_general(a, b, _NT, preferred_element_type=F32)


def _params(n_axes):
    return pltpu.CompilerParams(dimension_semantics=("arbitrary",) * n_axes,
                                vmem_limit_bytes=VMEM_LIMIT)


def _resident(shape, layer):
    nd = len(shape)
    return pl.BlockSpec((None,) + shape, lambda *_: (layer,) + (0,) * nd,
                        pipeline_mode=pl.Buffered(1))


def _const(shape):
    nd = len(shape)
    return pl.BlockSpec(shape, lambda *_: (0,) * nd, pipeline_mode=pl.Buffered(1))


class _CastJob:
    def __init__(self, src, layer, col_ranges=None):
        self.src, self.layer = src, layer
        self.rows, self.cols = src.shape[1:]
        self.col_ranges = col_ranges or [(0, self.cols)]

    def specs(self, n_steps):
        rps = self.rows // n_steps
        if rps % SUBLANES_BF16:
            rps = self.rows // (n_steps // 2)
        assert rps % SUBLANES_BF16 == 0 and self.rows % rps == 0
        last = self.rows // rps - 1
        layer = self.layer
        in_spec = pl.BlockSpec((None, rps, self.cols), lambda i: (layer, jnp.minimum(i, last), 0))
        out_specs = [pl.BlockSpec((rps, c1 - c0), lambda i: (jnp.minimum(i, last), 0))
                     for c0, c1 in self.col_ranges]
        out_shapes = [jax.ShapeDtypeStruct((self.rows, c1 - c0), BF16) for c0, c1 in self.col_ranges]
        return in_spec, out_specs, out_shapes


def _run_casts(jobs, src_refs, out_refs):
    out_refs = list(out_refs)
    for job, src_ref in zip(jobs, src_refs):
        x = src_ref[...]
        for c0, c1 in job.col_ranges:
            out_refs.pop(0)[...] = x[:, c0:c1].astype(BF16)


def _cast_specs(jobs, n_steps):
    in_specs, out_specs, out_shapes = [], [], []
    for job in jobs:
        i, o, s = job.specs(n_steps)
        in_specs.append(i)
        out_specs += o
        out_shapes += s
    return in_specs, out_specs, out_shapes


def _cast_body(*refs, jobs):
    _run_casts(jobs, refs[:len(jobs)], refs[len(jobs):])


def _cast(jobs):
    n_steps = 16
    in_specs, out_specs, out_shapes = _cast_specs(jobs, n_steps)
    return pl.pallas_call(
        functools.partial(_cast_body, jobs=jobs),
        out_shape=out_shapes,
        grid=(n_steps,),
        in_specs=in_specs,
        out_specs=out_specs,
        compiler_params=_params(1),
        name="cast_bf16",
    )(*[j.src for j in jobs])


def _ffn_body(*refs, final, jobs, with_attn):
    refs = list(refs)
    x_ref, g_ref, wg_ref, wu_ref, wd_ref = refs[:5]
    del refs[:5]
    fg_ref = refs.pop(0) if final else None
    cast_srcs = [refs.pop(0) for _ in jobs]
    attn_in = [refs.pop(0) for _ in range(7)] if with_attn else None
    o_ref = refs.pop(0)
    cast_outs = [refs.pop(0) for _ in range(sum(len(j.col_ranges) for j in jobs))]
    att_ref = refs.pop(0) if with_attn else None
    a_scr, = refs

    _run_casts(jobs, cast_srcs, cast_outs)
    if with_attn:
        _attn_sample_body(*attn_in, att_ref)
    x = x_ref[...]
    h = _rms(x, g_ref[...]).astype(BF16)
    for c in range(DFF // FC):
        cs = slice(c * FC, (c + 1) * FC)
        gate = _dot(h, wg_ref[:, cs])
        up = _dot(h, wu_ref[:, cs])
        a_scr[:, cs] = (jax.nn.silu(gate) * up).astype(BF16)
    y = x + 0.5 * _dot(a_scr[...], wd_ref[...])
    if final:
        y = _rms(y, fg_ref[...])
    o_ref[...] = y


def _ffn(x, gain, layer, weights, final_gain=None, jobs=(), attn=None):
    n = x.shape[0]
    tm = min(TM, n)
    n_steps = n // tm
    final = final_gain is not None
    jobs = list(jobs)
    in_specs = [pl.BlockSpec((tm, D), lambda i: (i, 0)), _resident((1, D), layer),
                _const((D, DFF)), _const((D, DFF)), _const((DFF, D))]
    args = [x, gain, *weights]
    if final:
        in_specs.append(_const((1, D)))
        args.append(final_gain)
    cast_in, cast_out, cast_shapes = _cast_specs(jobs, n_steps)
    in_specs += cast_in
    args += [j.src for j in jobs]
    out_specs = [pl.BlockSpec((tm, D), lambda i: (i, 0))] + cast_out
    out_shape = [jax.ShapeDtypeStruct((n, D), F32)] + cast_shapes
    if attn is not None:
        assert n_steps == B_S
        attn_args, attn_in_specs, attn_out_spec, attn_out_shape = attn
        in_specs += attn_in_specs
        args += attn_args
        out_specs.append(attn_out_spec)
        out_shape.append(attn_out_shape)
    outs = pl.pallas_call(
        functools.partial(_ffn_body, final=final, jobs=jobs, with_attn=attn is not None),
        out_shape=out_shape,
        grid=(n_steps,),
        in_specs=in_specs,
        out_specs=out_specs,
        scratch_shapes=[pltpu.VMEM((tm, DFF), BF16)],
        compiler_params=_params(1),
        name="ffn_final" if final else "ffn",
    )(*args)
    n_cast = len(cast_shapes)
    return outs[0], list(outs[1:1 + n_cast]), (outs[1 + n_cast] if attn is not None else None)


def _project(h, w_ref, sec, g):
    c0 = sec * ATT + g * GW
    z = _dot(h, w_ref[:, c0:c0 + GW])
    if sec == 0:
        z = z * (HD ** -0.5)
    return z


def _gmlp_inputs(h, w_ref, vn_ref, u_ref, vs_ref):
    u_ref[...] = jax.nn.gelu(_dot(h, w_ref[:, 3 * ATT:3 * ATT + GMW]))
    gv = jax.nn.gelu(_dot(h, w_ref[:, 3 * ATT + GMW:QKVUV]))
    vs_ref[...] = _rms(gv, vn_ref[...]).astype(vs_ref.dtype)


def _inproj_prompt_body(*refs, n_alias):
    x_ref, g_ref, w_ref, vn_ref = refs[:4]
    qkv_ref, kv1_ref, kv2_ref, kv3_ref, u_ref, vs_ref = refs[4 + n_alias:]
    h = _rms(x_ref[...], g_ref[...]).astype(BF16)
    _gmlp_inputs(h, w_ref, vn_ref, u_ref, vs_ref)
    for sec in range(3):
        for g in range(NG):
            z = _project(h, w_ref, sec, g)
            for sl in range(SLABS):
                qkv_ref[(sec * NG + g) * SLABS + sl] = z[:, sl * LANES:(sl + 1) * LANES]
            if sec == 0:
                continue
            rows = slice((sec - 1) * GW, sec * GW)
            if g == 2:
                kv3_ref[rows, :] = z.T
            elif g == 1:
                kv2_ref[rows, :] = z.T
            else:
                kv1_ref[rows, :] = z[TM - WINDOWS[0]:, :].T


def _inproj_prompt(x, gain, w, vnorm, layer, prev):
    tiles = S_P // TM
    n_alias = 0 if prev is None else 3
    in_specs = [pl.BlockSpec((TM, D), lambda i: (i, 0)),
                _resident((1, D), layer), _const((D, QKVUV)), _resident((1, GMW), layer)]
    in_specs += [pl.BlockSpec(memory_space=pl.ANY)] * n_alias
    out_shape = [jax.ShapeDtypeStruct((B_P, N_SLAB, S_P, LANES), F32),
                 jax.ShapeDtypeStruct((DEPTH * B_P, 2 * GW, WINDOWS[0]), F32),
                 jax.ShapeDtypeStruct((DEPTH * B_P, 2 * GW, WINDOWS[1]), F32),
                 jax.ShapeDtypeStruct((DEPTH * B_P, 2 * GW, WINDOWS[2]), F32),
                 jax.ShapeDtypeStruct((N_P, GMW), F32),
                 jax.ShapeDtypeStruct((N_P, GMW), BF16)]
    seq = lambda i: (layer * B_P + i // tiles, 0, 0)
    out_specs = [pl.BlockSpec((None, N_SLAB, TM, LANES), lambda i: (i // tiles, 0, lax.rem(i, tiles), 0)),
                 pl.BlockSpec((None, 2 * GW, WINDOWS[0]), seq),
                 pl.BlockSpec((None, 2 * GW, WINDOWS[1]), seq),
                 pl.BlockSpec((None, 2 * GW, TM), lambda i: (layer * B_P + i // tiles, 0, lax.rem(i, tiles))),
                 pl.BlockSpec((TM, GMW), lambda i: (i, 0)),
                 pl.BlockSpec((TM, GMW), lambda i: (i, 0))]
    args = [x, gain, w, vnorm] + ([] if prev is None else list(prev))
    return pl.pallas_call(
        functools.partial(_inproj_prompt_body, n_alias=n_alias),
        out_shape=out_shape,
        grid=(N_P // TM,),
        in_specs=in_specs,
        out_specs=out_specs,
        input_output_aliases={4 + k: 1 + k for k in range(n_alias)},
        compiler_params=_params(1),
        name="inproj_prompt",
    )(*args)


def _inproj_sample_body(*refs, n_alias):
    x_ref, g_ref, w_ref, vn_ref = refs[:4]
    qkv_ref, kv1_ref, kv2_ref, kv3_ref, u_ref, vs_ref = refs[4 + n_alias:]
    kv_refs = (kv1_ref, kv2_ref, kv3_ref)
    h = _rms(x_ref[...], g_ref[...]).astype(BF16)
    for sec in range(3):
        for g in range(NG):
            z = _project(h, w_ref, sec, g)
            qkv_ref[:, sec * ATT + g * GW:sec * ATT + (g + 1) * GW] = z
            if sec > 0:
                kv_refs[g][:, (sec - 1) * GW:sec * GW] = z
    _gmlp_inputs(h, w_ref, vn_ref, u_ref, vs_ref)


def _inproj_sample(x, gain, w, vnorm, layer, prev):
    n_alias = 0 if prev is None else 4
    in_specs = [pl.BlockSpec((N_S, D), lambda i: (0, 0)),
                _resident((1, D), layer), _const((D, QKVUV)), _resident((1, GMW), layer)]
    in_specs += [pl.BlockSpec(memory_space=pl.ANY)] * n_alias
    stacked = jax.ShapeDtypeStruct((DEPTH * N_S, 2 * GW), F32)
    out_shape = [jax.ShapeDtypeStruct((N_S, 3 * ATT), F32), stacked, stacked, stacked,
                 jax.ShapeDtypeStruct((N_S, GMW), F32),
                 jax.ShapeDtypeStruct((DEPTH * N_S, GMW), F32)]
    lay = lambda i: (layer, 0)
    out_specs = [pl.BlockSpec((N_S, 3 * ATT), lambda i: (0, 0)),
                 pl.BlockSpec((N_S, 2 * GW), lay), pl.BlockSpec((N_S, 2 * GW), lay),
                 pl.BlockSpec((N_S, 2 * GW), lay),
                 pl.BlockSpec((N_S, GMW), lambda i: (0, 0)),
                 pl.BlockSpec((N_S, GMW), lay)]
    args = [x, gain, w, vnorm] + ([] if prev is None else list(prev))
    aliases = {4: 1, 5: 2, 6: 3, 7: 5} if prev is not None else {}
    return pl.pallas_call(
        functools.partial(_inproj_sample_body, n_alias=n_alias),
        out_shape=out_shape,
        grid=(1,),
        in_specs=in_specs,
        out_specs=out_specs,
        input_output_aliases=aliases,
        compiler_params=_params(1),
        name="inproj_sample",
    )(*args)


def _alibi_slopes():
    h = np.arange(1, NG * NH + 1, dtype=np.float32)
    return np.power(np.float32(2.0), -8.0 * h / (NG * NH)).astype(np.float32).reshape(NG, NH)


def _head_masks():
    lane = lax.broadcasted_iota(jnp.int32, (1, GW), 1)
    return [(lane >= h * HD) & (lane < (h + 1) * HD) for h in range(NH)]


def _stack_heads(q, masks):
    return jnp.concatenate([jnp.where(m, q, jnp.zeros_like(q)) for m in masks], axis=0)


def _unstack_heads(x, masks, rows):
    out = jnp.zeros((rows, GW), F32)
    for h, m in enumerate(masks):
        out = jnp.where(m, x[h * rows:(h + 1) * rows], out)
    return out


def _softmax_parts(parts, values, value_dots):
    m = functools.reduce(jnp.maximum, [jnp.max(s, axis=-1, keepdims=True) for s in parts])
    ps = [jnp.exp(s - m) for s in parts]
    l = functools.reduce(jnp.add, [jnp.sum(p, axis=-1, keepdims=True) for p in ps])
    o = functools.reduce(jnp.add, [dot(p.astype(BF16), v) for p, v, dot in zip(ps, values, value_dots)])
    return o / l, m + jnp.log(l)


def _merge_groups(outs, lses):
    m = jnp.maximum(jnp.maximum(lses[0], lses[1]), lses[2])
    es = [jnp.exp(l - m) for l in lses]
    tot = es[0] + es[1] + es[2]
    return (es[0] / tot) * outs[0] + (es[1] / tot) * outs[1] + (es[2] / tot) * outs[2]


def _prompt_bias_table():
    slopes = _alibi_slopes()
    a = np.arange(KB)[:, None]
    c = np.arange(2 * KB)[None, :]
    dist = KB + a - c
    valid = (dist >= 0) & (dist <= KB)
    tab = np.empty((NG, NH * KB, 2 * KB), np.float32)
    for g in range(NG):
        for h in range(NH):
            bias = -slopes[g, h] * (DILS[g] * dist).astype(np.float32)
            tab[g, h * KB:(h + 1) * KB] = np.where(valid, bias, np.float32(NEG))
    return tab


def _sample_bias_tables():
    slopes = _alibi_slopes()
    j = np.arange(SPAD)[:, None]
    tabs = []
    for g in range(NG):
        w, d = WINDOWS[g], DILS[g]
        c = np.arange(w)[None, :]
        cn = np.arange(KB)[None, :]
        dist = np.concatenate([np.broadcast_to(w + j - c, (SPAD, w)),
                               np.broadcast_to(j - cn, (SPAD, KB))], axis=1)
        real = np.concatenate([np.ones((SPAD, w), bool), np.broadcast_to(cn < T_S, (SPAD, KB))], axis=1)
        valid = (j < T_S) & real & (dist >= 0) & (dist <= w) & (dist % d == 0)
        pad_row = np.broadcast_to(j >= T_S, dist.shape)
        tab = np.empty((NH * SPAD, dist.shape[1]), np.float32)
        for h in range(NH):
            bias = -slopes[g, h] * dist.astype(np.float32)
            blk = np.where(valid, bias, np.float32(NEG))
            tab[h * SPAD:(h + 1) * SPAD] = np.where(pad_row, np.float32(0.0), blk)
        tabs.append(tab)
    return tabs


def _attn_sample_body(z_ref, c1_ref, c2_ref, c3_ref, b1_ref, b2_ref, b3_ref, att_ref):
    masks = _head_masks()
    z = z_ref[...]
    pad = jnp.zeros((KB - SPAD, GW), F32)
    outs, lses = [], []
    for g, (c_ref, b_ref) in enumerate(((c1_ref, b1_ref), (c2_ref, b2_ref), (c3_ref, b3_ref))):
        w = WINDOWS[g]
        q = z[:, g * GW:(g + 1) * GW]
        k_new = jnp.concatenate([z[:, ATT + g * GW:ATT + (g + 1) * GW], pad], axis=0).astype(BF16)
        v_new = jnp.concatenate([z[:, 2 * ATT + g * GW:2 * ATT + (g + 1) * GW], pad], axis=0).astype(BF16)
        k_old = c_ref[:GW, :].astype(BF16)
        v_old = c_ref[GW:, :].astype(BF16)
        qs = _stack_heads(q, masks).astype(BF16)
        parts = [_dot(qs, k_old) + b_ref[:, :w], _dot_nt(qs, k_new) + b_ref[:, w:]]
        o, lse = _softmax_parts(parts, [v_old, v_new], [_dot_nt, _dot])
        outs.append(_unstack_heads(o, masks, SPAD))
        lses.append(_unstack_heads(lse, masks, SPAD))
    att_ref[...] = _merge_groups(outs, lses)


def _attn_sample_operands(z, caches, tabs, layer):
    zp = jnp.pad(z.reshape(B_S, T_S, 3 * ATT), ((0, 0), (0, SPAD - T_S), (0, 0)))
    cs = [c.transpose(0, 1, 3, 4, 5, 2).reshape(DEPTH, B_S, 2 * GW, w) for c, w in zip(caches, WINDOWS)]
    seq = lambda b: (layer, b, 0, 0)
    in_specs = [pl.BlockSpec((None, SPAD, 3 * ATT), lambda b: (b, 0, 0))]
    in_specs += [pl.BlockSpec((None, None, 2 * GW, w), seq) for w in WINDOWS]
    in_specs += [_const(t.shape) for t in tabs]
    out_spec = pl.BlockSpec((None, SPAD, GW), lambda b: (b, 0, 0))
    return [zp, *cs, *tabs], in_specs, out_spec, jax.ShapeDtypeStruct((B_S, SPAD, GW), F32)


def _mix_out(x, h, att, spat, wgate_ref, pa_ref, ps_ref, wo_ref):
    ga = jax.nn.sigmoid(_dot(h, wgate_ref[:, :D]))
    gb = jax.nn.sigmoid(_dot(h, wgate_ref[:, D:]))
    mix = ga * _dot(att.astype(BF16), pa_ref[...]) + gb * _dot(spat, ps_ref[...])
    return x + _dot(mix.astype(BF16), wo_ref[...])


def _mixer_out_body(x_ref, g_ref, wgate_ref, qkv_ref, kp1_ref, vp1_ref, kp2_ref, vp2_ref, kh3_ref, vh3_ref,
                    bias_ref, u_ref, vs_ref, ws_ref, b_ref, pa_ref, ps_ref, wo_ref, out_ref,
                    o_scr, l_scr, spat_scr):
    j = pl.program_id(1)
    masks = _head_masks()
    lane = lax.broadcasted_iota(jnp.int32, (1, 2 * KB), 1)
    first_tile_mask = (j > 0) | (lane >= KB)

    def wide(ref, base, rows):
        return jnp.concatenate([ref[base + sl, rows, :] for sl in range(SLABS)], axis=1)

    def tile(sec, g, rows):
        return wide(qkv_ref, (sec * NG + g) * SLABS, rows)

    def band(g, q, k, v, bias, rows, n):
        qs = _stack_heads(q.astype(BF16), masks)
        s = _dot_nt(qs, k.astype(BF16)) + bias
        m = jnp.max(s, axis=-1, keepdims=True)
        p = jnp.exp(s - m)
        l = jnp.sum(p, axis=-1, keepdims=True)
        o = _unstack_heads(_dot(p.astype(BF16), v.astype(BF16)) / l, masks, n)
        lse = _unstack_heads(m + jnp.log(l), masks, n)
        for sl in range(SLABS):
            o_scr[g * SLABS + sl, rows, :] = o[:, sl * LANES:(sl + 1) * LANES]
            l_scr[g * SLABS + sl, rows, :] = lse[:, sl * LANES:(sl + 1) * LANES]

    bias1 = bias_ref[0]
    for blk in range(TM // KB):
        own = pl.ds(blk * KB, KB)
        if blk == 0:
            k = jnp.concatenate([wide(kp1_ref, 0, slice(None)), tile(1, 0, own)], axis=0)
            v = jnp.concatenate([wide(vp1_ref, 0, slice(None)), tile(2, 0, own)], axis=0)
            bias = jnp.where(first_tile_mask, bias1, NEG)
        else:
            both = pl.ds((blk - 1) * KB, 2 * KB)
            k, v, bias = tile(1, 0, both), tile(2, 0, both), bias1
        band(0, tile(0, 0, own), k, v, bias, own, KB)

    bias2 = jnp.where(first_tile_mask, bias_ref[1], NEG)
    for r in range(DILS[1]):
        rows = pl.ds(r, KB, stride=DILS[1])
        k = jnp.concatenate([wide(kp2_ref, 0, rows), tile(1, 1, rows)], axis=0)
        v = jnp.concatenate([wide(vp2_ref, 0, rows), tile(2, 1, rows)], axis=0)
        band(1, tile(0, 1, rows), k, v, bias2, rows, KB)

    nq = TM // DILS[2]
    q0 = pl.multiple_of(j * nq, nq)
    bias3 = jnp.concatenate([bias_ref[2, pl.ds(h * KB + q0, nq), KB:] for h in range(NH)], axis=0)
    for r in range(DILS[2]):
        rows = pl.ds(r, nq, stride=DILS[2])
        keys = pl.ds(r, KB, stride=DILS[2])
        band(2, tile(0, 2, rows), wide(kh3_ref, 0, keys), wide(vh3_ref, 0, keys), bias3, rows, nq)

    full = slice(None)
    att = _merge_groups([wide(o_scr, g * SLABS, full) for g in range(NG)],
                        [wide(l_scr, g * SLABS, full) for g in range(NG)])

    x = x_ref[...]
    h = _rms(x, g_ref[...]).astype(BF16)
    row = lax.broadcasted_iota(jnp.int32, (CHUNK, CHUNK), 0)
    col = lax.broadcasted_iota(jnp.int32, (CHUNK, CHUNK), 1)
    for g in range(GG):
        w = jnp.where(col <= row, ws_ref[g], 0.0).astype(BF16)
        gs = slice(g * GC, (g + 1) * GC)
        for c in range(TM // CHUNK):
            rs = slice(c * CHUNK, (c + 1) * CHUNK)
            zc = _dot(w, vs_ref[rs, gs]) + b_ref[g]
            spat_scr[rs, gs] = (u_ref[rs, gs] * zc).astype(BF16)
    out_ref[...] = _mix_out(x, h, att, spat_scr[...], wgate_ref, pa_ref, ps_ref, wo_ref)


def _mixer_out(x, gain, wgate, qkv, bias_tab, u, vs, ws, bias, pa, ps, wo, layer):
    tiles = S_P // TM
    assert TM == KB * DILS[1] and S_P == KB * DILS[2]
    row = lambda b, j: (b * tiles + j, 0)

    def slabs(rows, sec, g, row_blk):
        return pl.BlockSpec((None, SLABS, rows, LANES), lambda b, j: (b, sec * NG + g, row_blk(j), 0))

    before1 = lambda j: jnp.maximum(j * (TM // KB) - 1, 0)
    before2 = lambda j: jnp.maximum(j - 1, 0)
    whole = lambda j: 0
    in_specs = [pl.BlockSpec((TM, D), row), _resident((1, D), layer), _const((D, 2 * D)),
                pl.BlockSpec((None, N_SLAB, TM, LANES), lambda b, j: (b, 0, j, 0)),
                slabs(KB, 1, 0, before1), slabs(KB, 2, 0, before1),
                slabs(TM, 1, 1, before2), slabs(TM, 2, 1, before2),
                slabs(S_P, 1, 2, whole), slabs(S_P, 2, 2, whole),
                _const(bias_tab.shape),
                pl.BlockSpec((TM, GMW), row), pl.BlockSpec((TM, GMW), row),
                _resident((GG, CHUNK, CHUNK), layer), _resident((GG, CHUNK, 1), layer),
                _const((GW, D)), _const((GMW, D)), _const((D, D))]
    return pl.pallas_call(
        _mixer_out_body,
        out_shape=jax.ShapeDtypeStruct((N_P, D), F32),
        grid=(B_P, tiles),
        in_specs=in_specs,
        out_specs=pl.BlockSpec((TM, D), row),
        scratch_shapes=[pltpu.VMEM((NG * SLABS, TM, LANES), F32), pltpu.VMEM((NG * SLABS, TM, LANES), F32),
                        pltpu.VMEM((TM, GMW), BF16)],
        compiler_params=_params(2),
        name="mixer_out",
    )(x, gain, wgate, qkv, qkv, qkv, qkv, qkv, qkv, qkv, bias_tab, u, vs, ws, bias, pa, ps, wo)


def _outproj_sample_body(x_ref, g_ref, wgate_ref, att_ref, u_ref, vs_ref, wt_ref, b_ref,
                         pa_ref, ps_ref, wo_ref, out_ref):
    x = x_ref[...]
    h = _rms(x, g_ref[...]).astype(BF16)
    row = lax.broadcasted_iota(jnp.int32, (N_S, N_S), 0)
    col = lax.broadcasted_iota(jnp.int32, (N_S, N_S), 1)
    keep = (row // T_S == col // T_S) & (col <= row)
    u = u_ref[...]
    vs = vs_ref[...].astype(BF16)
    parts = []
    for g in range(GG):
        corner = wt_ref[g]
        w = jnp.zeros((N_S, N_S), F32)
        for t in range(T_S):
            w = jnp.where(keep & (lax.rem(row, T_S) == t), corner[t:t + 1, :], w)
        w = w.astype(BF16)
        gs = slice(g * GC, (g + 1) * GC)
        parts.append(u[:, gs] * (_dot(w, vs[:, gs]) + b_ref[g]))
    spat = jnp.concatenate(parts, axis=1).astype(BF16)
    out_ref[...] = _mix_out(x, h, att_ref[...], spat, wgate_ref, pa_ref, ps_ref, wo_ref)


def _outproj_sample(x, gain, wgate, att, u, vs_all, wt, bias, pa, ps, wo, layer):
    zero = lambda i: (0, 0)
    in_specs = [pl.BlockSpec((N_S, D), zero), _resident((1, D), layer), _const((D, 2 * D)),
                pl.BlockSpec((N_S, GW), zero), pl.BlockSpec((N_S, GMW), zero),
                pl.BlockSpec((N_S, GMW), lambda i: (layer, 0)),
                _resident((GG, T_S, N_S), layer), _resident((GG, N_S, 1), layer),
                _const((GW, D)), _const((GMW, D)), _const((D, D))]
    return pl.pallas_call(
        _outproj_sample_body,
        out_shape=jax.ShapeDtypeStruct((N_S, D), F32),
        grid=(1,),
        in_specs=in_specs,
        out_specs=pl.BlockSpec((N_S, D), zero),
        compiler_params=_params(1),
        name="outproj_sample",
    )(x, gain, wgate, att, u, vs_all, wt, bias, pa, ps, wo)


def kernel(x_prompt, x_sample, cache_kv_w128, cache_kv_w512, cache_kv_w2048, ffn1_norm, ffn1_gate, ffn1_up, ffn1_down, mix_norm, w_in, gmlp_v_norm, gmlp_ws, gmlp_bias, proj_att, proj_spatial, w_out, ffn2_norm, ffn2_gate, ffn2_up, ffn2_down, final_norm):
    caches = (cache_kv_w128, cache_kv_w512, cache_kv_w2048)
    xp = x_prompt.reshape(N_P, D)
    xs = x_sample.reshape(N_S, D)

    ffn1_jobs = lambda l: [_CastJob(ffn1_gate, l), _CastJob(ffn1_up, l), _CastJob(ffn1_down, l)]
    ffn2_jobs = lambda l: [_CastJob(ffn2_gate, l), _CastJob(ffn2_up, l), _CastJob(ffn2_down, l)]
    mix_jobs = lambda l: [_CastJob(w_in, l, [(0, QKVUV), (QKVUV, QKVUV + 2 * D)]),
                          _CastJob(proj_att, l), _CastJob(proj_spatial, l), _CastJob(w_out, l)]

    n1 = ffn1_norm.reshape(DEPTH, 1, D)
    n2 = ffn2_norm.reshape(DEPTH, 1, D)
    nm = mix_norm.reshape(DEPTH, 1, D)
    nv = gmlp_v_norm.reshape(DEPTH, 1, GMW)
    nf = final_norm.reshape(1, D)
    bias_p = gmlp_bias.reshape(DEPTH, GG, CHUNK, 1)
    wt_s = jnp.tile(gmlp_ws[:, :, :T_S, :T_S], (1, 1, 1, B_S))
    bias_s = jnp.tile(gmlp_bias[:, :, :T_S], (1, 1, B_S)).reshape(DEPTH, GG, N_S, 1)

    attn_bias = jnp.asarray(_prompt_bias_table())
    sample_tabs = [jnp.asarray(t) for t in _sample_bias_tables()]

    f1 = {0: _cast(ffn1_jobs(0))}
    f2, mix = {}, {}
    kv_p = kv_s = None
    for l in range(DEPTH):
        last = l == DEPTH - 1
        fin = nf if last else None
        nxt = [] if last else ffn1_jobs(l + 1) + mix_jobs(l + 1)
        if l == 0:
            xp, outs, _ = _ffn(xp, n1, l, f1[l], jobs=mix_jobs(l) + ffn2_jobs(l))
            mix[l], f2[l] = outs[:5], outs[5:]
        xs, _, _ = _ffn(xs, n1, l, f1[l])
        z_s, ks1, ks2, ks3, u_s, vs_s = _inproj_sample(xs, nm, mix[l][0], nv, l, kv_s)
        kv_s = (ks1, ks2, ks3, vs_s)
        attn = _attn_sample_operands(z_s, caches, sample_tabs, l)
        if l > 0:
            xp, f2[l], att = _ffn(xp, n1, l, f1[l], jobs=ffn2_jobs(l), attn=attn)
        qkv, kv1, kv2, kv3, u, vs = _inproj_prompt(xp, nm, mix[l][0], nv, l, kv_p)
        kv_p = (kv1, kv2, kv3)
        xp = _mixer_out(xp, nm, mix[l][1], qkv, attn_bias, u, vs, gmlp_ws, bias_p, *mix[l][2:], l)
        if l == 0:
            xp, outs, att = _ffn(xp, n2, l, f2[l], fin, jobs=nxt, attn=attn)
        else:
            xp, outs, _ = _ffn(xp, n2, l, f2[l], fin, jobs=nxt)
        if not last:
            f1[l + 1], mix[l + 1] = outs[:3], outs[3:]
        att_s = att[:, :T_S].reshape(N_S, GW)
        xs = _outproj_sample(xs, nm, mix[l][1], att_s, u_s, vs_s, wt_s, bias_s, *mix[l][2:], l)
        xs, _, _ = _ffn(xs, n2, l, f2[l], fin)

    kv_out = [k.reshape(DEPTH, B_P, 2, NH, HD, w).transpose(0, 1, 5, 2, 3, 4) for k, w in zip(kv_p, WINDOWS)]
    ks_shape = (DEPTH, B_S, T_S, 2, NH, HD)
    return (xp.reshape(B_P, S_P, D), xs.reshape(B_S, T_S, D), kv_out[0], kv_out[1], kv_out[2],
            kv_s[0].reshape(ks_shape), kv_s[1].reshape(ks_shape), kv_s[2].reshape(ks_shape),
            kv_s[3].reshape(DEPTH, B_S, T_S, GMW))
```

```python
import functools

import numpy as np
import jax
import jax.numpy as jnp
from jax import lax
from jax.experimental import pallas as pl
from jax.experimental.pallas import tpu as pltpu

F32 = jnp.float32
BF16 = jnp.bfloat16

D = 1024
DEPTH = 2
B_P, S_P = 8, 2048
B_S, T_S = 32, 4
N_P = B_P * S_P
N_S = B_S * T_S
HD, NH, NG = 64, 4, 3
GW = NH * HD
ATT = NG * GW
WINDOWS = (128, 512, 2048)
DILS = (1, 4, 16)
KB = 128
CHUNK = 128
GG, GC = 4, 128
GMW = GG * GC
DFF = 2816
FC = 256
QKVUV = 3 * ATT + 2 * GMW
EPS = 1e-6
NEG = -1e30
TM = 512
TM_FFN = 1024
SPAD = 16
LANES = 128
SUBLANES_BF16 = 16
SLABS = GW // LANES
N_SLAB = 3 * ATT // LANES
VMEM_LIMIT = 56 * 1024 * 1024

_NT = (((1,), (1,)), ((), ()))


def _rms(x, gain):
    r = lax.rsqrt(jnp.mean(x * x, axis=-1, keepdims=True) + EPS)
    return (x * r) * gain


def _dot(a, b):
    return jnp.dot(a, b, preferred_element_type=F32)


def _dot_nt(a, b):
    return lax.dot_general(a, b, _NT, preferred_element_type=F32)


def _params(n_axes):
    return pltpu.CompilerParams(dimension_semantics=("arbitrary",) * n_axes,
                                vmem_limit_bytes=VMEM_LIMIT)


def _resident(shape, layer):
    nd = len(shape)
    return pl.BlockSpec((None,) + shape, lambda *_: (layer,) + (0,) * nd,
                        pipeline_mode=pl.Buffered(1))


def _const(shape):
    nd = len(shape)
    return pl.BlockSpec(shape, lambda *_: (0,) * nd, pipeline_mode=pl.Buffered(1))


class _CastJob:
    def __init__(self, src, layer, col_ranges=None):
        self.src, self.layer = src, layer
        self.rows, self.cols = src.shape[1:]
        self.col_ranges = col_ranges or [(0, self.cols)]

    def specs(self, n_steps):
        rps = self.rows // n_steps
        if rps % SUBLANES_BF16:
            rps = self.rows // (n_steps // 2)
        assert rps % SUBLANES_BF16 == 0 and self.rows % rps == 0
        last = self.rows // rps - 1
        layer = self.layer
        in_spec = pl.BlockSpec((None, rps, self.cols), lambda i: (layer, jnp.minimum(i, last), 0))
        out_specs = [pl.BlockSpec((rps, c1 - c0), lambda i: (jnp.minimum(i, last), 0))
                     for c0, c1 in self.col_ranges]
        out_shapes = [jax.ShapeDtypeStruct((self.rows, c1 - c0), BF16) for c0, c1 in self.col_ranges]
        return in_spec, out_specs, out_shapes


def _run_casts(jobs, src_refs, out_refs):
    out_refs = list(out_refs)
    for job, src_ref in zip(jobs, src_refs):
        x = src_ref[...]
        for c0, c1 in job.col_ranges:
            out_refs.pop(0)[...] = x[:, c0:c1].astype(BF16)


def _cast_specs(jobs, n_steps):
    in_specs, out_specs, out_shapes = [], [], []
    for job in jobs:
        i, o, s = job.specs(n_steps)
        in_specs.append(i)
        out_specs += o
        out_shapes += s
    return in_specs, out_specs, out_shapes


def _cast_body(*refs, jobs):
    _run_casts(jobs, refs[:len(jobs)], refs[len(jobs):])


def _cast(jobs):
    n_steps = 16
    in_specs, out_specs, out_shapes = _cast_specs(jobs, n_steps)
    return pl.pallas_call(
        functools.partial(_cast_body, jobs=jobs),
        out_shape=out_shapes,
        grid=(n_steps,),
        in_specs=in_specs,
        out_specs=out_specs,
        compiler_params=_params(1),
        name="cast_bf16",
    )(*[j.src for j in jobs])


def _ffn_body(*refs, tail, jobs):
    refs = list(refs)
    x_ref, g_ref, wg_ref, wu_ref, wd_ref = refs[:5]
    del refs[:5]
    tg_ref = refs.pop(0) if tail else None
    cast_srcs = [refs.pop(0) for _ in jobs]
    o_ref = refs.pop(0)
    h_ref = refs.pop(0) if tail == "next" else None
    cast_outs = [refs.pop(0) for _ in range(sum(len(j.col_ranges) for j in jobs))]
    a_scr, = refs

    _run_casts(jobs, cast_srcs, cast_outs)
    x = x_ref[...]
    h = _rms(x, g_ref[...]).astype(BF16)
    for c in range(DFF // FC):
        cs = slice(c * FC, (c + 1) * FC)
        gate = _dot(h, wg_ref[:, cs])
        up = _dot(h, wu_ref[:, cs])
        a_scr[:, cs] = (jax.nn.silu(gate) * up).astype(BF16)
    y = x + 0.5 * _dot(a_scr[...], wd_ref[...])
    if tail == "final":
        y = _rms(y, tg_ref[...])
    o_ref[...] = y
    if tail == "next":
        h_ref[...] = _rms(y, tg_ref[...]).astype(BF16)


def _ffn(x, gain, layer, weights, final_gain=None, next_gain=None, jobs=()):
    n = x.shape[0]
    tm = min(TM_FFN, n)
    n_steps = n // tm
    assert final_gain is None or next_gain is None
    tail = "final" if final_gain is not None else "next" if next_gain is not None else None
    jobs = list(jobs)
    row = pl.BlockSpec((tm, D), lambda i: (i, 0))
    in_specs = [row, _resident((1, D), layer), _const((D, DFF)), _const((D, DFF)), _const((DFF, D))]
    args = [x, gain, *weights]
    out_specs, out_shape = [row], [jax.ShapeDtypeStruct((n, D), F32)]
    if tail == "final":
        in_specs.append(_const((1, D)))
        args.append(final_gain)
    elif tail == "next":
        in_specs.append(_resident((1, D), layer))
        args.append(next_gain)
        out_specs.append(row)
        out_shape.append(jax.ShapeDtypeStruct((n, D), BF16))
    cast_in, cast_out, cast_shapes = _cast_specs(jobs, n_steps)
    in_specs += cast_in
    args += [j.src for j in jobs]
    out_specs += cast_out
    out_shape += cast_shapes
    outs = pl.pallas_call(
        functools.partial(_ffn_body, tail=tail, jobs=jobs),
        out_shape=out_shape,
        grid=(n_steps,),
        in_specs=in_specs,
        out_specs=out_specs,
        scratch_shapes=[pltpu.VMEM((tm, DFF), BF16)],
        compiler_params=_params(1),
        name="ffn_" + tail if tail else "ffn",
    )(*args)
    n_head = len(out_shape) - len(cast_shapes)
    return outs[0], (outs[1] if tail == "next" else None), list(outs[n_head:])


def _project(h, w_ref, sec, g):
    c0 = sec * ATT + g * GW
    z = _dot(h, w_ref[:, c0:c0 + GW])
    if sec == 0:
        z = z * (HD ** -0.5)
    return z


def _gmlp_inputs(h, w_ref, vn_ref, u_ref, vs_ref):
    u_ref[...] = jax.nn.gelu(_dot(h, w_ref[:, 3 * ATT:3 * ATT + GMW]))
    gv = jax.nn.gelu(_dot(h, w_ref[:, 3 * ATT + GMW:QKVUV]))
    vs_ref[...] = _rms(gv, vn_ref[...]).astype(vs_ref.dtype)


def _inproj_prompt_body(*refs, n_alias, with_attn):
    refs = list(refs)
    h_ref, w_ref, vn_ref = refs[:3]
    del refs[:3 + n_alias]
    attn_in = [refs.pop(0) for _ in range(7)] if with_attn else None
    qkv_ref, kv1_ref, kv2_ref, kv3_ref, u_ref, vs_ref = refs[:6]
    if with_attn:
        _attn_sample_body(*attn_in, refs[6])
    h = h_ref[...]
    _gmlp_inputs(h, w_ref, vn_ref, u_ref, vs_ref)
    for sec in range(3):
        for g in range(NG):
            z = _project(h, w_ref, sec, g)
            for sl in range(SLABS):
                qkv_ref[(sec * NG + g) * SLABS + sl] = z[:, sl * LANES:(sl + 1) * LANES]
            if sec == 0:
                continue
            rows = slice((sec - 1) * GW, sec * GW)
            if g == 2:
                kv3_ref[rows, :] = z.T
            elif g == 1:
                kv2_ref[rows, :] = z.T
            else:
                kv1_ref[rows, :] = z[TM - WINDOWS[0]:, :].T


def _inproj_prompt(h, w, vnorm, layer, prev, attn=None):
    tiles = S_P // TM
    n_alias = 0 if prev is None else 3
    in_specs = [pl.BlockSpec((TM, D), lambda i: (i, 0)), _const((D, QKVUV)), _resident((1, GMW), layer)]
    in_specs += [pl.BlockSpec(memory_space=pl.ANY)] * n_alias
    out_shape = [jax.ShapeDtypeStruct((B_P, N_SLAB, S_P, LANES), F32),
                 jax.ShapeDtypeStruct((DEPTH * B_P, 2 * GW, WINDOWS[0]), F32),
                 jax.ShapeDtypeStruct((DEPTH * B_P, 2 * GW, WINDOWS[1]), F32),
                 jax.ShapeDtypeStruct((DEPTH * B_P, 2 * GW, WINDOWS[2]), F32),
                 jax.ShapeDtypeStruct((N_P, GMW), F32),
                 jax.ShapeDtypeStruct((N_P, GMW), BF16)]
    seq = lambda i: (layer * B_P + i // tiles, 0, 0)
    out_specs = [pl.BlockSpec((None, N_SLAB, TM, LANES), lambda i: (i // tiles, 0, lax.rem(i, tiles), 0)),
                 pl.BlockSpec((None, 2 * GW, WINDOWS[0]), seq),
                 pl.BlockSpec((None, 2 * GW, WINDOWS[1]), seq),
                 pl.BlockSpec((None, 2 * GW, TM), lambda i: (layer * B_P + i // tiles, 0, lax.rem(i, tiles))),
                 pl.BlockSpec((TM, GMW), lambda i: (i, 0)),
                 pl.BlockSpec((TM, GMW), lambda i: (i, 0))]
    args = [h, w, vnorm] + ([] if prev is None else list(prev))
    if attn is not None:
        assert N_P // TM == B_S
        attn_args, attn_in_specs, attn_out_spec, attn_out_shape = attn
        in_specs += attn_in_specs
        args += attn_args
        out_specs.append(attn_out_spec)
        out_shape.append(attn_out_shape)
    outs = pl.pallas_call(
        functools.partial(_inproj_prompt_body, n_alias=n_alias, with_attn=attn is not None),
        out_shape=out_shape,
        grid=(N_P // TM,),
        in_specs=in_specs,
        out_specs=out_specs,
        input_output_aliases={3 + k: 1 + k for k in range(n_alias)},
        compiler_params=_params(1),
        name="inproj_prompt",
    )(*args)
    return (*outs[:6], outs[6] if attn is not None else None)


def _inproj_sample_body(*refs, n_alias):
    x_ref, g_ref, w_ref, vn_ref = refs[:4]
    qkv_ref, kv1_ref, kv2_ref, kv3_ref, u_ref, vs_ref = refs[4 + n_alias:]
    kv_refs = (kv1_ref, kv2_ref, kv3_ref)
    h = _rms(x_ref[...], g_ref[...]).astype(BF16)
    for sec in range(3):
        for g in range(NG):
            z = _project(h, w_ref, sec, g)
            qkv_ref[:, sec * ATT + g * GW:sec * ATT + (g + 1) * GW] = z
            if sec > 0:
                kv_refs[g][:, (sec - 1) * GW:sec * GW] = z
    _gmlp_inputs(h, w_ref, vn_ref, u_ref, vs_ref)


def _inproj_sample(x, gain, w, vnorm, layer, prev):
    n_alias = 0 if prev is None else 4
    in_specs = [pl.BlockSpec((N_S, D), lambda i: (0, 0)),
                _resident((1, D), layer), _const((D, QKVUV)), _resident((1, GMW), layer)]
    in_specs += [pl.BlockSpec(memory_space=pl.ANY)] * n_alias
    stacked = jax.ShapeDtypeStruct((DEPTH * N_S, 2 * GW), F32)
    out_shape = [jax.ShapeDtypeStruct((N_S, 3 * ATT), F32), stacked, stacked, stacked,
                 jax.ShapeDtypeStruct((N_S, GMW), F32),
                 jax.ShapeDtypeStruct((DEPTH * N_S, GMW), F32)]
    lay = lambda i: (layer, 0)
    out_specs = [pl.BlockSpec((N_S, 3 * ATT), lambda i: (0, 0)),
                 pl.BlockSpec((N_S, 2 * GW), lay), pl.BlockSpec((N_S, 2 * GW), lay),
                 pl.BlockSpec((N_S, 2 * GW), lay),
                 pl.BlockSpec((N_S, GMW), lambda i: (0, 0)),
                 pl.BlockSpec((N_S, GMW), lay)]
    args = [x, gain, w, vnorm] + ([] if prev is None else list(prev))
    aliases = {4: 1, 5: 2, 6: 3, 7: 5} if prev is not None else {}
    return pl.pallas_call(
        functools.partial(_inproj_sample_body, n_alias=n_alias),
        out_shape=out_shape,
        grid=(1,),
        in_specs=in_specs,
        out_specs=out_specs,
        input_output_aliases=aliases,
        compiler_params=_params(1),
        name="inproj_sample",
    )(*args)


def _alibi_slopes():
    h = np.arange(1, NG * NH + 1, dtype=np.float32)
    return np.power(np.float32(2.0), -8.0 * h / (NG * NH)).astype(np.float32).reshape(NG, NH)


def _head_masks():
    lane = lax.broadcasted_iota(jnp.int32, (1, GW), 1)
    return [(lane >= h * HD) & (lane < (h + 1) * HD) for h in range(NH)]


def _stack_heads(q, masks):
    return jnp.concatenate([jnp.where(m, q, jnp.zeros_like(q)) for m in masks], axis=0)


def _unstack_heads(x, masks, rows):
    out = jnp.zeros((rows, GW), F32)
    for h, m in enumerate(masks):
        out = jnp.where(m, x[h * rows:(h + 1) * rows], out)
    return out


def _softmax_parts(parts, values, value_dots):
    m = functools.reduce(jnp.maximum, [jnp.max(s, axis=-1, keepdims=True) for s in parts])
    ps = [jnp.exp(s - m) for s in parts]
    l = functools.reduce(jnp.add, [jnp.sum(p, axis=-1, keepdims=True) for p in ps])
    o = functools.reduce(jnp.add, [dot(p.astype(BF16), v) for p, v, dot in zip(ps, values, value_dots)])
    return o / l, m + jnp.log(l)


def _merge_groups(outs, lses):
    m = jnp.maximum(jnp.maximum(lses[0], lses[1]), lses[2])
    es = [jnp.exp(l - m) for l in lses]
    tot = es[0] + es[1] + es[2]
    return (es[0] / tot) * outs[0] + (es[1] / tot) * outs[1] + (es[2] / tot) * outs[2]


def _prompt_bias_table():
    slopes = _alibi_slopes()
    a = np.arange(KB)[:, None]
    c = np.arange(2 * KB)[None, :]
    dist = KB + a - c
    valid = (dist >= 0) & (dist <= KB)
    tab = np.empty((NG, NH * KB, 2 * KB), np.float32)
    for g in range(NG):
        for h in range(NH):
            bias = -slopes[g, h] * (DILS[g] * dist).astype(np.float32)
            tab[g, h * KB:(h + 1) * KB] = np.where(valid, bias, np.float32(NEG))
    return tab


def _sample_bias_tables():
    slopes = _alibi_slopes()
    j = np.arange(SPAD)[:, None]
    tabs = []
    for g in range(NG):
        w, d = WINDOWS[g], DILS[g]
        c = np.arange(w)[None, :]
        cn = np.arange(KB)[None, :]
        dist = np.concatenate([np.broadcast_to(w + j - c, (SPAD, w)),
                               np.broadcast_to(j - cn, (SPAD, KB))], axis=1)
        real = np.concatenate([np.ones((SPAD, w), bool), np.broadcast_to(cn < T_S, (SPAD, KB))], axis=1)
        valid = (j < T_S) & real & (dist >= 0) & (dist <= w) & (dist % d == 0)
        pad_row = np.broadcast_to(j >= T_S, dist.shape)
        tab = np.empty((NH * SPAD, dist.shape[1]), np.float32)
        for h in range(NH):
            bias = -slopes[g, h] * dist.astype(np.float32)
            blk = np.where(valid, bias, np.float32(NEG))
            tab[h * SPAD:(h + 1) * SPAD] = np.where(pad_row, np.float32(0.0), blk)
        tabs.append(tab)
    return tabs


def _attn_sample_body(z_ref, c1_ref, c2_ref, c3_ref, b1_ref, b2_ref, b3_ref, att_ref):
    masks = _head_masks()
    z = z_ref[...]
    pad = jnp.zeros((KB - SPAD, GW), F32)
    outs, lses = [], []
    for g, (c_ref, b_ref) in enumerate(((c1_ref, b1_ref), (c2_ref, b2_ref), (c3_ref, b3_ref))):
        w = WINDOWS[g]
        q = z[:, g * GW:(g + 1) * GW]
        k_new = jnp.concatenate([z[:, ATT + g * GW:ATT + (g + 1) * GW], pad], axis=0).astype(BF16)
        v_new = jnp.concatenate([z[:, 2 * ATT + g * GW:2 * ATT + (g + 1) * GW], pad], axis=0).astype(BF16)
        k_old = c_ref[:GW, :].astype(BF16)
        v_old = c_ref[GW:, :].astype(BF16)
        qs = _stack_heads(q, masks).astype(BF16)
        parts = [_dot(qs, k_old) + b_ref[:, :w], _dot_nt(qs, k_new) + b_ref[:, w:]]
        o, lse = _softmax_parts(parts, [v_old, v_new], [_dot_nt, _dot])
        outs.append(_unstack_heads(o, masks, SPAD))
        lses.append(_unstack_heads(lse, masks, SPAD))
    att_ref[...] = _merge_groups(outs, lses)


def _attn_sample_operands(z, caches, tabs, layer):
    zp = jnp.pad(z.reshape(B_S, T_S, 3 * ATT), ((0, 0), (0, SPAD - T_S), (0, 0)))
    cs = [c.transpose(0, 1, 3, 4, 5, 2).reshape(DEPTH, B_S, 2 * GW, w) for c, w in zip(caches, WINDOWS)]
    seq = lambda b: (layer, b, 0, 0)
    in_specs = [pl.BlockSpec((None, SPAD, 3 * ATT), lambda b: (b, 0, 0))]
    in_specs += [pl.BlockSpec((None, None, 2 * GW, w), seq) for w in WINDOWS]
    in_specs += [_const(t.shape) for t in tabs]
    out_spec = pl.BlockSpec((None, SPAD, GW), lambda b: (b, 0, 0))
    return [zp, *cs, *tabs], in_specs, out_spec, jax.ShapeDtypeStruct((B_S, SPAD, GW), F32)


def _mix_out(x, h, att, spat, wgate_ref, pa_ref, ps_ref, wo_ref):
    ga = jax.nn.sigmoid(_dot(h, wgate_ref[:, :D]))
    gb = jax.nn.sigmoid(_dot(h, wgate_ref[:, D:]))
    mix = ga * _dot(att.astype(BF16), pa_ref[...]) + gb * _dot(spat, ps_ref[...])
    return x + _dot(mix.astype(BF16), wo_ref[...])


def _mixer_out_body(x_ref, h_ref, wgate_ref, qkv_ref, kp1_ref, vp1_ref, kp2_ref, vp2_ref, kh3_ref, vh3_ref,
                    bias_ref, u_ref, vs_ref, ws_ref, b_ref, pa_ref, ps_ref, wo_ref, out_ref,
                    o_scr, l_scr, spat_scr):
    j = pl.program_id(1)
    masks = _head_masks()
    lane = lax.broadcasted_iota(jnp.int32, (1, 2 * KB), 1)
    first_tile_mask = (j > 0) | (lane >= KB)

    def wide(ref, base, rows):
        return jnp.concatenate([ref[base + sl, rows, :] for sl in range(SLABS)], axis=1)

    def tile(sec, g, rows):
        return wide(qkv_ref, (sec * NG + g) * SLABS, rows)

    def band(g, q, k, v, bias, rows, n):
        qs = _stack_heads(q.astype(BF16), masks)
        s = _dot_nt(qs, k.astype(BF16)) + bias
        m = jnp.max(s, axis=-1, keepdims=True)
        p = jnp.exp(s - m)
        l = jnp.sum(p, axis=-1, keepdims=True)
        o = _unstack_heads(_dot(p.astype(BF16), v.astype(BF16)) / l, masks, n)
        lse = _unstack_heads(m + jnp.log(l), masks, n)
        for sl in range(SLABS):
            o_scr[g * SLABS + sl, rows, :] = o[:, sl * LANES:(sl + 1) * LANES]
            l_scr[g * SLABS + sl, rows, :] = lse[:, sl * LANES:(sl + 1) * LANES]

    bias1 = bias_ref[0]
    for blk in range(TM // KB):
        own = pl.ds(blk * KB, KB)
        if blk == 0:
            k = jnp.concatenate([wide(kp1_ref, 0, slice(None)), tile(1, 0, own)], axis=0)
            v = jnp.concatenate([wide(vp1_ref, 0, slice(None)), tile(2, 0, own)], axis=0)
            bias = jnp.where(first_tile_mask, bias1, NEG)
        else:
            both = pl.ds((blk - 1) * KB, 2 * KB)
            k, v, bias = tile(1, 0, both), tile(2, 0, both), bias1
        band(0, tile(0, 0, own), k, v, bias, own, KB)

    bias2 = jnp.where(first_tile_mask, bias_ref[1], NEG)
    for r in range(DILS[1]):
        rows = pl.ds(r, KB, stride=DILS[1])
        k = jnp.concatenate([wide(kp2_ref, 0, rows), tile(1, 1, rows)], axis=0)
        v = jnp.concatenate([wide(vp2_ref, 0, rows), tile(2, 1, rows)], axis=0)
        band(1, tile(0, 1, rows), k, v, bias2, rows, KB)

    nq = TM // DILS[2]
    q0 = pl.multiple_of(j * nq, nq)
    bias3 = jnp.concatenate([bias_ref[2, pl.ds(h * KB + q0, nq), KB:] for h in range(NH)], axis=0)
    for r in range(DILS[2]):
        rows = pl.ds(r, nq, stride=DILS[2])
        keys = pl.ds(r, KB, stride=DILS[2])
        band(2, tile(0, 2, rows), wide(kh3_ref, 0, keys), wide(vh3_ref, 0, keys), bias3, rows, nq)

    full = slice(None)
    att = _merge_groups([wide(o_scr, g * SLABS, full) for g in range(NG)],
                        [wide(l_scr, g * SLABS, full) for g in range(NG)])

    x = x_ref[...]
    h = h_ref[...]
    row = lax.broadcasted_iota(jnp.int32, (CHUNK, CHUNK), 0)
    col = lax.broadcasted_iota(jnp.int32, (CHUNK, CHUNK), 1)
    for g in range(GG):
        w = jnp.where(col <= row, ws_ref[g], 0.0).astype(BF16)
        gs = slice(g * GC, (g + 1) * GC)
        for c in range(TM // CHUNK):
            rs = slice(c * CHUNK, (c + 1) * CHUNK)
            zc = _dot(w, vs_ref[rs, gs]) + b_ref[g]
            spat_scr[rs, gs] = (u_ref[rs, gs] * zc).astype(BF16)
    out_ref[...] = _mix_out(x, h, att, spat_scr[...], wgate_ref, pa_ref, ps_ref, wo_ref)


def _mixer_out(x, h, wgate, qkv, bias_tab, u, vs, ws, bias, pa, ps, wo, layer):
    tiles = S_P // TM
    assert TM == KB * DILS[1] and S_P == KB * DILS[2]
    row = lambda b, j: (b * tiles + j, 0)

    def slabs(rows, sec, g, row_blk):
        return pl.BlockSpec((None, SLABS, rows, LANES), lambda b, j: (b, sec * NG + g, row_blk(j), 0))

    before1 = lambda j: jnp.maximum(j * (TM // KB) - 1, 0)
    before2 = lambda j: jnp.maximum(j - 1, 0)
    whole = lambda j: 0
    in_specs = [pl.BlockSpec((TM, D), row), pl.BlockSpec((TM, D), row), _const((D, 2 * D)),
                pl.BlockSpec((None, N_SLAB, TM, LANES), lambda b, j: (b, 0, j, 0)),
                slabs(KB, 1, 0, before1), slabs(KB, 2, 0, before1),
                slabs(TM, 1, 1, before2), slabs(TM, 2, 1, before2),
                slabs(S_P, 1, 2, whole), slabs(S_P, 2, 2, whole),
                _const(bias_tab.shape),
                pl.BlockSpec((TM, GMW), row), pl.BlockSpec((TM, GMW), row),
                _resident((GG, CHUNK, CHUNK), layer), _resident((GG, CHUNK, 1), layer),
                _const((GW, D)), _const((GMW, D)), _const((D, D))]
    return pl.pallas_call(
        _mixer_out_body,
        out_shape=jax.ShapeDtypeStruct((N_P, D), F32),
        grid=(B_P, tiles),
        in_specs=in_specs,
        out_specs=pl.BlockSpec((TM, D), row),
        scratch_shapes=[pltpu.VMEM((NG * SLABS, TM, LANES), F32), pltpu.VMEM((NG * SLABS, TM, LANES), F32),
                        pltpu.VMEM((TM, GMW), BF16)],
        compiler_params=_params(2),
        name="mixer_out",
    )(x, h, wgate, qkv, qkv, qkv, qkv, qkv, qkv, qkv, bias_tab, u, vs, ws, bias, pa, ps, wo)


def _outproj_sample_body(x_ref, g_ref, wgate_ref, att_ref, u_ref, vs_ref, wt_ref, b_ref,
                         pa_ref, ps_ref, wo_ref, out_ref):
    x = x_ref[...]
    h = _rms(x, g_ref[...]).astype(BF16)
    row = lax.broadcasted_iota(jnp.int32, (N_S, N_S), 0)
    col = lax.broadcasted_iota(jnp.int32, (N_S, N_S), 1)
    keep = (row // T_S == col // T_S) & (col <= row)
    u = u_ref[...]
    vs = vs_ref[...].astype(BF16)
    parts = []
    for g in range(GG):
        corner = wt_ref[g]
        w = jnp.zeros((N_S, N_S), F32)
        for t in range(T_S):
            w = jnp.where(keep & (lax.rem(row, T_S) == t), corner[t:t + 1, :], w)
        w = w.astype(BF16)
        gs = slice(g * GC, (g + 1) * GC)
        parts.append(u[:, gs] * (_dot(w, vs[:, gs]) + b_ref[g]))
    spat = jnp.concatenate(parts, axis=1).astype(BF16)
    out_ref[...] = _mix_out(x, h, att_ref[...], spat, wgate_ref, pa_ref, ps_ref, wo_ref)


def _outproj_sample(x, gain, wgate, att, u, vs_all, wt, bias, pa, ps, wo, layer):
    zero = lambda i: (0, 0)
    in_specs = [pl.BlockSpec((N_S, D), zero), _resident((1, D), layer), _const((D, 2 * D)),
                pl.BlockSpec((N_S, GW), zero), pl.BlockSpec((N_S, GMW), zero),
                pl.BlockSpec((N_S, GMW), lambda i: (layer, 0)),
                _resident((GG, T_S, N_S), layer), _resident((GG, N_S, 1), layer),
                _const((GW, D)), _const((GMW, D)), _const((D, D))]
    return pl.pallas_call(
        _outproj_sample_body,
        out_shape=jax.ShapeDtypeStruct((N_S, D), F32),
        grid=(1,),
        in_specs=in_specs,
        out_specs=pl.BlockSpec((N_S, D), zero),
        compiler_params=_params(1),
        name="outproj_sample",
    )(x, gain, wgate, att, u, vs_all, wt, bias, pa, ps, wo)


def kernel(x_prompt, x_sample, cache_kv_w128, cache_kv_w512, cache_kv_w2048, ffn1_norm, ffn1_gate, ffn1_up, ffn1_down, mix_norm, w_in, gmlp_v_norm, gmlp_ws, gmlp_bias, proj_att, proj_spatial, w_out, ffn2_norm, ffn2_gate, ffn2_up, ffn2_down, final_norm):
    caches = (cache_kv_w128, cache_kv_w512, cache_kv_w2048)
    xp = x_prompt.reshape(N_P, D)
    xs = x_sample.reshape(N_S, D)

    ffn1_jobs = lambda l: [_CastJob(ffn1_gate, l), _CastJob(ffn1_up, l), _CastJob(ffn1_down, l)]
    ffn2_jobs = lambda l: [_CastJob(ffn2_gate, l), _CastJob(ffn2_up, l), _CastJob(ffn2_down, l)]
    mix_jobs = lambda l: [_CastJob(w_in, l, [(0, QKVUV), (QKVUV, QKVUV + 2 * D)]),
                          _CastJob(proj_att, l), _CastJob(proj_spatial, l), _CastJob(w_out, l)]

    n1 = ffn1_norm.reshape(DEPTH, 1, D)
    n2 = ffn2_norm.reshape(DEPTH, 1, D)
    nm = mix_norm.reshape(DEPTH, 1, D)
    nv = gmlp_v_norm.reshape(DEPTH, 1, GMW)
    nf = final_norm.reshape(1, D)
    bias_p = gmlp_bias.reshape(DEPTH, GG, CHUNK, 1)
    wt_s = jnp.tile(gmlp_ws[:, :, :T_S, :T_S], (1, 1, 1, B_S))
    bias_s = jnp.tile(gmlp_bias[:, :, :T_S], (1, 1, B_S)).reshape(DEPTH, GG, N_S, 1)

    attn_bias = jnp.asarray(_prompt_bias_table())
    sample_tabs = [jnp.asarray(t) for t in _sample_bias_tables()]

    f1 = {0: _cast(ffn1_jobs(0))}
    f2, mix = {}, {}
    kv_p = kv_s = None
    for l in range(DEPTH):
        last = l == DEPTH - 1
        fin = nf if last else None
        xp, hp, outs = _ffn(xp, n1, l, f1[l], next_gain=nm, jobs=ffn2_jobs(l) + (mix_jobs(l) if l == 0 else []))
        f2[l] = outs[:3]
        if l == 0:
            mix[l] = outs[3:]
        xs, _, _ = _ffn(xs, n1, l, f1[l])
        z_s, ks1, ks2, ks3, u_s, vs_s = _inproj_sample(xs, nm, mix[l][0], nv, l, kv_s)
        kv_s = (ks1, ks2, ks3, vs_s)
        attn = _attn_sample_operands(z_s, caches, sample_tabs, l)
        qkv, kv1, kv2, kv3, u, vs, att = _inproj_prompt(hp, mix[l][0], nv, l, kv_p, attn)
        kv_p = (kv1, kv2, kv3)
        xp = _mixer_out(xp, hp, mix[l][1], qkv, attn_bias, u, vs, gmlp_ws, bias_p, *mix[l][2:], l)
        xp, _, outs = _ffn(xp, n2, l, f2[l], fin, jobs=[] if last else ffn1_jobs(l + 1) + mix_jobs(l + 1))
        if not last:
            f1[l + 1], mix[l + 1] = outs[:3], outs[3:]
        att_s = att[:, :T_S].reshape(N_S, GW)
        xs = _outproj_sample(xs, nm, mix[l][1], att_s, u_s, vs_s, wt_s, bias_s, *mix[l][2:], l)
        xs, _, _ = _ffn(xs, n2, l, f2[l], fin)

    kv_out = [k.reshape(DEPTH, B_P, 2, NH, HD, w).transpose(0, 1, 5, 2, 3, 4) for k, w in zip(kv_p, WINDOWS)]
    ks_shape = (DEPTH, B_S, T_S, 2, NH, HD)
    return (xp.reshape(B_P, S_P, D), xs.reshape(B_S, T_S, D), kv_out[0], kv_out[1], kv_out[2],
            kv_s[0].reshape(ks_shape), kv_s[1].reshape(ks_shape), kv_s[2].reshape(ks_shape),
            kv_s[3].reshape(DEPTH, B_S, T_S, GMW))
```

```python
import functools

import numpy as np
import jax
import jax.numpy as jnp
from jax import lax
from jax.experimental import pallas as pl
from jax.experimental.pallas import tpu as pltpu

F32 = jnp.float32
BF16 = jnp.bfloat16

D = 1024
DEPTH = 2
B_P, S_P = 8, 2048
B_S, T_S = 32, 4
N_P = B_P * S_P
N_S = B_S * T_S
HD, NH, NG = 64, 4, 3
GW = NH * HD
ATT = NG * GW
WINDOWS = (128, 512, 2048)
DILS = (1, 4, 16)
KB = 128
CHUNK = 128
GG, GC = 4, 128
GMW = GG * GC
DFF = 2816
FC = 256
QKVUV = 3 * ATT + 2 * GMW
EPS = 1e-6
NEG = -1e30
TM = 512
TM_FFN = 1024
FFN_SUB = 256
SPAD = 16
LANES = 128
SUBLANES_BF16 = 16
SLABS = GW // LANES
N_SLAB = 3 * ATT // LANES
VMEM_LIMIT = 56 * 1024 * 1024

_NT = (((1,), (1,)), ((), ()))


def _rms(x, gain):
    r = lax.rsqrt(jnp.mean(x * x, axis=-1, keepdims=True) + EPS)
    return (x * r) * gain


def _dot(a, b):
    return jnp.dot(a, b, preferred_element_type=F32)


def _dot_nt(a, b):
    return lax.dot_general(a, b, _NT, preferred_element_type=F32)


def _params(n_axes):
    return pltpu.CompilerParams(dimension_semantics=("arbitrary",) * n_axes,
                                vmem_limit_bytes=VMEM_LIMIT)


def _resident(shape, layer):
    nd = len(shape)
    return pl.BlockSpec((None,) + shape, lambda *_: (layer,) + (0,) * nd,
                        pipeline_mode=pl.Buffered(1))


def _const(shape):
    nd = len(shape)
    return pl.BlockSpec(shape, lambda *_: (0,) * nd, pipeline_mode=pl.Buffered(1))


class _CastJob:
    def __init__(self, src, layer, col_ranges=None):
        self.src, self.layer = src, layer
        self.rows, self.cols = src.shape[1:]
        self.col_ranges = col_ranges or [(0, self.cols)]

    def specs(self, n_steps):
        rps = self.rows // n_steps
        if rps % SUBLANES_BF16:
            rps = self.rows // (n_steps // 2)
        assert rps % SUBLANES_BF16 == 0 and self.rows % rps == 0
        last = self.rows // rps - 1
        layer = self.layer
        in_spec = pl.BlockSpec((None, rps, self.cols), lambda i: (layer, jnp.minimum(i, last), 0))
        out_specs = [pl.BlockSpec((rps, c1 - c0), lambda i: (jnp.minimum(i, last), 0))
                     for c0, c1 in self.col_ranges]
        out_shapes = [jax.ShapeDtypeStruct((self.rows, c1 - c0), BF16) for c0, c1 in self.col_ranges]
        return in_spec, out_specs, out_shapes


def _run_casts(jobs, src_refs, out_refs):
    out_refs = list(out_refs)
    for job, src_ref in zip(jobs, src_refs):
        x = src_ref[...]
        for c0, c1 in job.col_ranges:
            out_refs.pop(0)[...] = x[:, c0:c1].astype(BF16)


def _cast_specs(jobs, n_steps):
    in_specs, out_specs, out_shapes = [], [], []
    for job in jobs:
        i, o, s = job.specs(n_steps)
        in_specs.append(i)
        out_specs += o
        out_shapes += s
    return in_specs, out_specs, out_shapes


def _cast_body(*refs, jobs):
    _run_casts(jobs, refs[:len(jobs)], refs[len(jobs):])


def _cast(jobs):
    n_steps = 16
    in_specs, out_specs, out_shapes = _cast_specs(jobs, n_steps)
    return pl.pallas_call(
        functools.partial(_cast_body, jobs=jobs),
        out_shape=out_shapes,
        grid=(n_steps,),
        in_specs=in_specs,
        out_specs=out_specs,
        compiler_params=_params(1),
        name="cast_bf16",
    )(*[j.src for j in jobs])


def _ffn_body(*refs, tail, jobs, with_attn):
    refs = list(refs)
    x_ref, g_ref, wg_ref, wu_ref, wd_ref = refs[:5]
    del refs[:5]
    tg_ref = refs.pop(0) if tail else None
    cast_srcs = [refs.pop(0) for _ in jobs]
    attn_in = [refs.pop(0) for _ in range(7)] if with_attn else None
    o_ref = refs.pop(0)
    h_ref = refs.pop(0) if tail == "next" else None
    cast_outs = [refs.pop(0) for _ in range(sum(len(j.col_ranges) for j in jobs))]
    att_ref = refs.pop(0) if with_attn else None
    a_scr, = refs

    _run_casts(jobs, cast_srcs, cast_outs)
    if with_attn:
        _attn_sample_body(*attn_in, att_ref)
    sub = min(FFN_SUB, x_ref.shape[0])
    for r0 in range(0, x_ref.shape[0], sub):
        rows = slice(r0, r0 + sub)
        x = x_ref[rows, :]
        h = _rms(x, g_ref[...]).astype(BF16)
        for c in range(DFF // FC):
            cs = slice(c * FC, (c + 1) * FC)
            gate = _dot(h, wg_ref[:, cs])
            up = _dot(h, wu_ref[:, cs])
            a_scr[rows, cs] = (jax.nn.silu(gate) * up).astype(BF16)
        y = x + 0.5 * _dot(a_scr[rows, :], wd_ref[...])
        if tail == "final":
            y = _rms(y, tg_ref[...])
        o_ref[rows, :] = y
        if tail == "next":
            h_ref[rows, :] = _rms(y, tg_ref[...]).astype(BF16)


def _ffn(x, gain, layer, weights, final_gain=None, next_gain=None, jobs=(), attn=None):
    n = x.shape[0]
    tm = min(TM if attn is not None else TM_FFN, n)
    n_steps = n // tm
    assert final_gain is None or next_gain is None
    tail = "final" if final_gain is not None else "next" if next_gain is not None else None
    jobs = list(jobs)
    row = pl.BlockSpec((tm, D), lambda i: (i, 0))
    in_specs = [row, _resident((1, D), layer), _const((D, DFF)), _const((D, DFF)), _const((DFF, D))]
    args = [x, gain, *weights]
    out_specs, out_shape = [row], [jax.ShapeDtypeStruct((n, D), F32)]
    if tail == "final":
        in_specs.append(_const((1, D)))
        args.append(final_gain)
    elif tail == "next":
        in_specs.append(_resident((1, D), layer))
        args.append(next_gain)
        out_specs.append(row)
        out_shape.append(jax.ShapeDtypeStruct((n, D), BF16))
    cast_in, cast_out, cast_shapes = _cast_specs(jobs, n_steps)
    in_specs += cast_in
    args += [j.src for j in jobs]
    out_specs += cast_out
    out_shape += cast_shapes
    n_head = len(out_shape) - len(cast_shapes)
    if attn is not None:
        attn_args, attn_in_specs, attn_out_spec, attn_out_shape = attn(n_steps)
        in_specs += attn_in_specs
        args += attn_args
        out_specs.append(attn_out_spec)
        out_shape.append(attn_out_shape)
    outs = pl.pallas_call(
        functools.partial(_ffn_body, tail=tail, jobs=jobs, with_attn=attn is not None),
        out_shape=out_shape,
        grid=(n_steps,),
        in_specs=in_specs,
        out_specs=out_specs,
        scratch_shapes=[pltpu.VMEM((tm, DFF), BF16)],
        compiler_params=_params(1),
        name="ffn_" + tail if tail else "ffn",
    )(*args)
    casts = list(outs[n_head:n_head + len(cast_shapes)])
    return outs[0], (outs[1] if tail == "next" else None), casts, (outs[-1] if attn is not None else None)


def _project(h, w_ref, sec, g):
    c0 = sec * ATT + g * GW
    z = _dot(h, w_ref[:, c0:c0 + GW])
    if sec == 0:
        z = z * (HD ** -0.5)
    return z


def _gmlp_inputs(h, w_ref, vn_ref, u_ref, vs_ref):
    u_ref[...] = jax.nn.gelu(_dot(h, w_ref[:, 3 * ATT:3 * ATT + GMW]))
    gv = jax.nn.gelu(_dot(h, w_ref[:, 3 * ATT + GMW:QKVUV]))
    vs_ref[...] = _rms(gv, vn_ref[...]).astype(vs_ref.dtype)


def _inproj_prompt_body(*refs, n_alias):
    refs = list(refs)
    h_ref, w_ref, vn_ref = refs[:3]
    del refs[:3 + n_alias]
    kv1_ref, kv2_ref, kv3_ref = refs[:3]
    q_refs, kvb_refs = refs[3:9:2], refs[4:9:2]
    u_ref, vs_ref, z_scr = refs[9:]
    h = h_ref[...]
    _gmlp_inputs(h, w_ref, vn_ref, u_ref, vs_ref)
    for sec in range(3):
        for g in range(NG):
            z = _project(h, w_ref, sec, g)
            dst = q_refs[g] if sec == 0 else kvb_refs[g]
            cols = slice(GW, 2 * GW) if sec == 2 else slice(0, GW)
            if g == 0:
                dst[:, cols] = z.astype(BF16)
            else:
                d = DILS[g]
                base = (sec * (NG - 1) + g - 1) * SLABS
                for sl in range(SLABS):
                    z_scr[base + sl] = z[:, sl * LANES:(sl + 1) * LANES]
                for r in range(d):
                    sub = [z_scr[base + sl, pl.ds(r, TM // d, stride=d), :] for sl in range(SLABS)]
                    dst[r, :, cols] = jnp.concatenate(sub, axis=1).astype(BF16)
            if sec == 0:
                continue
            rows = slice((sec - 1) * GW, sec * GW)
            if g == 2:
                kv3_ref[rows, :] = z.T
            elif g == 1:
                kv2_ref[rows, :] = z.T
            else:
                kv1_ref[rows, :] = z[TM - WINDOWS[0]:, :].T


def _inproj_prompt(h, w, vnorm, layer, prev):
    tiles = S_P // TM
    n_alias = 0 if prev is None else 3
    in_specs = [pl.BlockSpec((TM, D), lambda i: (i, 0)), _const((D, QKVUV)), _resident((1, GMW), layer)]
    in_specs += [pl.BlockSpec(memory_space=pl.ANY)] * n_alias
    seq = lambda i: (layer * B_P + i // tiles, 0, 0)
    row = lambda i: (i, 0)
    res = lambda i: (i // tiles, 0, lax.rem(i, tiles), 0)
    out_shape = [jax.ShapeDtypeStruct((DEPTH * B_P, 2 * GW, w), F32) for w in WINDOWS]
    out_specs = [pl.BlockSpec((None, 2 * GW, WINDOWS[0]), seq),
                 pl.BlockSpec((None, 2 * GW, WINDOWS[1]), seq),
                 pl.BlockSpec((None, 2 * GW, TM), lambda i: (layer * B_P + i // tiles, 0, lax.rem(i, tiles)))]
    for g, d in enumerate(DILS):
        for width in (GW, 2 * GW):
            if g == 0:
                out_shape.append(jax.ShapeDtypeStruct((N_P, width), BF16))
                out_specs.append(pl.BlockSpec((TM, width), row))
            else:
                out_shape.append(jax.ShapeDtypeStruct((B_P, d, S_P // d, width), BF16))
                out_specs.append(pl.BlockSpec((None, d, TM // d, width), res))
    out_shape += [jax.ShapeDtypeStruct((N_P, GMW), F32), jax.ShapeDtypeStruct((N_P, GMW), BF16)]
    out_specs += [pl.BlockSpec((TM, GMW), row), pl.BlockSpec((TM, GMW), row)]
    args = [h, w, vnorm] + ([] if prev is None else list(prev))
    outs = pl.pallas_call(
        functools.partial(_inproj_prompt_body, n_alias=n_alias),
        out_shape=out_shape,
        grid=(N_P // TM,),
        in_specs=in_specs,
        out_specs=out_specs,
        scratch_shapes=[pltpu.VMEM((3 * (NG - 1) * SLABS, TM, LANES), F32)],
        input_output_aliases={3 + k: k for k in range(n_alias)},
        compiler_params=_params(1),
        name="inproj_prompt",
    )(*args)
    return tuple(outs[:3]), list(outs[3:9]), outs[9], outs[10]


def _inproj_sample_body(*refs, n_alias):
    x_ref, g_ref, w_ref, vn_ref = refs[:4]
    qkv_ref, kv1_ref, kv2_ref, kv3_ref, u_ref, vs_ref = refs[4 + n_alias:]
    kv_refs = (kv1_ref, kv2_ref, kv3_ref)
    h = _rms(x_ref[...], g_ref[...]).astype(BF16)
    for sec in range(3):
        for g in range(NG):
            z = _project(h, w_ref, sec, g)
            qkv_ref[:, sec * ATT + g * GW:sec * ATT + (g + 1) * GW] = z
            if sec > 0:
                kv_refs[g][:, (sec - 1) * GW:sec * GW] = z
    _gmlp_inputs(h, w_ref, vn_ref, u_ref, vs_ref)


def _inproj_sample(x, gain, w, vnorm, layer, prev):
    n_alias = 0 if prev is None else 4
    in_specs = [pl.BlockSpec((N_S, D), lambda i: (0, 0)),
                _resident((1, D), layer), _const((D, QKVUV)), _resident((1, GMW), layer)]
    in_specs += [pl.BlockSpec(memory_space=pl.ANY)] * n_alias
    stacked = jax.ShapeDtypeStruct((DEPTH * N_S, 2 * GW), F32)
    out_shape = [jax.ShapeDtypeStruct((N_S, 3 * ATT), F32), stacked, stacked, stacked,
                 jax.ShapeDtypeStruct((N_S, GMW), F32),
                 jax.ShapeDtypeStruct((DEPTH * N_S, GMW), F32)]
    lay = lambda i: (layer, 0)
    out_specs = [pl.BlockSpec((N_S, 3 * ATT), lambda i: (0, 0)),
                 pl.BlockSpec((N_S, 2 * GW), lay), pl.BlockSpec((N_S, 2 * GW), lay),
                 pl.BlockSpec((N_S, 2 * GW), lay),
                 pl.BlockSpec((N_S, GMW), lambda i: (0, 0)),
                 pl.BlockSpec((N_S, GMW), lay)]
    args = [x, gain, w, vnorm] + ([] if prev is None else list(prev))
    aliases = {4: 1, 5: 2, 6: 3, 7: 5} if prev is not None else {}
    return pl.pallas_call(
        functools.partial(_inproj_sample_body, n_alias=n_alias),
        out_shape=out_shape,
        grid=(1,),
        in_specs=in_specs,
        out_specs=out_specs,
        input_output_aliases=aliases,
        compiler_params=_params(1),
        name="inproj_sample",
    )(*args)


def _alibi_slopes():
    h = np.arange(1, NG * NH + 1, dtype=np.float32)
    return np.power(np.float32(2.0), -8.0 * h / (NG * NH)).astype(np.float32).reshape(NG, NH)


def _head_masks():
    lane = lax.broadcasted_iota(jnp.int32, (1, GW), 1)
    return [(lane >= h * HD) & (lane < (h + 1) * HD) for h in range(NH)]


def _stack_heads(q, masks):
    return jnp.concatenate([jnp.where(m, q, jnp.zeros_like(q)) for m in masks], axis=0)


def _unstack_heads(x, masks, rows):
    out = jnp.zeros((rows, GW), F32)
    for h, m in enumerate(masks):
        out = jnp.where(m, x[h * rows:(h + 1) * rows], out)
    return out


def _softmax_parts(parts, values, value_dots):
    m = functools.reduce(jnp.maximum, [jnp.max(s, axis=-1, keepdims=True) for s in parts])
    ps = [jnp.exp(s - m) for s in parts]
    l = functools.reduce(jnp.add, [jnp.sum(p, axis=-1, keepdims=True) for p in ps])
    o = functools.reduce(jnp.add, [dot(p.astype(BF16), v) for p, v, dot in zip(ps, values, value_dots)])
    return o / l, m + jnp.log(l)


def _merge_groups(outs, lses):
    m = jnp.maximum(jnp.maximum(lses[0], lses[1]), lses[2])
    es = [jnp.exp(l - m) for l in lses]
    tot = es[0] + es[1] + es[2]
    return (es[0] / tot) * outs[0] + (es[1] / tot) * outs[1] + (es[2] / tot) * outs[2]


def _prompt_bias_table():
    slopes = _alibi_slopes()
    a = np.arange(KB)[:, None]
    c = np.arange(2 * KB)[None, :]
    dist = KB + a - c
    valid = (dist >= 0) & (dist <= KB)
    tab = np.empty((NG, NH * KB, 2 * KB), np.float32)
    for g in range(NG):
        for h in range(NH):
            bias = -slopes[g, h] * (DILS[g] * dist).astype(np.float32)
            tab[g, h * KB:(h + 1) * KB] = np.where(valid, bias, np.float32(NEG))
    return tab


def _sample_bias_tables():
    slopes = _alibi_slopes()
    j = np.arange(SPAD)[:, None]
    tabs = []
    for g in range(NG):
        w, d = WINDOWS[g], DILS[g]
        c = np.arange(w)[None, :]
        cn = np.arange(KB)[None, :]
        dist = np.concatenate([np.broadcast_to(w + j - c, (SPAD, w)),
                               np.broadcast_to(j - cn, (SPAD, KB))], axis=1)
        real = np.concatenate([np.ones((SPAD, w), bool), np.broadcast_to(cn < T_S, (SPAD, KB))], axis=1)
        valid = (j < T_S) & real & (dist >= 0) & (dist <= w) & (dist % d == 0)
        pad_row = np.broadcast_to(j >= T_S, dist.shape)
        tab = np.empty((NH * SPAD, dist.shape[1]), np.float32)
        for h in range(NH):
            bias = -slopes[g, h] * dist.astype(np.float32)
            blk = np.where(valid, bias, np.float32(NEG))
            tab[h * SPAD:(h + 1) * SPAD] = np.where(pad_row, np.float32(0.0), blk)
        tabs.append(tab)
    return tabs


def _attn_sample_body(z_ref, c1_ref, c2_ref, c3_ref, b1_ref, b2_ref, b3_ref, att_ref):
    for s in range(z_ref.shape[0]):
        _attn_sample_one(z_ref.at[s], (c1_ref.at[s], c2_ref.at[s], c3_ref.at[s]),
                         (b1_ref, b2_ref, b3_ref), att_ref.at[s])


def _attn_sample_one(z_ref, c_refs, b_refs, att_ref):
    masks = _head_masks()
    z = z_ref[...]
    pad = jnp.zeros((KB - SPAD, GW), F32)
    outs, lses = [], []
    for g, (c_ref, b_ref) in enumerate(zip(c_refs, b_refs)):
        w = WINDOWS[g]
        q = z[:, g * GW:(g + 1) * GW]
        k_new = jnp.concatenate([z[:, ATT + g * GW:ATT + (g + 1) * GW], pad], axis=0).astype(BF16)
        v_new = jnp.concatenate([z[:, 2 * ATT + g * GW:2 * ATT + (g + 1) * GW], pad], axis=0).astype(BF16)
        k_old = c_ref[:GW, :].astype(BF16)
        v_old = c_ref[GW:, :].astype(BF16)
        qs = _stack_heads(q, masks).astype(BF16)
        parts = [_dot(qs, k_old) + b_ref[:, :w], _dot_nt(qs, k_new) + b_ref[:, w:]]
        o, lse = _softmax_parts(parts, [v_old, v_new], [_dot_nt, _dot])
        outs.append(_unstack_heads(o, masks, SPAD))
        lses.append(_unstack_heads(lse, masks, SPAD))
    att_ref[...] = _merge_groups(outs, lses)


def _attn_sample_operands(z, caches, tabs, layer, n_steps):
    per_step = B_S // n_steps
    assert per_step * n_steps == B_S
    zp = jnp.pad(z.reshape(B_S, T_S, 3 * ATT), ((0, 0), (0, SPAD - T_S), (0, 0)))
    cs = [c.transpose(0, 1, 3, 4, 5, 2).reshape(DEPTH, B_S, 2 * GW, w) for c, w in zip(caches, WINDOWS)]
    in_specs = [pl.BlockSpec((per_step, SPAD, 3 * ATT), lambda i: (i, 0, 0))]
    in_specs += [pl.BlockSpec((None, per_step, 2 * GW, w), lambda i: (layer, i, 0, 0)) for w in WINDOWS]
    in_specs += [_const(t.shape) for t in tabs]
    out_spec = pl.BlockSpec((per_step, SPAD, GW), lambda i: (i, 0, 0))
    return [zp, *cs, *tabs], in_specs, out_spec, jax.ShapeDtypeStruct((B_S, SPAD, GW), F32)


def _mix_out(x, h, att, spat, wgate_ref, pa_ref, ps_ref, wo_ref):
    ga = jax.nn.sigmoid(_dot(h, wgate_ref[:, :D]))
    gb = jax.nn.sigmoid(_dot(h, wgate_ref[:, D:]))
    mix = ga * _dot(att.astype(BF16), pa_ref[...]) + gb * _dot(spat, ps_ref[...])
    return x + _dot(mix.astype(BF16), wo_ref[...])


def _mixer_out_body(x_ref, h_ref, wgate_ref, q1_ref, kv1_ref, kv1p_ref, q2_ref, kv2_ref, kv2p_ref, q3_ref, kv3_ref,
                    bias_ref, u_ref, vs_ref, ws_ref, b_ref, pa_ref, ps_ref, wo_ref, out_ref,
                    o_scr, l_scr, spat_scr):
    j = pl.program_id(1)
    masks = _head_masks()
    lane = lax.broadcasted_iota(jnp.int32, (1, 2 * KB), 1)
    first_tile_mask = (j > 0) | (lane >= KB)

    def wide(ref, base, rows):
        return jnp.concatenate([ref[base + sl, rows, :] for sl in range(SLABS)], axis=1)

    def band(g, q, kv, bias, rows, n):
        qs = _stack_heads(q, masks)
        s = _dot_nt(qs, kv[:, :GW]) + bias
        m = jnp.max(s, axis=-1, keepdims=True)
        p = jnp.exp(s - m)
        l = jnp.sum(p, axis=-1, keepdims=True)
        o = _unstack_heads(_dot(p.astype(BF16), kv[:, GW:]) / l, masks, n)
        lse = _unstack_heads(m + jnp.log(l), masks, n)
        for sl in range(SLABS):
            o_scr[g * SLABS + sl, rows, :] = o[:, sl * LANES:(sl + 1) * LANES]
            l_scr[g * SLABS + sl, rows, :] = lse[:, sl * LANES:(sl + 1) * LANES]

    bias1 = bias_ref[0]
    for blk in range(TM // KB):
        own = pl.ds(blk * KB, KB)
        if blk == 0:
            kv = jnp.concatenate([kv1p_ref[...], kv1_ref[own, :]], axis=0)
            bias = jnp.where(first_tile_mask, bias1, NEG)
        else:
            kv, bias = kv1_ref[pl.ds((blk - 1) * KB, 2 * KB), :], bias1
        band(0, q1_ref[own, :], kv, bias, own, KB)

    bias2 = jnp.where(first_tile_mask, bias_ref[1], NEG)
    for r in range(DILS[1]):
        kv = jnp.concatenate([kv2p_ref[r], kv2_ref[r]], axis=0)
        band(1, q2_ref[r], kv, bias2, pl.ds(r, KB, stride=DILS[1]), KB)

    nq = TM // DILS[2]
    q0 = pl.multiple_of(j * nq, nq)
    bias3 = jnp.concatenate([bias_ref[2, pl.ds(h * KB + q0, nq), KB:] for h in range(NH)], axis=0)
    for r in range(DILS[2]):
        band(2, q3_ref[r], kv3_ref[r], bias3, pl.ds(r, nq, stride=DILS[2]), nq)

    full = slice(None)
    att = _merge_groups([wide(o_scr, g * SLABS, full) for g in range(NG)],
                        [wide(l_scr, g * SLABS, full) for g in range(NG)])

    x = x_ref[...]
    h = h_ref[...]
    row = lax.broadcasted_iota(jnp.int32, (CHUNK, CHUNK), 0)
    col = lax.broadcasted_iota(jnp.int32, (CHUNK, CHUNK), 1)
    for g in range(GG):
        w = jnp.where(col <= row, ws_ref[g], 0.0).astype(BF16)
        gs = slice(g * GC, (g + 1) * GC)
        for c in range(TM // CHUNK):
            rs = slice(c * CHUNK, (c + 1) * CHUNK)
            zc = _dot(w, vs_ref[rs, gs]) + b_ref[g]
            spat_scr[rs, gs] = (u_ref[rs, gs] * zc).astype(BF16)
    out_ref[...] = _mix_out(x, h, att, spat_scr[...], wgate_ref, pa_ref, ps_ref, wo_ref)


def _mixer_out(x, h, wgate, attn_ops, bias_tab, u, vs, ws, bias, pa, ps, wo, layer):
    tiles = S_P // TM
    assert TM == KB * DILS[1] and S_P == KB * DILS[2]
    q1, kv1, q2, kv2, q3, kv3 = attn_ops
    row = lambda b, j: (b * tiles + j, 0)
    res = lambda b, j: (b, 0, j, 0)
    res_before = lambda b, j: (b, 0, jnp.maximum(j - 1, 0), 0)
    row_before = lambda b, j: (jnp.maximum((b * tiles + j) * (TM // KB) - 1, 0), 0)
    d2, d3 = DILS[1], DILS[2]
    in_specs = [pl.BlockSpec((TM, D), row), pl.BlockSpec((TM, D), row), _const((D, 2 * D)),
                pl.BlockSpec((TM, GW), row), pl.BlockSpec((TM, 2 * GW), row),
                pl.BlockSpec((KB, 2 * GW), row_before),
                pl.BlockSpec((None, d2, TM // d2, GW), res), pl.BlockSpec((None, d2, TM // d2, 2 * GW), res),
                pl.BlockSpec((None, d2, TM // d2, 2 * GW), res_before),
                pl.BlockSpec((None, d3, TM // d3, GW), res),
                pl.BlockSpec((None, d3, S_P // d3, 2 * GW), lambda b, j: (b, 0, 0, 0)),
                _const(bias_tab.shape),
                pl.BlockSpec((TM, GMW), row), pl.BlockSpec((TM, GMW), row),
                _resident((GG, CHUNK, CHUNK), layer), _resident((GG, CHUNK, 1), layer),
                _const((GW, D)), _const((GMW, D)), _const((D, D))]
    return pl.pallas_call(
        _mixer_out_body,
        out_shape=jax.ShapeDtypeStruct((N_P, D), F32),
        grid=(B_P, tiles),
        in_specs=in_specs,
        out_specs=pl.BlockSpec((TM, D), row),
        scratch_shapes=[pltpu.VMEM((NG * SLABS, TM, LANES), F32), pltpu.VMEM((NG * SLABS, TM, LANES), F32),
                        pltpu.VMEM((TM, GMW), BF16)],
        compiler_params=_params(2),
        name="mixer_out",
    )(x, h, wgate, q1, kv1, kv1, q2, kv2, kv2, q3, kv3, bias_tab, u, vs, ws, bias, pa, ps, wo)


def _outproj_sample_body(x_ref, g_ref, wgate_ref, att_ref, u_ref, vs_ref, wt_ref, b_ref,
                         pa_ref, ps_ref, wo_ref, out_ref):
    x = x_ref[...]
    h = _rms(x, g_ref[...]).astype(BF16)
    row = lax.broadcasted_iota(jnp.int32, (N_S, N_S), 0)
    col = lax.broadcasted_iota(jnp.int32, (N_S, N_S), 1)
    keep = (row // T_S == col // T_S) & (col <= row)
    u = u_ref[...]
    vs = vs_ref[...].astype(BF16)
    parts = []
    for g in range(GG):
        corner = wt_ref[g]
        w = jnp.zeros((N_S, N_S), F32)
        for t in range(T_S):
            w = jnp.where(keep & (lax.rem(row, T_S) == t), corner[t:t + 1, :], w)
        w = w.astype(BF16)
        gs = slice(g * GC, (g + 1) * GC)
        parts.append(u[:, gs] * (_dot(w, vs[:, gs]) + b_ref[g]))
    spat = jnp.concatenate(parts, axis=1).astype(BF16)
    out_ref[...] = _mix_out(x, h, att_ref[...], spat, wgate_ref, pa_ref, ps_ref, wo_ref)


def _outproj_sample(x, gain, wgate, att, u, vs_all, wt, bias, pa, ps, wo, layer):
    zero = lambda i: (0, 0)
    in_specs = [pl.BlockSpec((N_S, D), zero), _resident((1, D), layer), _const((D, 2 * D)),
                pl.BlockSpec((N_S, GW), zero), pl.BlockSpec((N_S, GMW), zero),
                pl.BlockSpec((N_S, GMW), lambda i: (layer, 0)),
                _resident((GG, T_S, N_S), layer), _resident((GG, N_S, 1), layer),
                _const((GW, D)), _const((GMW, D)), _const((D, D))]
    return pl.pallas_call(
        _outproj_sample_body,
        out_shape=jax.ShapeDtypeStruct((N_S, D), F32),
        grid=(1,),
        in_specs=in_specs,
        out_specs=pl.BlockSpec((N_S, D), zero),
        compiler_params=_params(1),
        name="outproj_sample",
    )(x, gain, wgate, att, u, vs_all, wt, bias, pa, ps, wo)


def kernel(x_prompt, x_sample, cache_kv_w128, cache_kv_w512, cache_kv_w2048, ffn1_norm, ffn1_gate, ffn1_up, ffn1_down, mix_norm, w_in, gmlp_v_norm, gmlp_ws, gmlp_bias, proj_att, proj_spatial, w_out, ffn2_norm, ffn2_gate, ffn2_up, ffn2_down, final_norm):
    caches = (cache_kv_w128, cache_kv_w512, cache_kv_w2048)
    xp = x_prompt.reshape(N_P, D)
    xs = x_sample.reshape(N_S, D)

    ffn1_jobs = lambda l: [_CastJob(ffn1_gate, l), _CastJob(ffn1_up, l), _CastJob(ffn1_down, l)]
    ffn2_jobs = lambda l: [_CastJob(ffn2_gate, l), _CastJob(ffn2_up, l), _CastJob(ffn2_down, l)]
    mix_jobs = lambda l: [_CastJob(w_in, l, [(0, QKVUV), (QKVUV, QKVUV + 2 * D)]),
                          _CastJob(proj_att, l), _CastJob(proj_spatial, l), _CastJob(w_out, l)]

    n1 = ffn1_norm.reshape(DEPTH, 1, D)
    n2 = ffn2_norm.reshape(DEPTH, 1, D)
    nm = mix_norm.reshape(DEPTH, 1, D)
    nv = gmlp_v_norm.reshape(DEPTH, 1, GMW)
    nf = final_norm.reshape(1, D)
    bias_p = gmlp_bias.reshape(DEPTH, GG, CHUNK, 1)
    wt_s = jnp.tile(gmlp_ws[:, :, :T_S, :T_S], (1, 1, 1, B_S))
    bias_s = jnp.tile(gmlp_bias[:, :, :T_S], (1, 1, B_S)).reshape(DEPTH, GG, N_S, 1)

    attn_bias = jnp.asarray(_prompt_bias_table())
    sample_tabs = [jnp.asarray(t) for t in _sample_bias_tables()]

    f1 = {0: _cast(ffn1_jobs(0))}
    f2, mix = {}, {}
    kv_p = kv_s = None
    for l in range(DEPTH):
        last = l == DEPTH - 1
        fin = nf if last else None
        xp, hp, outs, _ = _ffn(xp, n1, l, f1[l], next_gain=nm,
                               jobs=ffn2_jobs(l) + (mix_jobs(l) if l == 0 else []))
        f2[l] = outs[:3]
        if l == 0:
            mix[l] = outs[3:]
        xs, _, _, _ = _ffn(xs, n1, l, f1[l])
        z_s, ks1, ks2, ks3, u_s, vs_s = _inproj_sample(xs, nm, mix[l][0], nv, l, kv_s)
        kv_s = (ks1, ks2, ks3, vs_s)
        attn = functools.partial(_attn_sample_operands, z_s, caches, sample_tabs, l)
        kv_p, attn_ops, u, vs = _inproj_prompt(hp, mix[l][0], nv, l, kv_p)
        xp = _mixer_out(xp, hp, mix[l][1], attn_ops, attn_bias, u, vs, gmlp_ws, bias_p, *mix[l][2:], l)
        xp, _, outs, att = _ffn(xp, n2, l, f2[l], fin, attn=attn,
                                jobs=[] if last else ffn1_jobs(l + 1) + mix_jobs(l + 1))
        if not last:
            f1[l + 1], mix[l + 1] = outs[:3], outs[3:]
        att_s = att[:, :T_S].reshape(N_S, GW)
        xs = _outproj_sample(xs, nm, mix[l][1], att_s, u_s, vs_s, wt_s, bias_s, *mix[l][2:], l)
        xs, _, _, _ = _ffn(xs, n2, l, f2[l], fin)

    kv_out = [k.reshape(DEPTH, B_P, 2, NH, HD, w).transpose(0, 1, 5, 2, 3, 4) for k, w in zip(kv_p, WINDOWS)]
    ks_shape = (DEPTH, B_S, T_S, 2, NH, HD)
    return (xp.reshape(B_P, S_P, D), xs.reshape(B_S, T_S, D), kv_out[0], kv_out[1], kv_out[2],
            kv_s[0].reshape(ks_shape), kv_s[1].reshape(ks_shape), kv_s[2].reshape(ks_shape),
            kv_s[3].reshape(DEPTH, B_S, T_S, GMW))
```

```python
import functools

import numpy as np
import jax
import jax.numpy as jnp
from jax import lax
from jax.experimental import pallas as pl
from jax.experimental.pallas import tpu as pltpu

F32 = jnp.float32
BF16 = jnp.bfloat16

D = 1024
DEPTH = 2
B_P, S_P = 8, 2048
B_S, T_S = 32, 4
N_P = B_P * S_P
N_S = B_S * T_S
HD, NH, NG = 64, 4, 3
GW = NH * HD
ATT = NG * GW
WINDOWS = (128, 512, 2048)
DILS = (1, 4, 16)
KB = 128
CHUNK = 128
GG, GC = 4, 128
GMW = GG * GC
DFF = 2816
FC = 256
QKVUV = 3 * ATT + 2 * GMW
EPS = 1e-6
NEG = -1e30
TM = 512
TM_FFN = 1024
FFN_SUB = 256
SPAD = 16
LANES = 128
SUBLANES_BF16 = 16
SLABS = GW // LANES
N_SLAB = 3 * ATT // LANES
VMEM_LIMIT = 60 * 1024 * 1024

_NT = (((1,), (1,)), ((), ()))


def _rms(x, gain):
    r = lax.rsqrt(jnp.mean(x * x, axis=-1, keepdims=True) + EPS)
    return (x * r) * gain


def _dot(a, b):
    return jnp.dot(a, b, preferred_element_type=F32)


def _dot_nt(a, b):
    return lax.dot_general(a, b, _NT, preferred_element_type=F32)


def _params(n_axes):
    return pltpu.CompilerParams(dimension_semantics=("arbitrary",) * n_axes,
                                vmem_limit_bytes=VMEM_LIMIT)


def _resident(shape, layer):
    nd = len(shape)
    return pl.BlockSpec((None,) + shape, lambda *_: (layer,) + (0,) * nd,
                        pipeline_mode=pl.Buffered(1))


def _const(shape):
    nd = len(shape)
    return pl.BlockSpec(shape, lambda *_: (0,) * nd, pipeline_mode=pl.Buffered(1))


class _CastJob:
    def __init__(self, src, layer, col_ranges=None):
        self.src, self.layer = src, layer
        self.rows, self.cols = src.shape[1:]
        self.col_ranges = col_ranges or [(0, self.cols)]

    def specs(self, n_steps):
        rps = self.rows // n_steps
        if rps % SUBLANES_BF16:
            rps = self.rows // (n_steps // 2)
        assert rps % SUBLANES_BF16 == 0 and self.rows % rps == 0
        last = self.rows // rps - 1
        layer = self.layer
        in_spec = pl.BlockSpec((None, rps, self.cols), lambda i: (layer, jnp.minimum(i, last), 0))
        out_specs = [pl.BlockSpec((rps, c1 - c0), lambda i: (jnp.minimum(i, last), 0))
                     for c0, c1 in self.col_ranges]
        out_shapes = [jax.ShapeDtypeStruct((self.rows, c1 - c0), BF16) for c0, c1 in self.col_ranges]
        return in_spec, out_specs, out_shapes


def _run_casts(jobs, src_refs, out_refs):
    out_refs = list(out_refs)
    for job, src_ref in zip(jobs, src_refs):
        x = src_ref[...]
        for c0, c1 in job.col_ranges:
            out_refs.pop(0)[...] = x[:, c0:c1].astype(BF16)


def _cast_specs(jobs, n_steps):
    in_specs, out_specs, out_shapes = [], [], []
    for job in jobs:
        i, o, s = job.specs(n_steps)
        in_specs.append(i)
        out_specs += o
        out_shapes += s
    return in_specs, out_specs, out_shapes


def _cast_body(*refs, jobs):
    _run_casts(jobs, refs[:len(jobs)], refs[len(jobs):])


def _cast(jobs):
    n_steps = 16
    in_specs, out_specs, out_shapes = _cast_specs(jobs, n_steps)
    return pl.pallas_call(
        functools.partial(_cast_body, jobs=jobs),
        out_shape=out_shapes,
        grid=(n_steps,),
        in_specs=in_specs,
        out_specs=out_specs,
        compiler_params=_params(1),
        name="cast_bf16",
    )(*[j.src for j in jobs])


def _ffn_body(*refs, tail, jobs, with_attn):
    refs = list(refs)
    x_ref, g_ref, wg_ref, wu_ref, wd_ref = refs[:5]
    del refs[:5]
    tg_ref = refs.pop(0) if tail else None
    cast_srcs = [refs.pop(0) for _ in jobs]
    attn_in = [refs.pop(0) for _ in range(7)] if with_attn else None
    o_ref = refs.pop(0)
    h_ref = refs.pop(0) if tail == "next" else None
    cast_outs = [refs.pop(0) for _ in range(sum(len(j.col_ranges) for j in jobs))]
    att_ref = refs.pop(0) if with_attn else None
    a_scr, = refs

    _run_casts(jobs, cast_srcs, cast_outs)
    if with_attn:
        _attn_sample_body(*attn_in, att_ref)
    sub = min(FFN_SUB, x_ref.shape[0])
    for r0 in range(0, x_ref.shape[0], sub):
        rows = slice(r0, r0 + sub)
        x = x_ref[rows, :]
        h = _rms(x, g_ref[...]).astype(BF16)
        for c in range(DFF // FC):
            cs = slice(c * FC, (c + 1) * FC)
            gate = _dot(h, wg_ref[:, cs])
            up = _dot(h, wu_ref[:, cs])
            a_scr[rows, cs] = (jax.nn.silu(gate) * up).astype(BF16)
        y = x + 0.5 * _dot(a_scr[rows, :], wd_ref[...])
        if tail == "final":
            y = _rms(y, tg_ref[...])
        o_ref[rows, :] = y
        if tail == "next":
            h_ref[rows, :] = _rms(y, tg_ref[...]).astype(BF16)


def _ffn(x, gain, layer, weights, final_gain=None, next_gain=None, jobs=(), attn=None):
    n = x.shape[0]
    tm = min(TM if attn is not None else TM_FFN, n)
    n_steps = n // tm
    assert final_gain is None or next_gain is None
    tail = "final" if final_gain is not None else "next" if next_gain is not None else None
    jobs = list(jobs)
    row = pl.BlockSpec((tm, D), lambda i: (i, 0))
    in_specs = [row, _resident((1, D), layer), _const((D, DFF)), _const((D, DFF)), _const((DFF, D))]
    args = [x, gain, *weights]
    out_specs, out_shape = [row], [jax.ShapeDtypeStruct((n, D), F32)]
    if tail == "final":
        in_specs.append(_const((1, D)))
        args.append(final_gain)
    elif tail == "next":
        in_specs.append(_resident((1, D), layer))
        args.append(next_gain)
        out_specs.append(row)
        out_shape.append(jax.ShapeDtypeStruct((n, D), BF16))
    cast_in, cast_out, cast_shapes = _cast_specs(jobs, n_steps)
    in_specs += cast_in
    args += [j.src for j in jobs]
    out_specs += cast_out
    out_shape += cast_shapes
    n_head = len(out_shape) - len(cast_shapes)
    if attn is not None:
        attn_args, attn_in_specs, attn_out_spec, attn_out_shape = attn(n_steps)
        in_specs += attn_in_specs
        args += attn_args
        out_specs.append(attn_out_spec)
        out_shape.append(attn_out_shape)
    outs = pl.pallas_call(
        functools.partial(_ffn_body, tail=tail, jobs=jobs, with_attn=attn is not None),
        out_shape=out_shape,
        grid=(n_steps,),
        in_specs=in_specs,
        out_specs=out_specs,
        scratch_shapes=[pltpu.VMEM((tm, DFF), BF16)],
        compiler_params=_params(1),
        name="ffn_" + tail if tail else "ffn",
    )(*args)
    casts = list(outs[n_head:n_head + len(cast_shapes)])
    return outs[0], (outs[1] if tail == "next" else None), casts, (outs[-1] if attn is not None else None)


def _project(h, w_ref, sec, g):
    c0 = sec * ATT + g * GW
    z = _dot(h, w_ref[:, c0:c0 + GW])
    if sec == 0:
        z = z * (HD ** -0.5)
    return z


def _gmlp_inputs(h, w_ref, vn_ref, u_ref, vs_ref):
    u_ref[...] = jax.nn.gelu(_dot(h, w_ref[:, 3 * ATT:3 * ATT + GMW]))
    gv = jax.nn.gelu(_dot(h, w_ref[:, 3 * ATT + GMW:QKVUV]))
    vs_ref[...] = _rms(gv, vn_ref[...]).astype(vs_ref.dtype)


def _inproj_prompt_body(*refs, n_alias):
    refs = list(refs)
    h_ref, w_ref, vn_ref = refs[:3]
    del refs[:3 + n_alias]
    kv1_ref, kv2_ref, kv3_ref = refs[:3]
    q_refs, kvb_refs = refs[3:9:2], refs[4:9:2]
    u_ref, vs_ref, z_scr, t_scr = refs[9:]
    h = h_ref[...]
    _gmlp_inputs(h, w_ref, vn_ref, u_ref, vs_ref)
    for sec in range(3):
        for g in range(NG):
            z = _project(h, w_ref, sec, g)
            dst = q_refs[g] if sec == 0 else kvb_refs[g]
            cols = slice(GW, 2 * GW) if sec == 2 else slice(0, GW)
            if g == 0:
                dst[:, cols] = z.astype(BF16)
            else:
                d = DILS[g]
                base = (sec * (NG - 1) + g - 1) * SLABS
                for sl in range(SLABS):
                    z_scr[base + sl] = z[:, sl * LANES:(sl + 1) * LANES]
                src, step = z_scr, d
                if d == DILS[1] ** 2:
                    step = DILS[1]
                    n4 = TM // step
                    for sl in range(SLABS):
                        for r4 in range(step):
                            t_scr[sec * SLABS + sl, r4 * n4:(r4 + 1) * n4, :] = z_scr[base + sl, pl.ds(r4, n4, stride=step), :]
                    src, base = t_scr, sec * SLABS
                for r in range(d):
                    start = r if step == d else (r % step) * (TM // step) + r // step
                    sub = [src[base + sl, pl.ds(start, TM // d, stride=step), :] for sl in range(SLABS)]
                    dst[r, :, cols] = jnp.concatenate(sub, axis=1).astype(BF16)
            if sec == 0:
                continue
            rows = slice((sec - 1) * GW, sec * GW)
            if g == 2:
                kv3_ref[rows, :] = z.T
            elif g == 1:
                kv2_ref[rows, :] = z.T
            else:
                kv1_ref[rows, :] = z[TM - WINDOWS[0]:, :].T


def _inproj_prompt(h, w, vnorm, layer, prev):
    tiles = S_P // TM
    n_alias = 0 if prev is None else 3
    in_specs = [pl.BlockSpec((TM, D), lambda i: (i, 0)), _const((D, QKVUV)), _resident((1, GMW), layer)]
    in_specs += [pl.BlockSpec(memory_space=pl.ANY)] * n_alias
    seq = lambda i: (layer * B_P + i // tiles, 0, 0)
    row = lambda i: (i, 0)
    res = lambda i: (i // tiles, 0, lax.rem(i, tiles), 0)
    out_shape = [jax.ShapeDtypeStruct((DEPTH * B_P, 2 * GW, w), F32) for w in WINDOWS]
    out_specs = [pl.BlockSpec((None, 2 * GW, WINDOWS[0]), seq),
                 pl.BlockSpec((None, 2 * GW, WINDOWS[1]), seq),
                 pl.BlockSpec((None, 2 * GW, TM), lambda i: (layer * B_P + i // tiles, 0, lax.rem(i, tiles)))]
    for g, d in enumerate(DILS):
        for width in (GW, 2 * GW):
            if g == 0:
                out_shape.append(jax.ShapeDtypeStruct((N_P, width), BF16))
                out_specs.append(pl.BlockSpec((TM, width), row))
            else:
                out_shape.append(jax.ShapeDtypeStruct((B_P, d, S_P // d, width), BF16))
                out_specs.append(pl.BlockSpec((None, d, TM // d, width), res))
    out_shape += [jax.ShapeDtypeStruct((N_P, GMW), F32), jax.ShapeDtypeStruct((N_P, GMW), BF16)]
    out_specs += [pl.BlockSpec((TM, GMW), row), pl.BlockSpec((TM, GMW), row)]
    args = [h, w, vnorm] + ([] if prev is None else list(prev))
    outs = pl.pallas_call(
        functools.partial(_inproj_prompt_body, n_alias=n_alias),
        out_shape=out_shape,
        grid=(N_P // TM,),
        in_specs=in_specs,
        out_specs=out_specs,
        scratch_shapes=[pltpu.VMEM((3 * (NG - 1) * SLABS, TM, LANES), F32),
                        pltpu.VMEM((3 * SLABS, TM, LANES), F32)],
        input_output_aliases={3 + k: k for k in range(n_alias)},
        compiler_params=_params(1),
        name="inproj_prompt",
    )(*args)
    return tuple(outs[:3]), list(outs[3:9]), outs[9], outs[10]


def _inproj_sample_body(*refs, n_alias):
    x_ref, g_ref, w_ref, vn_ref = refs[:4]
    qkv_ref, kv1_ref, kv2_ref, kv3_ref, u_ref, vs_ref = refs[4 + n_alias:]
    kv_refs = (kv1_ref, kv2_ref, kv3_ref)
    h = _rms(x_ref[...], g_ref[...]).astype(BF16)
    for sec in range(3):
        for g in range(NG):
            z = _project(h, w_ref, sec, g)
            qkv_ref[:, sec * ATT + g * GW:sec * ATT + (g + 1) * GW] = z
            if sec > 0:
                kv_refs[g][:, (sec - 1) * GW:sec * GW] = z
    _gmlp_inputs(h, w_ref, vn_ref, u_ref, vs_ref)


def _inproj_sample(x, gain, w, vnorm, layer, prev):
    n_alias = 0 if prev is None else 4
    in_specs = [pl.BlockSpec((N_S, D), lambda i: (0, 0)),
                _resident((1, D), layer), _const((D, QKVUV)), _resident((1, GMW), layer)]
    in_specs += [pl.BlockSpec(memory_space=pl.ANY)] * n_alias
    stacked = jax.ShapeDtypeStruct((DEPTH * N_S, 2 * GW), F32)
    out_shape = [jax.ShapeDtypeStruct((N_S, 3 * ATT), F32), stacked, stacked, stacked,
                 jax.ShapeDtypeStruct((N_S, GMW), F32),
                 jax.ShapeDtypeStruct((DEPTH * N_S, GMW), F32)]
    lay = lambda i: (layer, 0)
    out_specs = [pl.BlockSpec((N_S, 3 * ATT), lambda i: (0, 0)),
                 pl.BlockSpec((N_S, 2 * GW), lay), pl.BlockSpec((N_S, 2 * GW), lay),
                 pl.BlockSpec((N_S, 2 * GW), lay),
                 pl.BlockSpec((N_S, GMW), lambda i: (0, 0)),
                 pl.BlockSpec((N_S, GMW), lay)]
    args = [x, gain, w, vnorm] + ([] if prev is None else list(prev))
    aliases = {4: 1, 5: 2, 6: 3, 7: 5} if prev is not None else {}
    return pl.pallas_call(
        functools.partial(_inproj_sample_body, n_alias=n_alias),
        out_shape=out_shape,
        grid=(1,),
        in_specs=in_specs,
        out_specs=out_specs,
        input_output_aliases=aliases,
        compiler_params=_params(1),
        name="inproj_sample",
    )(*args)


def _alibi_slopes():
    h = np.arange(1, NG * NH + 1, dtype=np.float32)
    return np.power(np.float32(2.0), -8.0 * h / (NG * NH)).astype(np.float32).reshape(NG, NH)


def _head_masks():
    lane = lax.broadcasted_iota(jnp.int32, (1, GW), 1)
    return [(lane >= h * HD) & (lane < (h + 1) * HD) for h in range(NH)]


def _stack_heads(q, masks):
    return jnp.concatenate([jnp.where(m, q, jnp.zeros_like(q)) for m in masks], axis=0)


def _unstack_heads(x, masks, rows):
    out = jnp.zeros((rows, GW), F32)
    for h, m in enumerate(masks):
        out = jnp.where(m, x[h * rows:(h + 1) * rows], out)
    return out


def _softmax_parts(parts, values, value_dots):
    m = functools.reduce(jnp.maximum, [jnp.max(s, axis=-1, keepdims=True) for s in parts])
    ps = [jnp.exp(s - m) for s in parts]
    l = functools.reduce(jnp.add, [jnp.sum(p, axis=-1, keepdims=True) for p in ps])
    o = functools.reduce(jnp.add, [dot(p.astype(BF16), v) for p, v, dot in zip(ps, values, value_dots)])
    return o / l, m + jnp.log(l)


def _merge_groups(outs, lses):
    m = jnp.maximum(jnp.maximum(lses[0], lses[1]), lses[2])
    es = [jnp.exp(l - m) for l in lses]
    tot = es[0] + es[1] + es[2]
    return (es[0] / tot) * outs[0] + (es[1] / tot) * outs[1] + (es[2] / tot) * outs[2]


def _prompt_bias_table():
    slopes = _alibi_slopes()
    a = np.arange(KB)[:, None]
    c = np.arange(2 * KB)[None, :]
    dist = KB + a - c
    valid = (dist >= 0) & (dist <= KB)
    tab = np.empty((NG, NH * KB, 2 * KB), np.float32)
    for g in range(NG):
        for h in range(NH):
            bias = -slopes[g, h] * (DILS[g] * dist).astype(np.float32)
            tab[g, h * KB:(h + 1) * KB] = np.where(valid, bias, np.float32(NEG))
    return tab


def _sample_bias_tables():
    slopes = _alibi_slopes()
    j = np.arange(SPAD)[:, None]
    tabs = []
    for g in range(NG):
        w, d = WINDOWS[g], DILS[g]
        c = np.arange(w)[None, :]
        cn = np.arange(KB)[None, :]
        dist = np.concatenate([np.broadcast_to(w + j - c, (SPAD, w)),
                               np.broadcast_to(j - cn, (SPAD, KB))], axis=1)
        real = np.concatenate([np.ones((SPAD, w), bool), np.broadcast_to(cn < T_S, (SPAD, KB))], axis=1)
        valid = (j < T_S) & real & (dist >= 0) & (dist <= w) & (dist % d == 0)
        pad_row = np.broadcast_to(j >= T_S, dist.shape)
        tab = np.empty((NH * SPAD, dist.shape[1]), np.float32)
        for h in range(NH):
            bias = -slopes[g, h] * dist.astype(np.float32)
            blk = np.where(valid, bias, np.float32(NEG))
            tab[h * SPAD:(h + 1) * SPAD] = np.where(pad_row, np.float32(0.0), blk)
        tabs.append(tab)
    return tabs


def _attn_sample_body(z_ref, c1_ref, c2_ref, c3_ref, b1_ref, b2_ref, b3_ref, att_ref):
    for s in range(z_ref.shape[0]):
        _attn_sample_one(z_ref.at[s], (c1_ref.at[s], c2_ref.at[s], c3_ref.at[s]),
                         (b1_ref, b2_ref, b3_ref), att_ref.at[s])


def _attn_sample_one(z_ref, c_refs, b_refs, att_ref):
    masks = _head_masks()
    z = z_ref[...]
    pad = jnp.zeros((KB - SPAD, GW), F32)
    outs, lses = [], []
    for g, (c_ref, b_ref) in enumerate(zip(c_refs, b_refs)):
        w = WINDOWS[g]
        q = z[:, g * GW:(g + 1) * GW]
        k_new = jnp.concatenate([z[:, ATT + g * GW:ATT + (g + 1) * GW], pad], axis=0).astype(BF16)
        v_new = jnp.concatenate([z[:, 2 * ATT + g * GW:2 * ATT + (g + 1) * GW], pad], axis=0).astype(BF16)
        k_old = c_ref[:GW, :].astype(BF16)
        v_old = c_ref[GW:, :].astype(BF16)
        qs = _stack_heads(q, masks).astype(BF16)
        parts = [_dot(qs, k_old) + b_ref[:, :w], _dot_nt(qs, k_new) + b_ref[:, w:]]
        o, lse = _softmax_parts(parts, [v_old, v_new], [_dot_nt, _dot])
        outs.append(_unstack_heads(o, masks, SPAD))
        lses.append(_unstack_heads(lse, masks, SPAD))
    att_ref[...] = _merge_groups(outs, lses)


def _attn_sample_operands(z, caches, tabs, layer, n_steps, step=lambda i: i):
    per_step = B_S // n_steps
    assert per_step * n_steps == B_S
    zp = jnp.pad(z.reshape(B_S, T_S, 3 * ATT), ((0, 0), (0, SPAD - T_S), (0, 0)))
    cs = [c.transpose(0, 1, 3, 4, 5, 2).reshape(DEPTH, B_S, 2 * GW, w) for c, w in zip(caches, WINDOWS)]
    in_specs = [pl.BlockSpec((per_step, SPAD, 3 * ATT), lambda *ids: (step(*ids), 0, 0))]
    in_specs += [pl.BlockSpec((None, per_step, 2 * GW, w), lambda *ids: (layer, step(*ids), 0, 0)) for w in WINDOWS]
    in_specs += [_const(t.shape) for t in tabs]
    out_spec = pl.BlockSpec((per_step, SPAD, GW), lambda *ids: (step(*ids), 0, 0))
    return [zp, *cs, *tabs], in_specs, out_spec, jax.ShapeDtypeStruct((B_S, SPAD, GW), F32)


def _mix_out(x, h, att, spat, wgate_ref, pa_ref, ps_ref, wo_ref):
    ga = jax.nn.sigmoid(_dot(h, wgate_ref[:, :D]))
    gb = jax.nn.sigmoid(_dot(h, wgate_ref[:, D:]))
    mix = ga * _dot(att.astype(BF16), pa_ref[...]) + gb * _dot(spat, ps_ref[...])
    return x + _dot(mix.astype(BF16), wo_ref[...])


def _mixer_out_body(x_ref, h_ref, wgate_ref, q1_ref, kv1_ref, kv1p_ref, q2_ref, kv2_ref, kv2p_ref, q3_ref, kv3_ref,
                    bias_ref, u_ref, vs_ref, ws_ref, b_ref, pa_ref, ps_ref, wo_ref, *rest):
    if len(rest) > 4:
        _attn_sample_body(*rest[:7], rest[8])
        out_ref, o_scr, l_scr, spat_scr = rest[7], *rest[9:]
    else:
        out_ref, o_scr, l_scr, spat_scr = rest
    j = pl.program_id(1)
    masks = _head_masks()
    lane = lax.broadcasted_iota(jnp.int32, (1, 2 * KB), 1)
    first_tile_mask = (j > 0) | (lane >= KB)

    def wide(ref, base, rows):
        return jnp.concatenate([ref[base + sl, rows, :] for sl in range(SLABS)], axis=1)

    def band(g, q, kv, bias, rows, n):
        qs = _stack_heads(q, masks)
        s = _dot_nt(qs, kv[:, :GW]) + bias
        m = jnp.max(s, axis=-1, keepdims=True)
        p = jnp.exp(s - m)
        l = jnp.sum(p, axis=-1, keepdims=True)
        o = _unstack_heads(_dot(p.astype(BF16), kv[:, GW:]) / l, masks, n)
        lse = _unstack_heads(m + jnp.log(l), masks, n)
        for sl in range(SLABS):
            o_scr[g * SLABS + sl, rows, :] = o[:, sl * LANES:(sl + 1) * LANES]
            l_scr[g * SLABS + sl, rows, :] = lse[:, sl * LANES:(sl + 1) * LANES]

    bias1 = bias_ref[0]
    for blk in range(TM // KB):
        own = pl.ds(blk * KB, KB)
        if blk == 0:
            kv = jnp.concatenate([kv1p_ref[...], kv1_ref[own, :]], axis=0)
            bias = jnp.where(first_tile_mask, bias1, NEG)
        else:
            kv, bias = kv1_ref[pl.ds((blk - 1) * KB, 2 * KB), :], bias1
        band(0, q1_ref[own, :], kv, bias, own, KB)

    bias2 = jnp.where(first_tile_mask, bias_ref[1], NEG)
    for r in range(DILS[1]):
        kv = jnp.concatenate([kv2p_ref[r], kv2_ref[r]], axis=0)
        band(1, q2_ref[r], kv, bias2, pl.ds(r, KB, stride=DILS[1]), KB)

    nq = TM // DILS[2]
    q0 = pl.multiple_of(j * nq, nq)
    bias3 = jnp.concatenate([bias_ref[2, pl.ds(h * KB + q0, nq), KB:] for h in range(NH)], axis=0)
    for r in range(DILS[2]):
        band(2, q3_ref[r], kv3_ref[r], bias3, pl.ds(r, nq, stride=DILS[2]), nq)

    full = slice(None)
    att = _merge_groups([wide(o_scr, g * SLABS, full) for g in range(NG)],
                        [wide(l_scr, g * SLABS, full) for g in range(NG)])

    x = x_ref[...]
    h = h_ref[...]
    row = lax.broadcasted_iota(jnp.int32, (CHUNK, CHUNK), 0)
    col = lax.broadcasted_iota(jnp.int32, (CHUNK, CHUNK), 1)
    for g in range(GG):
        w = jnp.where(col <= row, ws_ref[g], 0.0).astype(BF16)
        gs = slice(g * GC, (g + 1) * GC)
        for c in range(TM // CHUNK):
            rs = slice(c * CHUNK, (c + 1) * CHUNK)
            zc = _dot(w, vs_ref[rs, gs]) + b_ref[g]
            spat_scr[rs, gs] = (u_ref[rs, gs] * zc).astype(BF16)
    out_ref[...] = _mix_out(x, h, att, spat_scr[...], wgate_ref, pa_ref, ps_ref, wo_ref)


def _mixer_out(x, h, wgate, attn_ops, bias_tab, u, vs, ws, bias, pa, ps, wo, layer, attn=None):
    tiles = S_P // TM
    assert TM == KB * DILS[1] and S_P == KB * DILS[2]
    q1, kv1, q2, kv2, q3, kv3 = attn_ops
    row = lambda b, j: (b * tiles + j, 0)
    res = lambda b, j: (b, 0, j, 0)
    res_before = lambda b, j: (b, 0, jnp.maximum(j - 1, 0), 0)
    row_before = lambda b, j: (jnp.maximum((b * tiles + j) * (TM // KB) - 1, 0), 0)
    d2, d3 = DILS[1], DILS[2]
    in_specs = [pl.BlockSpec((TM, D), row), pl.BlockSpec((TM, D), row), _const((D, 2 * D)),
                pl.BlockSpec((TM, GW), row), pl.BlockSpec((TM, 2 * GW), row),
                pl.BlockSpec((KB, 2 * GW), row_before),
                pl.BlockSpec((None, d2, TM // d2, GW), res), pl.BlockSpec((None, d2, TM // d2, 2 * GW), res),
                pl.BlockSpec((None, d2, TM // d2, 2 * GW), res_before),
                pl.BlockSpec((None, d3, TM // d3, GW), res),
                pl.BlockSpec((None, d3, S_P // d3, 2 * GW), lambda b, j: (b, 0, 0, 0)),
                _const(bias_tab.shape),
                pl.BlockSpec((TM, GMW), row), pl.BlockSpec((TM, GMW), row),
                _resident((GG, CHUNK, CHUNK), layer), _resident((GG, CHUNK, 1), layer),
                _const((GW, D)), _const((GMW, D)), _const((D, D))]
    args = [x, h, wgate, q1, kv1, kv1, q2, kv2, kv2, q3, kv3, bias_tab, u, vs, ws, bias, pa, ps, wo]
    out_specs, out_shape = [pl.BlockSpec((TM, D), row)], [jax.ShapeDtypeStruct((N_P, D), F32)]
    if attn is not None:
        attn_args, attn_in_specs, attn_out_spec, attn_out_shape = attn(B_P * tiles, lambda b, j: b * tiles + j)
        in_specs += attn_in_specs
        args += attn_args
        out_specs.append(attn_out_spec)
        out_shape.append(attn_out_shape)
    outs = pl.pallas_call(
        _mixer_out_body,
        out_shape=out_shape,
        grid=(B_P, tiles),
        in_specs=in_specs,
        out_specs=out_specs,
        scratch_shapes=[pltpu.VMEM((NG * SLABS, TM, LANES), F32), pltpu.VMEM((NG * SLABS, TM, LANES), F32),
                        pltpu.VMEM((TM, GMW), BF16)],
        compiler_params=_params(2),
        name="mixer_out",
    )(*args)
    return outs[0], (outs[1] if attn is not None else None)


def _outproj_sample_body(x_ref, g_ref, wgate_ref, att_ref, u_ref, vs_ref, wt_ref, b_ref,
                         pa_ref, ps_ref, wo_ref, out_ref):
    x = x_ref[...]
    h = _rms(x, g_ref[...]).astype(BF16)
    row = lax.broadcasted_iota(jnp.int32, (N_S, N_S), 0)
    col = lax.broadcasted_iota(jnp.int32, (N_S, N_S), 1)
    keep = (row // T_S == col // T_S) & (col <= row)
    u = u_ref[...]
    vs = vs_ref[...].astype(BF16)
    parts = []
    for g in range(GG):
        corner = wt_ref[g]
        w = jnp.zeros((N_S, N_S), F32)
        for t in range(T_S):
            w = jnp.where(keep & (lax.rem(row, T_S) == t), corner[t:t + 1, :], w)
        w = w.astype(BF16)
        gs = slice(g * GC, (g + 1) * GC)
        parts.append(u[:, gs] * (_dot(w, vs[:, gs]) + b_ref[g]))
    spat = jnp.concatenate(parts, axis=1).astype(BF16)
    out_ref[...] = _mix_out(x, h, att_ref[...], spat, wgate_ref, pa_ref, ps_ref, wo_ref)


def _outproj_sample(x, gain, wgate, att, u, vs_all, wt, bias, pa, ps, wo, layer):
    zero = lambda i: (0, 0)
    in_specs = [pl.BlockSpec((N_S, D), zero), _resident((1, D), layer), _const((D, 2 * D)),
                pl.BlockSpec((N_S, GW), zero), pl.BlockSpec((N_S, GMW), zero),
                pl.BlockSpec((N_S, GMW), lambda i: (layer, 0)),
                _resident((GG, T_S, N_S), layer), _resident((GG, N_S, 1), layer),
                _const((GW, D)), _const((GMW, D)), _const((D, D))]
    return pl.pallas_call(
        _outproj_sample_body,
        out_shape=jax.ShapeDtypeStruct((N_S, D), F32),
        grid=(1,),
        in_specs=in_specs,
        out_specs=pl.BlockSpec((N_S, D), zero),
        compiler_params=_params(1),
        name="outproj_sample",
    )(x, gain, wgate, att, u, vs_all, wt, bias, pa, ps, wo)


def kernel(x_prompt, x_sample, cache_kv_w128, cache_kv_w512, cache_kv_w2048, ffn1_norm, ffn1_gate, ffn1_up, ffn1_down, mix_norm, w_in, gmlp_v_norm, gmlp_ws, gmlp_bias, proj_att, proj_spatial, w_out, ffn2_norm, ffn2_gate, ffn2_up, ffn2_down, final_norm):
    caches = (cache_kv_w128, cache_kv_w512, cache_kv_w2048)
    xp = x_prompt.reshape(N_P, D)
    xs = x_sample.reshape(N_S, D)

    ffn1_jobs = lambda l: [_CastJob(ffn1_gate, l), _CastJob(ffn1_up, l), _CastJob(ffn1_down, l)]
    ffn2_jobs = lambda l: [_CastJob(ffn2_gate, l), _CastJob(ffn2_up, l), _CastJob(ffn2_down, l)]
    mix_jobs = lambda l: [_CastJob(w_in, l, [(0, QKVUV), (QKVUV, QKVUV + 2 * D)]),
                          _CastJob(proj_att, l), _CastJob(proj_spatial, l), _CastJob(w_out, l)]

    n1 = ffn1_norm.reshape(DEPTH, 1, D)
    n2 = ffn2_norm.reshape(DEPTH, 1, D)
    nm = mix_norm.reshape(DEPTH, 1, D)
    nv = gmlp_v_norm.reshape(DEPTH, 1, GMW)
    nf = final_norm.reshape(1, D)
    bias_p = gmlp_bias.reshape(DEPTH, GG, CHUNK, 1)
    wt_s = jnp.tile(gmlp_ws[:, :, :T_S, :T_S], (1, 1, 1, B_S))
    bias_s = jnp.tile(gmlp_bias[:, :, :T_S], (1, 1, B_S)).reshape(DEPTH, GG, N_S, 1)

    attn_bias = jnp.asarray(_prompt_bias_table())
    sample_tabs = [jnp.asarray(t) for t in _sample_bias_tables()]

    f1 = {0: _cast(ffn1_jobs(0))}
    f2, mix = {}, {}
    kv_p = kv_s = None
    for l in range(DEPTH):
        last = l == DEPTH - 1
        fin = nf if last else None
        xp, hp, outs, _ = _ffn(xp, n1, l, f1[l], next_gain=nm,
                               jobs=ffn2_jobs(l) + (mix_jobs(l) if l == 0 else []))
        f2[l] = outs[:3]
        if l == 0:
            mix[l] = outs[3:]
        xs, _, _, _ = _ffn(xs, n1, l, f1[l])
        z_s, ks1, ks2, ks3, u_s, vs_s = _inproj_sample(xs, nm, mix[l][0], nv, l, kv_s)
        kv_s = (ks1, ks2, ks3, vs_s)
        attn = functools.partial(_attn_sample_operands, z_s, caches, sample_tabs, l)
        kv_p, attn_ops, u, vs = _inproj_prompt(hp, mix[l][0], nv, l, kv_p)
        xp, att = _mixer_out(xp, hp, mix[l][1], attn_ops, attn_bias, u, vs, gmlp_ws, bias_p, *mix[l][2:], l,
                             attn=attn)
        xp, _, outs, _ = _ffn(xp, n2, l, f2[l], fin, jobs=[] if last else ffn1_jobs(l + 1) + mix_jobs(l + 1))
        if not last:
            f1[l + 1], mix[l + 1] = outs[:3], outs[3:]
        att_s = att[:, :T_S].reshape(N_S, GW)
        xs = _outproj_sample(xs, nm, mix[l][1], att_s, u_s, vs_s, wt_s, bias_s, *mix[l][2:], l)
        xs, _, _, _ = _ffn(xs, n2, l, f2[l], fin)

    kv_out = [k.reshape(DEPTH, B_P, 2, NH, HD, w).transpose(0, 1, 5, 2, 3, 4) for k, w in zip(kv_p, WINDOWS)]
    ks_shape = (DEPTH, B_S, T_S, 2, NH, HD)
    return (xp.reshape(B_P, S_P, D), xs.reshape(B_S, T_S, D), kv_out[0], kv_out[1], kv_out[2],
            kv_s[0].reshape(ks_shape), kv_s[1].reshape(ks_shape), kv_s[2].reshape(ks_shape),
            kv_s[3].reshape(DEPTH, B_S, T_S, GMW))
```

```python
import functools

import numpy as np
import jax
import jax.numpy as jnp
from jax import lax
from jax.experimental import pallas as pl
from jax.experimental.pallas import tpu as pltpu

F32 = jnp.float32
BF16 = jnp.bfloat16

D = 1024
DEPTH = 2
B_P, S_P = 8, 2048
B_S, T_S = 32, 4
N_P = B_P * S_P
N_S = B_S * T_S
HD, NH, NG = 64, 4, 3
GW = NH * HD
ATT = NG * GW
WINDOWS = (128, 512, 2048)
DILS = (1, 4, 16)
KB = 128
CHUNK = 128
GG, GC = 4, 128
GMW = GG * GC
DFF = 2816
FC = 256
QKVUV = 3 * ATT + 2 * GMW
EPS = 1e-6
NEG = -1e30
TM = 512
TM_FFN = 1024
FFN_SUB = 256
IN_SUB = 256
SPAD = 16
LANES = 128
SUBLANES_BF16 = 16
SLABS = GW // LANES
N_SLAB = 3 * ATT // LANES
VMEM_LIMIT = 60 * 1024 * 1024

_NT = (((1,), (1,)), ((), ()))


def _rms(x, gain):
    r = lax.rsqrt(jnp.mean(x * x, axis=-1, keepdims=True) + EPS)
    return (x * r) * gain


def _dot(a, b):
    return jnp.dot(a, b, preferred_element_type=F32)


def _dot_nt(a, b):
    return lax.dot_general(a, b, _NT, preferred_element_type=F32)


def _params(n_axes):
    return pltpu.CompilerParams(dimension_semantics=("arbitrary",) * n_axes,
                                vmem_limit_bytes=VMEM_LIMIT)


def _resident(shape, layer):
    nd = len(shape)
    return pl.BlockSpec((None,) + shape, lambda *_: (layer,) + (0,) * nd,
                        pipeline_mode=pl.Buffered(1))


def _const(shape):
    nd = len(shape)
    return pl.BlockSpec(shape, lambda *_: (0,) * nd, pipeline_mode=pl.Buffered(1))


class _CastJob:
    def __init__(self, src, layer, col_ranges=None):
        self.src, self.layer = src, layer
        self.rows, self.cols = src.shape[1:]
        self.col_ranges = col_ranges or [(0, self.cols)]

    def specs(self, n_steps):
        rps = self.rows // n_steps
        if rps % SUBLANES_BF16:
            rps = self.rows // (n_steps // 2)
        assert rps % SUBLANES_BF16 == 0 and self.rows % rps == 0
        last = self.rows // rps - 1
        layer = self.layer
        in_spec = pl.BlockSpec((None, rps, self.cols), lambda i: (layer, jnp.minimum(i, last), 0))
        out_specs = [pl.BlockSpec((rps, c1 - c0), lambda i: (jnp.minimum(i, last), 0))
                     for c0, c1 in self.col_ranges]
        out_shapes = [jax.ShapeDtypeStruct((self.rows, c1 - c0), BF16) for c0, c1 in self.col_ranges]
        return in_spec, out_specs, out_shapes


def _run_casts(jobs, src_refs, out_refs):
    out_refs = list(out_refs)
    for job, src_ref in zip(jobs, src_refs):
        x = src_ref[...]
        for c0, c1 in job.col_ranges:
            out_refs.pop(0)[...] = x[:, c0:c1].astype(BF16)


def _cast_specs(jobs, n_steps):
    in_specs, out_specs, out_shapes = [], [], []
    for job in jobs:
        i, o, s = job.specs(n_steps)
        in_specs.append(i)
        out_specs += o
        out_shapes += s
    return in_specs, out_specs, out_shapes


def _cast_body(*refs, jobs):
    _run_casts(jobs, refs[:len(jobs)], refs[len(jobs):])


def _cast(jobs):
    n_steps = 16
    in_specs, out_specs, out_shapes = _cast_specs(jobs, n_steps)
    return pl.pallas_call(
        functools.partial(_cast_body, jobs=jobs),
        out_shape=out_shapes,
        grid=(n_steps,),
        in_specs=in_specs,
        out_specs=out_specs,
        compiler_params=_params(1),
        name="cast_bf16",
    )(*[j.src for j in jobs])


def _ffn_body(*refs, tail, jobs, with_attn):
    refs = list(refs)
    x_ref, g_ref, wg_ref, wu_ref, wd_ref = refs[:5]
    del refs[:5]
    tg_ref = refs.pop(0) if tail else None
    cast_srcs = [refs.pop(0) for _ in jobs]
    attn_in = [refs.pop(0) for _ in range(7)] if with_attn else None
    o_ref = refs.pop(0)
    h_ref = refs.pop(0) if tail == "next" else None
    cast_outs = [refs.pop(0) for _ in range(sum(len(j.col_ranges) for j in jobs))]
    att_ref = refs.pop(0) if with_attn else None
    a_scr, = refs

    _run_casts(jobs, cast_srcs, cast_outs)
    if with_attn:
        _attn_sample_body(*attn_in, att_ref)
    sub = min(FFN_SUB, x_ref.shape[0])
    for r0 in range(0, x_ref.shape[0], sub):
        rows = slice(r0, r0 + sub)
        x = x_ref[rows, :]
        h = _rms(x, g_ref[...]).astype(BF16)
        for c in range(DFF // FC):
            cs = slice(c * FC, (c + 1) * FC)
            gate = _dot(h, wg_ref[:, cs])
            up = _dot(h, wu_ref[:, cs])
            a_scr[rows, cs] = (jax.nn.silu(gate) * up).astype(BF16)
        y = x + 0.5 * _dot(a_scr[rows, :], wd_ref[...])
        if tail == "final":
            y = _rms(y, tg_ref[...])
        o_ref[rows, :] = y
        if tail == "next":
            h_ref[rows, :] = _rms(y, tg_ref[...]).astype(BF16)


def _ffn(x, gain, layer, weights, final_gain=None, next_gain=None, jobs=(), attn=None):
    n = x.shape[0]
    tm = min(TM if attn is not None else TM_FFN, n)
    n_steps = n // tm
    assert final_gain is None or next_gain is None
    tail = "final" if final_gain is not None else "next" if next_gain is not None else None
    jobs = list(jobs)
    row = pl.BlockSpec((tm, D), lambda i: (i, 0))
    in_specs = [row, _resident((1, D), layer), _const((D, DFF)), _const((D, DFF)), _const((DFF, D))]
    args = [x, gain, *weights]
    out_specs, out_shape = [row], [jax.ShapeDtypeStruct((n, D), F32)]
    if tail == "final":
        in_specs.append(_const((1, D)))
        args.append(final_gain)
    elif tail == "next":
        in_specs.append(_resident((1, D), layer))
        args.append(next_gain)
        out_specs.append(row)
        out_shape.append(jax.ShapeDtypeStruct((n, D), BF16))
    cast_in, cast_out, cast_shapes = _cast_specs(jobs, n_steps)
    in_specs += cast_in
    args += [j.src for j in jobs]
    out_specs += cast_out
    out_shape += cast_shapes
    n_head = len(out_shape) - len(cast_shapes)
    if attn is not None:
        attn_args, attn_in_specs, attn_out_spec, attn_out_shape = attn(n_steps)
        in_specs += attn_in_specs
        args += attn_args
        out_specs.append(attn_out_spec)
        out_shape.append(attn_out_shape)
    outs = pl.pallas_call(
        functools.partial(_ffn_body, tail=tail, jobs=jobs, with_attn=attn is not None),
        out_shape=out_shape,
        grid=(n_steps,),
        in_specs=in_specs,
        out_specs=out_specs,
        scratch_shapes=[pltpu.VMEM((tm, DFF), BF16)],
        compiler_params=_params(1),
        name="ffn_" + tail if tail else "ffn",
    )(*args)
    casts = list(outs[n_head:n_head + len(cast_shapes)])
    return outs[0], (outs[1] if tail == "next" else None), casts, (outs[-1] if attn is not None else None)


def _project(h, w_ref, sec, g):
    c0 = sec * ATT + g * GW
    z = _dot(h, w_ref[:, c0:c0 + GW])
    if sec == 0:
        z = z * (HD ** -0.5)
    return z


def _gmlp_inputs(h, w_ref, vn_ref, u_ref, vs_ref):
    u_ref[...] = jax.nn.gelu(_dot(h, w_ref[:, 3 * ATT:3 * ATT + GMW]))
    gv = jax.nn.gelu(_dot(h, w_ref[:, 3 * ATT + GMW:QKVUV]))
    vs_ref[...] = _rms(gv, vn_ref[...]).astype(vs_ref.dtype)


def _inproj_prompt_body(*refs, n_alias):
    refs = list(refs)
    h_ref, w_ref, vn_ref = refs[:3]
    del refs[:3 + n_alias]
    kv1_ref, kv2_ref, kv3_ref = refs[:3]
    q_refs, kvb_refs = refs[3:9:2], refs[4:9:2]
    u_ref, vs_ref, z_scr, t_scr = refs[9:]
    for r0 in range(0, TM, IN_SUB):
        _inproj_rows(h_ref, w_ref, vn_ref, (kv1_ref, kv2_ref, kv3_ref), q_refs, kvb_refs,
                     u_ref, vs_ref, z_scr, t_scr, r0)


def _inproj_rows(h_ref, w_ref, vn_ref, kvt_refs, q_refs, kvb_refs, u_ref, vs_ref, z_scr, t_scr, r0):
    rows = slice(r0, r0 + IN_SUB)
    h = h_ref[rows, :]
    _gmlp_inputs(h, w_ref, vn_ref, u_ref.at[rows, :], vs_ref.at[rows, :])
    for sec in (2, 1, 0):
        for g in reversed(range(NG)):
            z = _project(h, w_ref, sec, g)
            dst = q_refs[g] if sec == 0 else kvb_refs[g]
            cols = slice(GW, 2 * GW) if sec == 2 else slice(0, GW)
            if g == 0:
                dst[rows, cols] = z.astype(BF16)
            else:
                d = DILS[g]
                n = IN_SUB // d
                base = (sec * (NG - 1) + g - 1) * SLABS
                for sl in range(SLABS):
                    z_scr[base + sl, rows, :] = z[:, sl * LANES:(sl + 1) * LANES]
                src, step, first = z_scr, d, lambda r: r0 + r
                if d == DILS[1] ** 2:
                    step = DILS[1]
                    n4 = IN_SUB // step
                    for sl in range(SLABS):
                        for r4 in range(step):
                            t_scr[sec * SLABS + sl, pl.ds(r0 + r4 * n4, n4), :] = (
                                z_scr[base + sl, pl.ds(r0 + r4, n4, stride=step), :])
                    src, base, first = t_scr, sec * SLABS, lambda r: r0 + (r % step) * n4 + r // step
                for r in range(d):
                    sub = [src[base + sl, pl.ds(first(r), n, stride=step), :] for sl in range(SLABS)]
                    dst[r, pl.ds(r0 // d, n), cols] = jnp.concatenate(sub, axis=1).astype(BF16)
            if sec == 0:
                continue
            ch = slice((sec - 1) * GW, sec * GW)
            if g > 0:
                kvt_refs[g][ch, rows] = z.T
            elif r0 + IN_SUB == TM:
                kvt_refs[g][ch, :] = z[IN_SUB - WINDOWS[0]:, :].T


def _inproj_prompt(h, w, vnorm, layer, prev):
    tiles = S_P // TM
    n_alias = 0 if prev is None else 3
    in_specs = [pl.BlockSpec((TM, D), lambda i: (i, 0)), _const((D, QKVUV)), _resident((1, GMW), layer)]
    in_specs += [pl.BlockSpec(memory_space=pl.ANY)] * n_alias
    seq = lambda i: (layer * B_P + i // tiles, 0, 0)
    row = lambda i: (i, 0)
    res = lambda i: (i // tiles, 0, lax.rem(i, tiles), 0)
    out_shape = [jax.ShapeDtypeStruct((DEPTH * B_P, 2 * GW, w), F32) for w in WINDOWS]
    out_specs = [pl.BlockSpec((None, 2 * GW, WINDOWS[0]), seq),
                 pl.BlockSpec((None, 2 * GW, WINDOWS[1]), seq),
                 pl.BlockSpec((None, 2 * GW, TM), lambda i: (layer * B_P + i // tiles, 0, lax.rem(i, tiles)))]
    for g, d in enumerate(DILS):
        for width in (GW, 2 * GW):
            if g == 0:
                out_shape.append(jax.ShapeDtypeStruct((N_P, width), BF16))
                out_specs.append(pl.BlockSpec((TM, width), row))
            else:
                out_shape.append(jax.ShapeDtypeStruct((B_P, d, S_P // d, width), BF16))
                out_specs.append(pl.BlockSpec((None, d, TM // d, width), res))
    out_shape += [jax.ShapeDtypeStruct((N_P, GMW), F32), jax.ShapeDtypeStruct((N_P, GMW), BF16)]
    out_specs += [pl.BlockSpec((TM, GMW), row), pl.BlockSpec((TM, GMW), row)]
    args = [h, w, vnorm] + ([] if prev is None else list(prev))
    outs = pl.pallas_call(
        functools.partial(_inproj_prompt_body, n_alias=n_alias),
        out_shape=out_shape,
        grid=(N_P // TM,),
        in_specs=in_specs,
        out_specs=out_specs,
        scratch_shapes=[pltpu.VMEM((3 * (NG - 1) * SLABS, TM, LANES), F32),
                        pltpu.VMEM((3 * SLABS, TM, LANES), F32)],
        input_output_aliases={3 + k: k for k in range(n_alias)},
        compiler_params=_params(1),
        name="inproj_prompt",
    )(*args)
    return tuple(outs[:3]), list(outs[3:9]), outs[9], outs[10]


def _inproj_sample_body(*refs, n_alias):
    x_ref, g_ref, w_ref, vn_ref = refs[:4]
    qkv_ref, kv1_ref, kv2_ref, kv3_ref, u_ref, vs_ref = refs[4 + n_alias:]
    kv_refs = (kv1_ref, kv2_ref, kv3_ref)
    h = _rms(x_ref[...], g_ref[...]).astype(BF16)
    for sec in range(3):
        for g in range(NG):
            z = _project(h, w_ref, sec, g)
            qkv_ref[:, sec * ATT + g * GW:sec * ATT + (g + 1) * GW] = z
            if sec > 0:
                kv_refs[g][:, (sec - 1) * GW:sec * GW] = z
    _gmlp_inputs(h, w_ref, vn_ref, u_ref, vs_ref)


def _inproj_sample(x, gain, w, vnorm, layer, prev):
    n_alias = 0 if prev is None else 4
    in_specs = [pl.BlockSpec((N_S, D), lambda i: (0, 0)),
                _resident((1, D), layer), _const((D, QKVUV)), _resident((1, GMW), layer)]
    in_specs += [pl.BlockSpec(memory_space=pl.ANY)] * n_alias
    stacked = jax.ShapeDtypeStruct((DEPTH * N_S, 2 * GW), F32)
    out_shape = [jax.ShapeDtypeStruct((N_S, 3 * ATT), F32), stacked, stacked, stacked,
                 jax.ShapeDtypeStruct((N_S, GMW), F32),
                 jax.ShapeDtypeStruct((DEPTH * N_S, GMW), F32)]
    lay = lambda i: (layer, 0)
    out_specs = [pl.BlockSpec((N_S, 3 * ATT), lambda i: (0, 0)),
                 pl.BlockSpec((N_S, 2 * GW), lay), pl.BlockSpec((N_S, 2 * GW), lay),
                 pl.BlockSpec((N_S, 2 * GW), lay),
                 pl.BlockSpec((N_S, GMW), lambda i: (0, 0)),
                 pl.BlockSpec((N_S, GMW), lay)]
    args = [x, gain, w, vnorm] + ([] if prev is None else list(prev))
    aliases = {4: 1, 5: 2, 6: 3, 7: 5} if prev is not None else {}
    return pl.pallas_call(
        functools.partial(_inproj_sample_body, n_alias=n_alias),
        out_shape=out_shape,
        grid=(1,),
        in_specs=in_specs,
        out_specs=out_specs,
        input_output_aliases=aliases,
        compiler_params=_params(1),
        name="inproj_sample",
    )(*args)


def _alibi_slopes():
    h = np.arange(1, NG * NH + 1, dtype=np.float32)
    return np.power(np.float32(2.0), -8.0 * h / (NG * NH)).astype(np.float32).reshape(NG, NH)


def _head_masks():
    lane = lax.broadcasted_iota(jnp.int32, (1, GW), 1)
    return [(lane >= h * HD) & (lane < (h + 1) * HD) for h in range(NH)]


def _stack_heads(q, masks):
    return jnp.concatenate([jnp.where(m, q, jnp.zeros_like(q)) for m in masks], axis=0)


def _unstack_heads(x, masks, rows):
    out = jnp.zeros((rows, GW), F32)
    for h, m in enumerate(masks):
        out = jnp.where(m, x[h * rows:(h + 1) * rows], out)
    return out


def _softmax_parts(parts, values, value_dots):
    m = functools.reduce(jnp.maximum, [jnp.max(s, axis=-1, keepdims=True) for s in parts])
    ps = [jnp.exp(s - m) for s in parts]
    l = functools.reduce(jnp.add, [jnp.sum(p, axis=-1, keepdims=True) for p in ps])
    o = functools.reduce(jnp.add, [dot(p.astype(BF16), v) for p, v, dot in zip(ps, values, value_dots)])
    return o / l, m + jnp.log(l)


def _merge_groups(outs, lses):
    m = jnp.maximum(jnp.maximum(lses[0], lses[1]), lses[2])
    es = [jnp.exp(l - m) for l in lses]
    tot = es[0] + es[1] + es[2]
    return (es[0] / tot) * outs[0] + (es[1] / tot) * outs[1] + (es[2] / tot) * outs[2]


def _prompt_bias_table():
    slopes = _alibi_slopes()
    a = np.arange(KB)[:, None]
    c = np.arange(2 * KB)[None, :]
    dist = KB + a - c
    valid = (dist >= 0) & (dist <= KB)
    tab = np.empty((NG, NH * KB, 2 * KB), np.float32)
    for g in range(NG):
        for h in range(NH):
            bias = -slopes[g, h] * (DILS[g] * dist).astype(np.float32)
            tab[g, h * KB:(h + 1) * KB] = np.where(valid, bias, np.float32(NEG))
    return tab


def _sample_bias_tables():
    slopes = _alibi_slopes()
    j = np.arange(SPAD)[:, None]
    tabs = []
    for g in range(NG):
        w, d = WINDOWS[g], DILS[g]
        c = np.arange(w)[None, :]
        cn = np.arange(KB)[None, :]
        dist = np.concatenate([np.broadcast_to(w + j - c, (SPAD, w)),
                               np.broadcast_to(j - cn, (SPAD, KB))], axis=1)
        real = np.concatenate([np.ones((SPAD, w), bool), np.broadcast_to(cn < T_S, (SPAD, KB))], axis=1)
        valid = (j < T_S) & real & (dist >= 0) & (dist <= w) & (dist % d == 0)
        pad_row = np.broadcast_to(j >= T_S, dist.shape)
        tab = np.empty((NH * SPAD, dist.shape[1]), np.float32)
        for h in range(NH):
            bias = -slopes[g, h] * dist.astype(np.float32)
            blk = np.where(valid, bias, np.float32(NEG))
            tab[h * SPAD:(h + 1) * SPAD] = np.where(pad_row, np.float32(0.0), blk)
        tabs.append(tab)
    return tabs


def _attn_sample_body(z_ref, c1_ref, c2_ref, c3_ref, b1_ref, b2_ref, b3_ref, att_ref):
    for s in range(z_ref.shape[0]):
        _attn_sample_one(z_ref.at[s], (c1_ref.at[s], c2_ref.at[s], c3_ref.at[s]),
                         (b1_ref, b2_ref, b3_ref), att_ref.at[s])


def _attn_sample_one(z_ref, c_refs, b_refs, att_ref):
    masks = _head_masks()
    z = z_ref[...]
    pad = jnp.zeros((KB - SPAD, GW), F32)
    outs, lses = [], []
    for g, (c_ref, b_ref) in enumerate(zip(c_refs, b_refs)):
        w = WINDOWS[g]
        q = z[:, g * GW:(g + 1) * GW]
        k_new = jnp.concatenate([z[:, ATT + g * GW:ATT + (g + 1) * GW], pad], axis=0).astype(BF16)
        v_new = jnp.concatenate([z[:, 2 * ATT + g * GW:2 * ATT + (g + 1) * GW], pad], axis=0).astype(BF16)
        k_old = c_ref[:GW, :].astype(BF16)
        v_old = c_ref[GW:, :].astype(BF16)
        qs = _stack_heads(q, masks).astype(BF16)
        parts = [_dot(qs, k_old) + b_ref[:, :w], _dot_nt(qs, k_new) + b_ref[:, w:]]
        o, lse = _softmax_parts(parts, [v_old, v_new], [_dot_nt, _dot])
        outs.append(_unstack_heads(o, masks, SPAD))
        lses.append(_unstack_heads(lse, masks, SPAD))
    att_ref[...] = _merge_groups(outs, lses)


def _attn_sample_operands(z, caches, tabs, layer, n_steps, step=lambda i: i):
    per_step = B_S // n_steps
    assert per_step * n_steps == B_S
    zp = jnp.pad(z.reshape(B_S, T_S, 3 * ATT), ((0, 0), (0, SPAD - T_S), (0, 0)))
    cs = [c.transpose(0, 1, 3, 4, 5, 2).reshape(DEPTH, B_S, 2 * GW, w) for c, w in zip(caches, WINDOWS)]
    in_specs = [pl.BlockSpec((per_step, SPAD, 3 * ATT), lambda *ids: (step(*ids), 0, 0))]
    in_specs += [pl.BlockSpec((None, per_step, 2 * GW, w), lambda *ids: (layer, step(*ids), 0, 0)) for w in WINDOWS]
    in_specs += [_const(t.shape) for t in tabs]
    out_spec = pl.BlockSpec((per_step, SPAD, GW), lambda *ids: (step(*ids), 0, 0))
    return [zp, *cs, *tabs], in_specs, out_spec, jax.ShapeDtypeStruct((B_S, SPAD, GW), F32)


def _mix_out(x, h, att, spat, wgate_ref, pa_ref, ps_ref, wo_ref):
    ga = jax.nn.sigmoid(_dot(h, wgate_ref[:, :D]))
    gb = jax.nn.sigmoid(_dot(h, wgate_ref[:, D:]))
    mix = ga * _dot(att.astype(BF16), pa_ref[...]) + gb * _dot(spat, ps_ref[...])
    return x + _dot(mix.astype(BF16), wo_ref[...])


def _mixer_out_body(x_ref, h_ref, wgate_ref, q1_ref, kv1_ref, kv1p_ref, q2_ref, kv2_ref, kv2p_ref, q3_ref, kv3_ref,
                    bias_ref, u_ref, vs_ref, ws_ref, b_ref, pa_ref, ps_ref, wo_ref, *rest):
    if len(rest) > 4:
        _attn_sample_body(*rest[:7], rest[8])
        out_ref, o_scr, l_scr, spat_scr = rest[7], *rest[9:]
    else:
        out_ref, o_scr, l_scr, spat_scr = rest
    j = pl.program_id(1)
    masks = _head_masks()
    lane = lax.broadcasted_iota(jnp.int32, (1, 2 * KB), 1)
    first_tile_mask = (j > 0) | (lane >= KB)

    def wide(ref, base, rows):
        return jnp.concatenate([ref[base + sl, rows, :] for sl in range(SLABS)], axis=1)

    def band(g, q, kv, bias, rows, n):
        qs = _stack_heads(q, masks)
        s = _dot_nt(qs, kv[:, :GW]) + bias
        m = jnp.max(s, axis=-1, keepdims=True)
        p = jnp.exp(s - m)
        l = jnp.sum(p, axis=-1, keepdims=True)
        o = _unstack_heads(_dot(p.astype(BF16), kv[:, GW:]) / l, masks, n)
        lse = _unstack_heads(m + jnp.log(l), masks, n)
        for sl in range(SLABS):
            o_scr[g * SLABS + sl, rows, :] = o[:, sl * LANES:(sl + 1) * LANES]
            l_scr[g * SLABS + sl, rows, :] = lse[:, sl * LANES:(sl + 1) * LANES]

    bias1 = bias_ref[0]
    for blk in range(TM // KB):
        own = pl.ds(blk * KB, KB)
        if blk == 0:
            kv = jnp.concatenate([kv1p_ref[...], kv1_ref[own, :]], axis=0)
            bias = jnp.where(first_tile_mask, bias1, NEG)
        else:
            kv, bias = kv1_ref[pl.ds((blk - 1) * KB, 2 * KB), :], bias1
        band(0, q1_ref[own, :], kv, bias, own, KB)

    bias2 = jnp.where(first_tile_mask, bias_ref[1], NEG)
    for r in range(DILS[1]):
        kv = jnp.concatenate([kv2p_ref[r], kv2_ref[r]], axis=0)
        band(1, q2_ref[r], kv, bias2, pl.ds(r, KB, stride=DILS[1]), KB)

    nq = TM // DILS[2]
    q0 = pl.multiple_of(j * nq, nq)
    bias3 = jnp.concatenate([bias_ref[2, pl.ds(h * KB + q0, nq), KB:] for h in range(NH)], axis=0)
    for r in range(DILS[2]):
        band(2, q3_ref[r], kv3_ref[r], bias3, pl.ds(r, nq, stride=DILS[2]), nq)

    full = slice(None)
    att = _merge_groups([wide(o_scr, g * SLABS, full) for g in range(NG)],
                        [wide(l_scr, g * SLABS, full) for g in range(NG)])

    x = x_ref[...]
    h = h_ref[...]
    row = lax.broadcasted_iota(jnp.int32, (CHUNK, CHUNK), 0)
    col = lax.broadcasted_iota(jnp.int32, (CHUNK, CHUNK), 1)
    for g in range(GG):
        w = jnp.where(col <= row, ws_ref[g], 0.0).astype(BF16)
        gs = slice(g * GC, (g + 1) * GC)
        for c in range(TM // CHUNK):
            rs = slice(c * CHUNK, (c + 1) * CHUNK)
            zc = _dot(w, vs_ref[rs, gs]) + b_ref[g]
            spat_scr[rs, gs] = (u_ref[rs, gs] * zc).astype(BF16)
    out_ref[...] = _mix_out(x, h, att, spat_scr[...], wgate_ref, pa_ref, ps_ref, wo_ref)


def _mixer_out(x, h, wgate, attn_ops, bias_tab, u, vs, ws, bias, pa, ps, wo, layer, attn=None):
    tiles = S_P // TM
    assert TM == KB * DILS[1] and S_P == KB * DILS[2]
    q1, kv1, q2, kv2, q3, kv3 = attn_ops
    row = lambda b, j: (b * tiles + j, 0)
    res = lambda b, j: (b, 0, j, 0)
    res_before = lambda b, j: (b, 0, jnp.maximum(j - 1, 0), 0)
    row_before = lambda b, j: (jnp.maximum((b * tiles + j) * (TM // KB) - 1, 0), 0)
    d2, d3 = DILS[1], DILS[2]
    in_specs = [pl.BlockSpec((TM, D), row), pl.BlockSpec((TM, D), row), _const((D, 2 * D)),
                pl.BlockSpec((TM, GW), row), pl.BlockSpec((TM, 2 * GW), row),
                pl.BlockSpec((KB, 2 * GW), row_before),
                pl.BlockSpec((None, d2, TM // d2, GW), res), pl.BlockSpec((None, d2, TM // d2, 2 * GW), res),
                pl.BlockSpec((None, d2, TM // d2, 2 * GW), res_before),
                pl.BlockSpec((None, d3, TM // d3, GW), res),
                pl.BlockSpec((None, d3, S_P // d3, 2 * GW), lambda b, j: (b, 0, 0, 0)),
                _const(bias_tab.shape),
                pl.BlockSpec((TM, GMW), row), pl.BlockSpec((TM, GMW), row),
                _resident((GG, CHUNK, CHUNK), layer), _resident((GG, CHUNK, 1), layer),
                _const((GW, D)), _const((GMW, D)), _const((D, D))]
    args = [x, h, wgate, q1, kv1, kv1, q2, kv2, kv2, q3, kv3, bias_tab, u, vs, ws, bias, pa, ps, wo]
    out_specs, out_shape = [pl.BlockSpec((TM, D), row)], [jax.ShapeDtypeStruct((N_P, D), F32)]
    if attn is not None:
        attn_args, attn_in_specs, attn_out_spec, attn_out_shape = attn(B_P * tiles, lambda b, j: b * tiles + j)
        in_specs += attn_in_specs
        args += attn_args
        out_specs.append(attn_out_spec)
        out_shape.append(attn_out_shape)
    outs = pl.pallas_call(
        _mixer_out_body,
        out_shape=out_shape,
        grid=(B_P, tiles),
        in_specs=in_specs,
        out_specs=out_specs,
        scratch_shapes=[pltpu.VMEM((NG * SLABS, TM, LANES), F32), pltpu.VMEM((NG * SLABS, TM, LANES), F32),
                        pltpu.VMEM((TM, GMW), BF16)],
        compiler_params=_params(2),
        name="mixer_out",
    )(*args)
    return outs[0], (outs[1] if attn is not None else None)


def _outproj_sample_body(x_ref, g_ref, wgate_ref, att_ref, u_ref, vs_ref, wt_ref, b_ref,
                         pa_ref, ps_ref, wo_ref, out_ref):
    x = x_ref[...]
    h = _rms(x, g_ref[...]).astype(BF16)
    row = lax.broadcasted_iota(jnp.int32, (N_S, N_S), 0)
    col = lax.broadcasted_iota(jnp.int32, (N_S, N_S), 1)
    keep = (row // T_S == col // T_S) & (col <= row)
    u = u_ref[...]
    vs = vs_ref[...].astype(BF16)
    parts = []
    for g in range(GG):
        corner = wt_ref[g]
        w = jnp.zeros((N_S, N_S), F32)
        for t in range(T_S):
            w = jnp.where(keep & (lax.rem(row, T_S) == t), corner[t:t + 1, :], w)
        w = w.astype(BF16)
        gs = slice(g * GC, (g + 1) * GC)
        parts.append(u[:, gs] * (_dot(w, vs[:, gs]) + b_ref[g]))
    spat = jnp.concatenate(parts, axis=1).astype(BF16)
    out_ref[...] = _mix_out(x, h, att_ref[...], spat, wgate_ref, pa_ref, ps_ref, wo_ref)


def _outproj_sample(x, gain, wgate, att, u, vs_all, wt, bias, pa, ps, wo, layer):
    zero = lambda i: (0, 0)
    in_specs = [pl.BlockSpec((N_S, D), zero), _resident((1, D), layer), _const((D, 2 * D)),
                pl.BlockSpec((N_S, GW), zero), pl.BlockSpec((N_S, GMW), zero),
                pl.BlockSpec((N_S, GMW), lambda i: (layer, 0)),
                _resident((GG, T_S, N_S), layer), _resident((GG, N_S, 1), layer),
                _const((GW, D)), _const((GMW, D)), _const((D, D))]
    return pl.pallas_call(
        _outproj_sample_body,
        out_shape=jax.ShapeDtypeStruct((N_S, D), F32),
        grid=(1,),
        in_specs=in_specs,
        out_specs=pl.BlockSpec((N_S, D), zero),
        compiler_params=_params(1),
        name="outproj_sample",
    )(x, gain, wgate, att, u, vs_all, wt, bias, pa, ps, wo)


def kernel(x_prompt, x_sample, cache_kv_w128, cache_kv_w512, cache_kv_w2048, ffn1_norm, ffn1_gate, ffn1_up, ffn1_down, mix_norm, w_in, gmlp_v_norm, gmlp_ws, gmlp_bias, proj_att, proj_spatial, w_out, ffn2_norm, ffn2_gate, ffn2_up, ffn2_down, final_norm):
    caches = (cache_kv_w128, cache_kv_w512, cache_kv_w2048)
    xp = x_prompt.reshape(N_P, D)
    xs = x_sample.reshape(N_S, D)

    ffn1_jobs = lambda l: [_CastJob(ffn1_gate, l), _CastJob(ffn1_up, l), _CastJob(ffn1_down, l)]
    ffn2_jobs = lambda l: [_CastJob(ffn2_gate, l), _CastJob(ffn2_up, l), _CastJob(ffn2_down, l)]
    mix_jobs = lambda l: [_CastJob(w_in, l, [(0, QKVUV), (QKVUV, QKVUV + 2 * D)]),
                          _CastJob(proj_att, l), _CastJob(proj_spatial, l), _CastJob(w_out, l)]

    n1 = ffn1_norm.reshape(DEPTH, 1, D)
    n2 = ffn2_norm.reshape(DEPTH, 1, D)
    nm = mix_norm.reshape(DEPTH, 1, D)
    nv = gmlp_v_norm.reshape(DEPTH, 1, GMW)
    nf = final_norm.reshape(1, D)
    bias_p = gmlp_bias.reshape(DEPTH, GG, CHUNK, 1)
    wt_s = jnp.tile(gmlp_ws[:, :, :T_S, :T_S], (1, 1, 1, B_S))
    bias_s = jnp.tile(gmlp_bias[:, :, :T_S], (1, 1, B_S)).reshape(DEPTH, GG, N_S, 1)

    attn_bias = jnp.asarray(_prompt_bias_table())
    sample_tabs = [jnp.asarray(t) for t in _sample_bias_tables()]

    f1 = {0: _cast(ffn1_jobs(0))}
    f2, mix = {}, {}
    kv_p = kv_s = None
    for l in range(DEPTH):
        last = l == DEPTH - 1
        fin = nf if last else None
        xp, hp, outs, _ = _ffn(xp, n1, l, f1[l], next_gain=nm,
                               jobs=ffn2_jobs(l) + (mix_jobs(l) if l == 0 else []))
        f2[l] = outs[:3]
        if l == 0:
            mix[l] = outs[3:]
        xs, _, _, _ = _ffn(xs, n1, l, f1[l])
        z_s, ks1, ks2, ks3, u_s, vs_s = _inproj_sample(xs, nm, mix[l][0], nv, l, kv_s)
        kv_s = (ks1, ks2, ks3, vs_s)
        attn = functools.partial(_attn_sample_operands, z_s, caches, sample_tabs, l)
        kv_p, attn_ops, u, vs = _inproj_prompt(hp, mix[l][0], nv, l, kv_p)
        xp, att = _mixer_out(xp, hp, mix[l][1], attn_ops, attn_bias, u, vs, gmlp_ws, bias_p, *mix[l][2:], l,
                             attn=attn)
        xp, _, outs, _ = _ffn(xp, n2, l, f2[l], fin, jobs=[] if last else ffn1_jobs(l + 1) + mix_jobs(l + 1))
        if not last:
            f1[l + 1], mix[l + 1] = outs[:3], outs[3:]
        att_s = att[:, :T_S].reshape(N_S, GW)
        xs = _outproj_sample(xs, nm, mix[l][1], att_s, u_s, vs_s, wt_s, bias_s, *mix[l][2:], l)
        xs, _, _, _ = _ffn(xs, n2, l, f2[l], fin)

    kv_out = [k.reshape(DEPTH, B_P, 2, NH, HD, w).transpose(0, 1, 5, 2, 3, 4) for k, w in zip(kv_p, WINDOWS)]
    ks_shape = (DEPTH, B_S, T_S, 2, NH, HD)
    return (xp.reshape(B_P, S_P, D), xs.reshape(B_S, T_S, D), kv_out[0], kv_out[1], kv_out[2],
            kv_s[0].reshape(ks_shape), kv_s[1].reshape(ks_shape), kv_s[2].reshape(ks_shape),
            kv_s[3].reshape(DEPTH, B_S, T_S, GMW))
```

```python
import functools

import numpy as np
import jax
import jax.numpy as jnp
from jax import lax
from jax.experimental import pallas as pl
from jax.experimental.pallas import tpu as pltpu

F32 = jnp.float32
BF16 = jnp.bfloat16

D = 1024
DEPTH = 2
B_P, S_P = 8, 2048
B_S, T_S = 32, 4
N_P = B_P * S_P
N_S = B_S * T_S
HD, NH, NG = 64, 4, 3
GW = NH * HD
ATT = NG * GW
WINDOWS = (128, 512, 2048)
DILS = (1, 4, 16)
KB = 128
CHUNK = 128
GG, GC = 4, 128
GMW = GG * GC
DFF = 2816
FC = 256
QKVUV = 3 * ATT + 2 * GMW
EPS = 1e-6
NEG = -1e30
TM = 512
TM_FFN = 1024
FFN_SUB = 256
IN_SUB = 256
SPAD = 16
LANES = 128
SUBLANES_BF16 = 16
SLABS = GW // LANES
N_SLAB = 3 * ATT // LANES
VMEM_LIMIT = 60 * 1024 * 1024

_NT = (((1,), (1,)), ((), ()))


def _rms(x, gain):
    r = lax.rsqrt(jnp.mean(x * x, axis=-1, keepdims=True) + EPS)
    return (x * r) * gain


def _dot(a, b):
    return jnp.dot(a, b, preferred_element_type=F32)


def _dot_nt(a, b):
    return lax.dot_general(a, b, _NT, preferred_element_type=F32)


def _params(n_axes):
    return pltpu.CompilerParams(dimension_semantics=("arbitrary",) * n_axes,
                                vmem_limit_bytes=VMEM_LIMIT)


def _resident(shape, layer):
    nd = len(shape)
    return pl.BlockSpec((None,) + shape, lambda *_: (layer,) + (0,) * nd,
                        pipeline_mode=pl.Buffered(1))


def _const(shape):
    nd = len(shape)
    return pl.BlockSpec(shape, lambda *_: (0,) * nd, pipeline_mode=pl.Buffered(1))


class _CastJob:
    def __init__(self, src, layer, col_ranges=None):
        self.src, self.layer = src, layer
        self.rows, self.cols = src.shape[1:]
        self.col_ranges = col_ranges or [(0, self.cols)]

    def specs(self, n_steps):
        rps = self.rows // n_steps
        if rps % SUBLANES_BF16:
            rps = self.rows // (n_steps // 2)
        assert rps % SUBLANES_BF16 == 0 and self.rows % rps == 0
        last = self.rows // rps - 1
        layer = self.layer
        in_spec = pl.BlockSpec((None, rps, self.cols), lambda i: (layer, jnp.minimum(i, last), 0))
        out_specs = [pl.BlockSpec((rps, c1 - c0), lambda i: (jnp.minimum(i, last), 0))
                     for c0, c1 in self.col_ranges]
        out_shapes = [jax.ShapeDtypeStruct((self.rows, c1 - c0), BF16) for c0, c1 in self.col_ranges]
        return in_spec, out_specs, out_shapes


def _run_casts(jobs, src_refs, out_refs):
    out_refs = list(out_refs)
    for job, src_ref in zip(jobs, src_refs):
        x = src_ref[...]
        for c0, c1 in job.col_ranges:
            out_refs.pop(0)[...] = x[:, c0:c1].astype(BF16)


def _cast_specs(jobs, n_steps):
    in_specs, out_specs, out_shapes = [], [], []
    for job in jobs:
        i, o, s = job.specs(n_steps)
        in_specs.append(i)
        out_specs += o
        out_shapes += s
    return in_specs, out_specs, out_shapes


def _cast_body(*refs, jobs):
    _run_casts(jobs, refs[:len(jobs)], refs[len(jobs):])


def _cast(jobs):
    n_steps = 16
    in_specs, out_specs, out_shapes = _cast_specs(jobs, n_steps)
    return pl.pallas_call(
        functools.partial(_cast_body, jobs=jobs),
        out_shape=out_shapes,
        grid=(n_steps,),
        in_specs=in_specs,
        out_specs=out_specs,
        compiler_params=_params(1),
        name="cast_bf16",
    )(*[j.src for j in jobs])


def _ffn_rows(x_ref, o_ref, h_ref, a_scr, g_ref, tg_ref, w_refs, tail, n_rows):
    wg_ref, wu_ref, wd_ref = w_refs
    sub = min(FFN_SUB, n_rows)
    for r0 in range(0, n_rows, sub):
        rows = slice(r0, r0 + sub)
        x = x_ref[rows, :]
        h = _rms(x, g_ref[...]).astype(BF16)
        for c in range(DFF // FC):
            cs = slice(c * FC, (c + 1) * FC)
            gate = _dot(h, wg_ref[:, cs])
            up = _dot(h, wu_ref[:, cs])
            a_scr[rows, cs] = (jax.nn.silu(gate) * up).astype(BF16)
        y = x + 0.5 * _dot(a_scr[rows, :], wd_ref[...])
        if tail == "final":
            y = _rms(y, tg_ref[...])
        o_ref[rows, :] = y
        if tail == "next":
            h_ref[rows, :] = _rms(y, tg_ref[...]).astype(BF16)


def _ffn_body(*refs, tail, jobs, n_main):
    refs = list(refs)
    x_ref, xs_ref, g_ref = refs[:3]
    w_refs = refs[3:6]
    del refs[:6]
    tg_ref = refs.pop(0) if tail else None
    cast_srcs = [refs.pop(0) for _ in jobs]
    o_ref, os_ref = refs.pop(0), refs.pop(0)
    h_ref = refs.pop(0) if tail == "next" else None
    cast_outs = [refs.pop(0) for _ in range(sum(len(j.col_ranges) for j in jobs))]
    a_scr, = refs
    i = pl.program_id(0)

    @pl.when(i < n_main)
    def _():
        _run_casts(jobs, cast_srcs, cast_outs)
        _ffn_rows(x_ref, o_ref, h_ref, a_scr, g_ref, tg_ref, w_refs, tail, x_ref.shape[0])

    @pl.when(i == n_main)
    def _():
        _ffn_rows(xs_ref, os_ref, None, a_scr, g_ref, tg_ref, w_refs,
                  tail if tail == "final" else None, xs_ref.shape[0])


def _ffn(x, xs, gain, layer, weights, final_gain=None, next_gain=None, jobs=()):
    n, ns = x.shape[0], xs.shape[0]
    n_main = n // TM_FFN
    assert n_main * TM_FFN == n and ns <= TM_FFN
    assert final_gain is None or next_gain is None
    tail = "final" if final_gain is not None else "next" if next_gain is not None else None
    jobs = list(jobs)
    row = pl.BlockSpec((TM_FFN, D), lambda i: (jnp.minimum(i, n_main - 1), 0))
    srow = pl.BlockSpec((ns, D), lambda i: (0, 0))
    in_specs = [row, srow, _resident((1, D), layer), _const((D, DFF)), _const((D, DFF)), _const((DFF, D))]
    args = [x, xs, gain, *weights]
    out_specs = [row, srow]
    out_shape = [jax.ShapeDtypeStruct((n, D), F32), jax.ShapeDtypeStruct((ns, D), F32)]
    if tail == "final":
        in_specs.append(_const((1, D)))
        args.append(final_gain)
    elif tail == "next":
        in_specs.append(_resident((1, D), layer))
        args.append(next_gain)
        out_specs.append(row)
        out_shape.append(jax.ShapeDtypeStruct((n, D), BF16))
    cast_in, cast_out, cast_shapes = _cast_specs(jobs, n_main)
    in_specs += cast_in
    args += [j.src for j in jobs]
    n_head = len(out_shape)
    out_specs += cast_out
    out_shape += cast_shapes
    outs = pl.pallas_call(
        functools.partial(_ffn_body, tail=tail, jobs=jobs, n_main=n_main),
        out_shape=out_shape,
        grid=(n_main + 1,),
        in_specs=in_specs,
        out_specs=out_specs,
        scratch_shapes=[pltpu.VMEM((TM_FFN, DFF), BF16)],
        compiler_params=_params(1),
        name="ffn_" + tail if tail else "ffn",
    )(*args)
    return outs[0], (outs[2] if tail == "next" else None), outs[1], list(outs[n_head:])


def _project(h, w_ref, sec, g):
    c0 = sec * ATT + g * GW
    z = _dot(h, w_ref[:, c0:c0 + GW])
    if sec == 0:
        z = z * (HD ** -0.5)
    return z


def _gmlp_inputs(h, w_ref, vn_ref, u_ref, vs_ref):
    u_ref[...] = jax.nn.gelu(_dot(h, w_ref[:, 3 * ATT:3 * ATT + GMW]))
    gv = jax.nn.gelu(_dot(h, w_ref[:, 3 * ATT + GMW:QKVUV]))
    vs_ref[...] = _rms(gv, vn_ref[...]).astype(vs_ref.dtype)


def _inproj_prompt_body(*refs, n_alias):
    refs = list(refs)
    h_ref, w_ref, vn_ref = refs[:3]
    del refs[:3 + n_alias]
    kv1_ref, kv2_ref, kv3_ref = refs[:3]
    q_refs, kvb_refs = refs[3:9:2], refs[4:9:2]
    u_ref, vs_ref, z_scr, t_scr = refs[9:]
    for r0 in range(0, TM, IN_SUB):
        _inproj_rows(h_ref, w_ref, vn_ref, (kv1_ref, kv2_ref, kv3_ref), q_refs, kvb_refs,
                     u_ref, vs_ref, z_scr, t_scr, r0)


def _inproj_rows(h_ref, w_ref, vn_ref, kvt_refs, q_refs, kvb_refs, u_ref, vs_ref, z_scr, t_scr, r0):
    rows = slice(r0, r0 + IN_SUB)
    h = h_ref[rows, :]
    _gmlp_inputs(h, w_ref, vn_ref, u_ref.at[rows, :], vs_ref.at[rows, :])
    for sec in (2, 1, 0):
        for g in reversed(range(NG)):
            z = _project(h, w_ref, sec, g)
            dst = q_refs[g] if sec == 0 else kvb_refs[g]
            cols = slice(GW, 2 * GW) if sec == 2 else slice(0, GW)
            if g == 0:
                dst[rows, cols] = z.astype(BF16)
            else:
                d = DILS[g]
                n = IN_SUB // d
                base = (sec * (NG - 1) + g - 1) * SLABS
                for sl in range(SLABS):
                    z_scr[base + sl, rows, :] = z[:, sl * LANES:(sl + 1) * LANES]
                src, step, first = z_scr, d, lambda r: r0 + r
                if d == DILS[1] ** 2:
                    step = DILS[1]
                    n4 = IN_SUB // step
                    for sl in range(SLABS):
                        for r4 in range(step):
                            t_scr[sec * SLABS + sl, pl.ds(r0 + r4 * n4, n4), :] = (
                                z_scr[base + sl, pl.ds(r0 + r4, n4, stride=step), :])
                    src, base, first = t_scr, sec * SLABS, lambda r: r0 + (r % step) * n4 + r // step
                for r in range(d):
                    sub = [src[base + sl, pl.ds(first(r), n, stride=step), :] for sl in range(SLABS)]
                    dst[r, pl.ds(r0 // d, n), cols] = jnp.concatenate(sub, axis=1).astype(BF16)
            if sec == 0:
                continue
            ch = slice((sec - 1) * GW, sec * GW)
            if g > 0:
                kvt_refs[g][ch, rows] = z.T
            elif r0 + IN_SUB == TM:
                kvt_refs[g][ch, :] = z[IN_SUB - WINDOWS[0]:, :].T


def _inproj_prompt(h, w, vnorm, layer, prev):
    tiles = S_P // TM
    n_alias = 0 if prev is None else 3
    in_specs = [pl.BlockSpec((TM, D), lambda i: (i, 0)), _const((D, QKVUV)), _resident((1, GMW), layer)]
    in_specs += [pl.BlockSpec(memory_space=pl.ANY)] * n_alias
    seq = lambda i: (layer * B_P + i // tiles, 0, 0)
    row = lambda i: (i, 0)
    res = lambda i: (i // tiles, 0, lax.rem(i, tiles), 0)
    out_shape = [jax.ShapeDtypeStruct((DEPTH * B_P, 2 * GW, w), F32) for w in WINDOWS]
    out_specs = [pl.BlockSpec((None, 2 * GW, WINDOWS[0]), seq),
                 pl.BlockSpec((None, 2 * GW, WINDOWS[1]), seq),
                 pl.BlockSpec((None, 2 * GW, TM), lambda i: (layer * B_P + i // tiles, 0, lax.rem(i, tiles)))]
    for g, d in enumerate(DILS):
        for width in (GW, 2 * GW):
            if g == 0:
                out_shape.append(jax.ShapeDtypeStruct((N_P, width), BF16))
                out_specs.append(pl.BlockSpec((TM, width), row))
            else:
                out_shape.append(jax.ShapeDtypeStruct((B_P, d, S_P // d, width), BF16))
                out_specs.append(pl.BlockSpec((None, d, TM // d, width), res))
    out_shape += [jax.ShapeDtypeStruct((N_P, GMW), F32), jax.ShapeDtypeStruct((N_P, GMW), BF16)]
    out_specs += [pl.BlockSpec((TM, GMW), row), pl.BlockSpec((TM, GMW), row)]
    args = [h, w, vnorm] + ([] if prev is None else list(prev))
    outs = pl.pallas_call(
        functools.partial(_inproj_prompt_body, n_alias=n_alias),
        out_shape=out_shape,
        grid=(N_P // TM,),
        in_specs=in_specs,
        out_specs=out_specs,
        scratch_shapes=[pltpu.VMEM((3 * (NG - 1) * SLABS, TM, LANES), F32),
                        pltpu.VMEM((3 * SLABS, TM, LANES), F32)],
        input_output_aliases={3 + k: k for k in range(n_alias)},
        compiler_params=_params(1),
        name="inproj_prompt",
    )(*args)
    return tuple(outs[:3]), list(outs[3:9]), outs[9], outs[10]


def _inproj_sample_body(*refs, n_alias):
    x_ref, g_ref, w_ref, vn_ref = refs[:4]
    qkv_ref, kv1_ref, kv2_ref, kv3_ref, u_ref, vs_ref = refs[4 + n_alias:]
    kv_refs = (kv1_ref, kv2_ref, kv3_ref)
    h = _rms(x_ref[...], g_ref[...]).astype(BF16)
    for sec in range(3):
        for g in range(NG):
            z = _project(h, w_ref, sec, g)
            qkv_ref[:, sec * ATT + g * GW:sec * ATT + (g + 1) * GW] = z
            if sec > 0:
                kv_refs[g][:, (sec - 1) * GW:sec * GW] = z
    _gmlp_inputs(h, w_ref, vn_ref, u_ref, vs_ref)


def _inproj_sample(x, gain, w, vnorm, layer, prev):
    n_alias = 0 if prev is None else 4
    in_specs = [pl.BlockSpec((N_S, D), lambda i: (0, 0)),
                _resident((1, D), layer), _const((D, QKVUV)), _resident((1, GMW), layer)]
    in_specs += [pl.BlockSpec(memory_space=pl.ANY)] * n_alias
    stacked = jax.ShapeDtypeStruct((DEPTH * N_S, 2 * GW), F32)
    out_shape = [jax.ShapeDtypeStruct((N_S, 3 * ATT), F32), stacked, stacked, stacked,
                 jax.ShapeDtypeStruct((N_S, GMW), F32),
                 jax.ShapeDtypeStruct((DEPTH * N_S, GMW), F32)]
    lay = lambda i: (layer, 0)
    out_specs = [pl.BlockSpec((N_S, 3 * ATT), lambda i: (0, 0)),
                 pl.BlockSpec((N_S, 2 * GW), lay), pl.BlockSpec((N_S, 2 * GW), lay),
                 pl.BlockSpec((N_S, 2 * GW), lay),
                 pl.BlockSpec((N_S, GMW), lambda i: (0, 0)),
                 pl.BlockSpec((N_S, GMW), lay)]
    args = [x, gain, w, vnorm] + ([] if prev is None else list(prev))
    aliases = {4: 1, 5: 2, 6: 3, 7: 5} if prev is not None else {}
    return pl.pallas_call(
        functools.partial(_inproj_sample_body, n_alias=n_alias),
        out_shape=out_shape,
        grid=(1,),
        in_specs=in_specs,
        out_specs=out_specs,
        input_output_aliases=aliases,
        compiler_params=_params(1),
        name="inproj_sample",
    )(*args)


def _alibi_slopes():
    h = np.arange(1, NG * NH + 1, dtype=np.float32)
    return np.power(np.float32(2.0), -8.0 * h / (NG * NH)).astype(np.float32).reshape(NG, NH)


def _head_masks():
    lane = lax.broadcasted_iota(jnp.int32, (1, GW), 1)
    return [(lane >= h * HD) & (lane < (h + 1) * HD) for h in range(NH)]


def _stack_heads(q, masks):
    return jnp.concatenate([jnp.where(m, q, jnp.zeros_like(q)) for m in masks], axis=0)


def _unstack_heads(x, masks, rows):
    out = jnp.zeros((rows, GW), F32)
    for h, m in enumerate(masks):
        out = jnp.where(m, x[h * rows:(h + 1) * rows], out)
    return out


def _softmax_parts(parts, values, value_dots):
    m = functools.reduce(jnp.maximum, [jnp.max(s, axis=-1, keepdims=True) for s in parts])
    ps = [jnp.exp(s - m) for s in parts]
    l = functools.reduce(jnp.add, [jnp.sum(p, axis=-1, keepdims=True) for p in ps])
    o = functools.reduce(jnp.add, [dot(p.astype(BF16), v) for p, v, dot in zip(ps, values, value_dots)])
    return o / l, m + jnp.log(l)


def _merge_groups(outs, lses):
    m = jnp.maximum(jnp.maximum(lses[0], lses[1]), lses[2])
    es = [jnp.exp(l - m) for l in lses]
    tot = es[0] + es[1] + es[2]
    return (es[0] / tot) * outs[0] + (es[1] / tot) * outs[1] + (es[2] / tot) * outs[2]


def _prompt_bias_table():
    slopes = _alibi_slopes()
    a = np.arange(KB)[:, None]
    c = np.arange(2 * KB)[None, :]
    dist = KB + a - c
    valid = (dist >= 0) & (dist <= KB)
    tab = np.empty((NG, NH * KB, 2 * KB), np.float32)
    for g in range(NG):
        for h in range(NH):
            bias = -slopes[g, h] * (DILS[g] * dist).astype(np.float32)
            tab[g, h * KB:(h + 1) * KB] = np.where(valid, bias, np.float32(NEG))
    return tab


def _sample_bias_tables():
    slopes = _alibi_slopes()
    j = np.arange(SPAD)[:, None]
    tabs = []
    for g in range(NG):
        w, d = WINDOWS[g], DILS[g]
        c = np.arange(w)[None, :]
        cn = np.arange(KB)[None, :]
        dist = np.concatenate([np.broadcast_to(w + j - c, (SPAD, w)),
                               np.broadcast_to(j - cn, (SPAD, KB))], axis=1)
        real = np.concatenate([np.ones((SPAD, w), bool), np.broadcast_to(cn < T_S, (SPAD, KB))], axis=1)
        valid = (j < T_S) & real & (dist >= 0) & (dist <= w) & (dist % d == 0)
        pad_row = np.broadcast_to(j >= T_S, dist.shape)
        tab = np.empty((NH * SPAD, dist.shape[1]), np.float32)
        for h in range(NH):
            bias = -slopes[g, h] * dist.astype(np.float32)
            blk = np.where(valid, bias, np.float32(NEG))
            tab[h * SPAD:(h + 1) * SPAD] = np.where(pad_row, np.float32(0.0), blk)
        tabs.append(tab)
    return tabs


def _attn_sample_body(z_ref, c1_ref, c2_ref, c3_ref, b1_ref, b2_ref, b3_ref, att_ref):
    for s in range(z_ref.shape[0]):
        _attn_sample_one(z_ref.at[s], (c1_ref.at[s], c2_ref.at[s], c3_ref.at[s]),
                         (b1_ref, b2_ref, b3_ref), att_ref.at[s])


def _attn_sample_one(z_ref, c_refs, b_refs, att_ref):
    masks = _head_masks()
    z = z_ref[...]
    pad = jnp.zeros((KB - SPAD, GW), F32)
    outs, lses = [], []
    for g, (c_ref, b_ref) in enumerate(zip(c_refs, b_refs)):
        w = WINDOWS[g]
        q = z[:, g * GW:(g + 1) * GW]
        k_new = jnp.concatenate([z[:, ATT + g * GW:ATT + (g + 1) * GW], pad], axis=0).astype(BF16)
        v_new = jnp.concatenate([z[:, 2 * ATT + g * GW:2 * ATT + (g + 1) * GW], pad], axis=0).astype(BF16)
        k_old = c_ref[:GW, :].astype(BF16)
        v_old = c_ref[GW:, :].astype(BF16)
        qs = _stack_heads(q, masks).astype(BF16)
        parts = [_dot(qs, k_old) + b_ref[:, :w], _dot_nt(qs, k_new) + b_ref[:, w:]]
        o, lse = _softmax_parts(parts, [v_old, v_new], [_dot_nt, _dot])
        outs.append(_unstack_heads(o, masks, SPAD))
        lses.append(_unstack_heads(lse, masks, SPAD))
    att_ref[...] = _merge_groups(outs, lses)


def _attn_sample_operands(z, caches, tabs, layer, n_steps, step=lambda i: i):
    per_step = B_S // n_steps
    assert per_step * n_steps == B_S
    zp = jnp.pad(z.reshape(B_S, T_S, 3 * ATT), ((0, 0), (0, SPAD - T_S), (0, 0)))
    cs = [c.transpose(0, 1, 3, 4, 5, 2).reshape(DEPTH, B_S, 2 * GW, w) for c, w in zip(caches, WINDOWS)]
    in_specs = [pl.BlockSpec((per_step, SPAD, 3 * ATT), lambda *ids: (step(*ids), 0, 0))]
    in_specs += [pl.BlockSpec((None, per_step, 2 * GW, w), lambda *ids: (layer, step(*ids), 0, 0)) for w in WINDOWS]
    in_specs += [_const(t.shape) for t in tabs]
    out_spec = pl.BlockSpec((per_step, SPAD, GW), lambda *ids: (step(*ids), 0, 0))
    return [zp, *cs, *tabs], in_specs, out_spec, jax.ShapeDtypeStruct((B_S, SPAD, GW), F32)


def _mix_out(x, h, att, spat, wgate_ref, pa_ref, ps_ref, wo_ref):
    ga = jax.nn.sigmoid(_dot(h, wgate_ref[:, :D]))
    gb = jax.nn.sigmoid(_dot(h, wgate_ref[:, D:]))
    mix = ga * _dot(att.astype(BF16), pa_ref[...]) + gb * _dot(spat, ps_ref[...])
    return x + _dot(mix.astype(BF16), wo_ref[...])


def _mixer_out_body(x_ref, h_ref, wgate_ref, q1_ref, kv1_ref, kv1p_ref, q2_ref, kv2_ref, kv2p_ref, q3_ref, kv3_ref,
                    bias_ref, u_ref, vs_ref, ws_ref, b_ref, pa_ref, ps_ref, wo_ref, *rest):
    if len(rest) > 4:
        _attn_sample_body(*rest[:7], rest[8])
        out_ref, o_scr, l_scr, spat_scr = rest[7], *rest[9:]
    else:
        out_ref, o_scr, l_scr, spat_scr = rest
    j = pl.program_id(1)
    masks = _head_masks()
    lane = lax.broadcasted_iota(jnp.int32, (1, 2 * KB), 1)
    first_tile_mask = (j > 0) | (lane >= KB)

    def wide(ref, base, rows):
        return jnp.concatenate([ref[base + sl, rows, :] for sl in range(SLABS)], axis=1)

    def band(g, q, kv, bias, rows, n):
        qs = _stack_heads(q, masks)
        s = _dot_nt(qs, kv[:, :GW]) + bias
        m = jnp.max(s, axis=-1, keepdims=True)
        p = jnp.exp(s - m)
        l = jnp.sum(p, axis=-1, keepdims=True)
        o = _unstack_heads(_dot(p.astype(BF16), kv[:, GW:]) / l, masks, n)
        lse = _unstack_heads(m + jnp.log(l), masks, n)
        for sl in range(SLABS):
            o_scr[g * SLABS + sl, rows, :] = o[:, sl * LANES:(sl + 1) * LANES]
            l_scr[g * SLABS + sl, rows, :] = lse[:, sl * LANES:(sl + 1) * LANES]

    bias1 = bias_ref[0]
    for blk in range(TM // KB):
        own = pl.ds(blk * KB, KB)
        if blk == 0:
            kv = jnp.concatenate([kv1p_ref[...], kv1_ref[own, :]], axis=0)
            bias = jnp.where(first_tile_mask, bias1, NEG)
        else:
            kv, bias = kv1_ref[pl.ds((blk - 1) * KB, 2 * KB), :], bias1
        band(0, q1_ref[own, :], kv, bias, own, KB)

    bias2 = jnp.where(first_tile_mask, bias_ref[1], NEG)
    for r in range(DILS[1]):
        kv = jnp.concatenate([kv2p_ref[r], kv2_ref[r]], axis=0)
        band(1, q2_ref[r], kv, bias2, pl.ds(r, KB, stride=DILS[1]), KB)

    nq = TM // DILS[2]
    q0 = pl.multiple_of(j * nq, nq)
    bias3 = jnp.concatenate([bias_ref[2, pl.ds(h * KB + q0, nq), KB:] for h in range(NH)], axis=0)
    for r in range(DILS[2]):
        band(2, q3_ref[r], kv3_ref[r], bias3, pl.ds(r, nq, stride=DILS[2]), nq)

    full = slice(None)
    att = _merge_groups([wide(o_scr, g * SLABS, full) for g in range(NG)],
                        [wide(l_scr, g * SLABS, full) for g in range(NG)])

    x = x_ref[...]
    h = h_ref[...]
    row = lax.broadcasted_iota(jnp.int32, (CHUNK, CHUNK), 0)
    col = lax.broadcasted_iota(jnp.int32, (CHUNK, CHUNK), 1)
    for g in range(GG):
        w = jnp.where(col <= row, ws_ref[g], 0.0).astype(BF16)
        gs = slice(g * GC, (g + 1) * GC)
        for c in range(TM // CHUNK):
            rs = slice(c * CHUNK, (c + 1) * CHUNK)
            zc = _dot(w, vs_ref[rs, gs]) + b_ref[g]
            spat_scr[rs, gs] = (u_ref[rs, gs] * zc).astype(BF16)
    out_ref[...] = _mix_out(x, h, att, spat_scr[...], wgate_ref, pa_ref, ps_ref, wo_ref)


def _mixer_out(x, h, wgate, attn_ops, bias_tab, u, vs, ws, bias, pa, ps, wo, layer, attn=None):
    tiles = S_P // TM
    assert TM == KB * DILS[1] and S_P == KB * DILS[2]
    q1, kv1, q2, kv2, q3, kv3 = attn_ops
    row = lambda b, j: (b * tiles + j, 0)
    res = lambda b, j: (b, 0, j, 0)
    res_before = lambda b, j: (b, 0, jnp.maximum(j - 1, 0), 0)
    row_before = lambda b, j: (jnp.maximum((b * tiles + j) * (TM // KB) - 1, 0), 0)
    d2, d3 = DILS[1], DILS[2]
    in_specs = [pl.BlockSpec((TM, D), row), pl.BlockSpec((TM, D), row), _const((D, 2 * D)),
                pl.BlockSpec((TM, GW), row), pl.BlockSpec((TM, 2 * GW), row),
                pl.BlockSpec((KB, 2 * GW), row_before),
                pl.BlockSpec((None, d2, TM // d2, GW), res), pl.BlockSpec((None, d2, TM // d2, 2 * GW), res),
                pl.BlockSpec((None, d2, TM // d2, 2 * GW), res_before),
                pl.BlockSpec((None, d3, TM // d3, GW), res),
                pl.BlockSpec((None, d3, S_P // d3, 2 * GW), lambda b, j: (b, 0, 0, 0)),
                _const(bias_tab.shape),
                pl.BlockSpec((TM, GMW), row), pl.BlockSpec((TM, GMW), row),
                _resident((GG, CHUNK, CHUNK), layer), _resident((GG, CHUNK, 1), layer),
                _const((GW, D)), _const((GMW, D)), _const((D, D))]
    args = [x, h, wgate, q1, kv1, kv1, q2, kv2, kv2, q3, kv3, bias_tab, u, vs, ws, bias, pa, ps, wo]
    out_specs, out_shape = [pl.BlockSpec((TM, D), row)], [jax.ShapeDtypeStruct((N_P, D), F32)]
    if attn is not None:
        attn_args, attn_in_specs, attn_out_spec, attn_out_shape = attn(B_P * tiles, lambda b, j: b * tiles + j)
        in_specs += attn_in_specs
        args += attn_args
        out_specs.append(attn_out_spec)
        out_shape.append(attn_out_shape)
    outs = pl.pallas_call(
        _mixer_out_body,
        out_shape=out_shape,
        grid=(B_P, tiles),
        in_specs=in_specs,
        out_specs=out_specs,
        scratch_shapes=[pltpu.VMEM((NG * SLABS, TM, LANES), F32), pltpu.VMEM((NG * SLABS, TM, LANES), F32),
                        pltpu.VMEM((TM, GMW), BF16)],
        compiler_params=_params(2),
        name="mixer_out",
    )(*args)
    return outs[0], (outs[1] if attn is not None else None)


def _outproj_sample_body(x_ref, g_ref, wgate_ref, att_ref, u_ref, vs_ref, wt_ref, b_ref,
                         pa_ref, ps_ref, wo_ref, out_ref):
    x = x_ref[...]
    h = _rms(x, g_ref[...]).astype(BF16)
    row = lax.broadcasted_iota(jnp.int32, (N_S, N_S), 0)
    col = lax.broadcasted_iota(jnp.int32, (N_S, N_S), 1)
    keep = (row // T_S == col // T_S) & (col <= row)
    u = u_ref[...]
    vs = vs_ref[...].astype(BF16)
    parts = []
    for g in range(GG):
        corner = wt_ref[g]
        w = jnp.zeros((N_S, N_S), F32)
        for t in range(T_S):
            w = jnp.where(keep & (lax.rem(row, T_S) == t), corner[t:t + 1, :], w)
        w = w.astype(BF16)
        gs = slice(g * GC, (g + 1) * GC)
        parts.append(u[:, gs] * (_dot(w, vs[:, gs]) + b_ref[g]))
    spat = jnp.concatenate(parts, axis=1).astype(BF16)
    out_ref[...] = _mix_out(x, h, att_ref[...], spat, wgate_ref, pa_ref, ps_ref, wo_ref)


def _outproj_sample(x, gain, wgate, att, u, vs_all, wt, bias, pa, ps, wo, layer):
    zero = lambda i: (0, 0)
    in_specs = [pl.BlockSpec((N_S, D), zero), _resident((1, D), layer), _const((D, 2 * D)),
                pl.BlockSpec((N_S, GW), zero), pl.BlockSpec((N_S, GMW), zero),
                pl.BlockSpec((N_S, GMW), lambda i: (layer, 0)),
                _resident((GG, T_S, N_S), layer), _resident((GG, N_S, 1), layer),
                _const((GW, D)), _const((GMW, D)), _const((D, D))]
    return pl.pallas_call(
        _outproj_sample_body,
        out_shape=jax.ShapeDtypeStruct((N_S, D), F32),
        grid=(1,),
        in_specs=in_specs,
        out_specs=pl.BlockSpec((N_S, D), zero),
        compiler_params=_params(1),
        name="outproj_sample",
    )(x, gain, wgate, att, u, vs_all, wt, bias, pa, ps, wo)


def kernel(x_prompt, x_sample, cache_kv_w128, cache_kv_w512, cache_kv_w2048, ffn1_norm, ffn1_gate, ffn1_up, ffn1_down, mix_norm, w_in, gmlp_v_norm, gmlp_ws, gmlp_bias, proj_att, proj_spatial, w_out, ffn2_norm, ffn2_gate, ffn2_up, ffn2_down, final_norm):
    caches = (cache_kv_w128, cache_kv_w512, cache_kv_w2048)
    xp = x_prompt.reshape(N_P, D)
    xs = x_sample.reshape(N_S, D)

    ffn1_jobs = lambda l: [_CastJob(ffn1_gate, l), _CastJob(ffn1_up, l), _CastJob(ffn1_down, l)]
    ffn2_jobs = lambda l: [_CastJob(ffn2_gate, l), _CastJob(ffn2_up, l), _CastJob(ffn2_down, l)]
    mix_jobs = lambda l: [_CastJob(w_in, l, [(0, QKVUV), (QKVUV, QKVUV + 2 * D)]),
                          _CastJob(proj_att, l), _CastJob(proj_spatial, l), _CastJob(w_out, l)]

    n1 = ffn1_norm.reshape(DEPTH, 1, D)
    n2 = ffn2_norm.reshape(DEPTH, 1, D)
    nm = mix_norm.reshape(DEPTH, 1, D)
    nv = gmlp_v_norm.reshape(DEPTH, 1, GMW)
    nf = final_norm.reshape(1, D)
    bias_p = gmlp_bias.reshape(DEPTH, GG, CHUNK, 1)
    wt_s = jnp.tile(gmlp_ws[:, :, :T_S, :T_S], (1, 1, 1, B_S))
    bias_s = jnp.tile(gmlp_bias[:, :, :T_S], (1, 1, B_S)).reshape(DEPTH, GG, N_S, 1)

    attn_bias = jnp.asarray(_prompt_bias_table())
    sample_tabs = [jnp.asarray(t) for t in _sample_bias_tables()]

    f1 = {0: _cast(ffn1_jobs(0))}
    f2, mix = {}, {}
    kv_p = kv_s = None
    for l in range(DEPTH):
        last = l == DEPTH - 1
        fin = nf if last else None
        xp, hp, xs, outs = _ffn(xp, xs, n1, l, f1[l], next_gain=nm,
                                jobs=ffn2_jobs(l) + (mix_jobs(l) if l == 0 else []))
        f2[l] = outs[:3]
        if l == 0:
            mix[l] = outs[3:]
        z_s, ks1, ks2, ks3, u_s, vs_s = _inproj_sample(xs, nm, mix[l][0], nv, l, kv_s)
        kv_s = (ks1, ks2, ks3, vs_s)
        attn = functools.partial(_attn_sample_operands, z_s, caches, sample_tabs, l)
        kv_p, attn_ops, u, vs = _inproj_prompt(hp, mix[l][0], nv, l, kv_p)
        xp, att = _mixer_out(xp, hp, mix[l][1], attn_ops, attn_bias, u, vs, gmlp_ws, bias_p, *mix[l][2:], l,
                             attn=attn)
        att_s = att[:, :T_S].reshape(N_S, GW)
        xs = _outproj_sample(xs, nm, mix[l][1], att_s, u_s, vs_s, wt_s, bias_s, *mix[l][2:], l)
        xp, _, xs, outs = _ffn(xp, xs, n2, l, f2[l], fin, jobs=[] if last else ffn1_jobs(l + 1) + mix_jobs(l + 1))
        if not last:
            f1[l + 1], mix[l + 1] = outs[:3], outs[3:]

    kv_out = [k.reshape(DEPTH, B_P, 2, NH, HD, w).transpose(0, 1, 5, 2, 3, 4) for k, w in zip(kv_p, WINDOWS)]
    ks_shape = (DEPTH, B_S, T_S, 2, NH, HD)
    return (xp.reshape(B_P, S_P, D), xs.reshape(B_S, T_S, D), kv_out[0], kv_out[1], kv_out[2],
            kv_s[0].reshape(ks_shape), kv_s[1].reshape(ks_shape), kv_s[2].reshape(ks_shape),
            kv_s[3].reshape(DEPTH, B_S, T_S, GMW))
```

```python
import functools

import numpy as np
import jax
import jax.numpy as jnp
from jax import lax
from jax.experimental import pallas as pl
from jax.experimental.pallas import tpu as pltpu

F32 = jnp.float32
BF16 = jnp.bfloat16

D = 1024
DEPTH = 2
B_P, S_P = 8, 2048
B_S, T_S = 32, 4
N_P = B_P * S_P
N_S = B_S * T_S
HD, NH, NG = 64, 4, 3
GW = NH * HD
ATT = NG * GW
WINDOWS = (128, 512, 2048)
DILS = (1, 4, 16)
KB = 128
CHUNK = 128
GG, GC = 4, 128
GMW = GG * GC
DFF = 2816
FC = 256
QKVUV = 3 * ATT + 2 * GMW
EPS = 1e-6
NEG = -1e30
TM = 512
TM_FFN = 1024
TM_IN = 1024
FFN_SUB = 256
IN_SUB = 256
SPAD = 8
LANES = 128
SUBLANES_BF16 = 16
SLABS = GW // LANES
VMEM_LIMIT = 60 * 1024 * 1024

_NT = (((1,), (1,)), ((), ()))


def _rms(x, gain):
    r = lax.rsqrt(jnp.mean(x * x, axis=-1, keepdims=True) + EPS)
    return (x * r) * gain


def _dot(a, b):
    return jnp.dot(a, b, preferred_element_type=F32)


def _dot_nt(a, b):
    return lax.dot_general(a, b, _NT, preferred_element_type=F32)


def _params(n_axes):
    return pltpu.CompilerParams(dimension_semantics=("arbitrary",) * n_axes,
                                vmem_limit_bytes=VMEM_LIMIT)


def _resident(shape, layer):
    nd = len(shape)
    return pl.BlockSpec((None,) + shape, lambda *_: (layer,) + (0,) * nd,
                        pipeline_mode=pl.Buffered(1))


def _const(shape):
    nd = len(shape)
    return pl.BlockSpec(shape, lambda *_: (0,) * nd, pipeline_mode=pl.Buffered(1))


class _CastJob:
    def __init__(self, src, layer, col_ranges=None):
        self.src, self.layer = src, layer
        self.rows, self.cols = src.shape[1:]
        self.col_ranges = col_ranges or [(0, self.cols)]

    def specs(self, n_steps):
        rps = self.rows // n_steps
        if rps % SUBLANES_BF16:
            rps = self.rows // (n_steps // 2)
        assert rps % SUBLANES_BF16 == 0 and self.rows % rps == 0
        last = self.rows // rps - 1
        layer = self.layer
        in_spec = pl.BlockSpec((None, rps, self.cols), lambda i: (layer, jnp.minimum(i, last), 0))
        out_specs = [pl.BlockSpec((rps, c1 - c0), lambda i: (jnp.minimum(i, last), 0))
                     for c0, c1 in self.col_ranges]
        out_shapes = [jax.ShapeDtypeStruct((self.rows, c1 - c0), BF16) for c0, c1 in self.col_ranges]
        return in_spec, out_specs, out_shapes


def _run_casts(jobs, src_refs, out_refs):
    out_refs = list(out_refs)
    for job, src_ref in zip(jobs, src_refs):
        x = src_ref[...]
        for c0, c1 in job.col_ranges:
            out_refs.pop(0)[...] = x[:, c0:c1].astype(BF16)


def _cast_specs(jobs, n_steps):
    in_specs, out_specs, out_shapes = [], [], []
    for job in jobs:
        i, o, s = job.specs(n_steps)
        in_specs.append(i)
        out_specs += o
        out_shapes += s
    return in_specs, out_specs, out_shapes


def _cast_body(*refs, jobs):
    _run_casts(jobs, refs[:len(jobs)], refs[len(jobs):])


def _cast(jobs):
    n_steps = 8
    in_specs, out_specs, out_shapes = _cast_specs(jobs, n_steps)
    return pl.pallas_call(
        functools.partial(_cast_body, jobs=jobs),
        out_shape=out_shapes,
        grid=(n_steps,),
        in_specs=in_specs,
        out_specs=out_specs,
        compiler_params=_params(1),
        name="cast_bf16",
    )(*[j.src for j in jobs])


def _ffn_rows(x_ref, o_ref, h_ref, a_scr, g_ref, tg_ref, w_refs, tail, n_rows):
    wg_ref, wu_ref, wd_ref = w_refs
    sub = min(FFN_SUB, n_rows)
    for r0 in range(0, n_rows, sub):
        rows = slice(r0, r0 + sub)
        x = x_ref[rows, :]
        h = _rms(x, g_ref[...]).astype(BF16)
        for c in range(DFF // FC):
            cs = slice(c * FC, (c + 1) * FC)
            gate = _dot(h, wg_ref[:, cs])
            up = _dot(h, wu_ref[:, cs])
            a_scr[rows, cs] = (jax.nn.silu(gate) * up).astype(BF16)
        y = x + 0.5 * _dot(a_scr[rows, :], wd_ref[...])
        if tail == "final":
            y = _rms(y, tg_ref[...])
        o_ref[rows, :] = y
        if tail == "next":
            h_ref[rows, :] = _rms(y, tg_ref[...]).astype(BF16)


def _ffn_body(*refs, tail, jobs, n_main):
    refs = list(refs)
    x_ref, xs_ref, g_ref = refs[:3]
    w_refs = refs[3:6]
    del refs[:6]
    tg_ref = refs.pop(0) if tail else None
    cast_srcs = [refs.pop(0) for _ in jobs]
    o_ref, os_ref = refs.pop(0), refs.pop(0)
    h_ref = refs.pop(0) if tail == "next" else None
    cast_outs = [refs.pop(0) for _ in range(sum(len(j.col_ranges) for j in jobs))]
    a_scr, = refs
    i = pl.program_id(0)

    @pl.when(i < n_main)
    def _():
        _run_casts(jobs, cast_srcs, cast_outs)
        _ffn_rows(x_ref, o_ref, h_ref, a_scr, g_ref, tg_ref, w_refs, tail, x_ref.shape[0])

    @pl.when(i == n_main)
    def _():
        _ffn_rows(xs_ref, os_ref, None, a_scr, g_ref, tg_ref, w_refs,
                  tail if tail == "final" else None, xs_ref.shape[0])


def _ffn(x, xs, gain, layer, weights, final_gain=None, next_gain=None, jobs=()):
    n, ns = x.shape[0], xs.shape[0]
    n_main = n // TM_FFN
    assert n_main * TM_FFN == n and ns <= TM_FFN
    assert final_gain is None or next_gain is None
    tail = "final" if final_gain is not None else "next" if next_gain is not None else None
    jobs = list(jobs)
    row = pl.BlockSpec((TM_FFN, D), lambda i: (jnp.minimum(i, n_main - 1), 0))
    srow = pl.BlockSpec((ns, D), lambda i: (0, 0))
    in_specs = [row, srow, _resident((1, D), layer), _const((D, DFF)), _const((D, DFF)), _const((DFF, D))]
    args = [x, xs, gain, *weights]
    out_specs = [row, srow]
    out_shape = [jax.ShapeDtypeStruct((n, D), F32), jax.ShapeDtypeStruct((ns, D), F32)]
    if tail == "final":
        in_specs.append(_const((1, D)))
        args.append(final_gain)
    elif tail == "next":
        in_specs.append(_resident((1, D), layer))
        args.append(next_gain)
        out_specs.append(row)
        out_shape.append(jax.ShapeDtypeStruct((n, D), BF16))
    cast_in, cast_out, cast_shapes = _cast_specs(jobs, n_main)
    in_specs += cast_in
    args += [j.src for j in jobs]
    n_head = len(out_shape)
    out_specs += cast_out
    out_shape += cast_shapes
    outs = pl.pallas_call(
        functools.partial(_ffn_body, tail=tail, jobs=jobs, n_main=n_main),
        out_shape=out_shape,
        grid=(n_main + 1,),
        in_specs=in_specs,
        out_specs=out_specs,
        scratch_shapes=[pltpu.VMEM((TM_FFN, DFF), BF16)],
        compiler_params=_params(1),
        name="ffn_" + tail if tail else "ffn",
    )(*args)
    return outs[0], (outs[2] if tail == "next" else None), outs[1], list(outs[n_head:])


def _project(h, w_ref, sec, g):
    c0 = sec * ATT + g * GW
    z = _dot(h, w_ref[:, c0:c0 + GW])
    if sec == 0:
        z = z * (HD ** -0.5)
    return z


def _gmlp_inputs(h, w_ref, vn_ref, u_ref, vs_ref):
    u_ref[...] = jax.nn.gelu(_dot(h, w_ref[:, 3 * ATT:3 * ATT + GMW]))
    gv = jax.nn.gelu(_dot(h, w_ref[:, 3 * ATT + GMW:QKVUV]))
    vs_ref[...] = _rms(gv, vn_ref[...]).astype(vs_ref.dtype)


def _inproj_prompt_body(*refs, n_alias):
    refs = list(refs)
    h_ref, w_ref, vn_ref = refs[:3]
    del refs[:3 + n_alias]
    kv1_ref, kv2_ref, kv3_ref = refs[:3]
    q_refs, kvb_refs = refs[3:9:2], refs[4:9:2]
    u_ref, vs_ref, z_scr, t_scr = refs[9:]
    for r0 in range(0, TM_IN, IN_SUB):
        _inproj_rows(h_ref, w_ref, vn_ref, (kv1_ref, kv2_ref, kv3_ref), q_refs, kvb_refs,
                     u_ref, vs_ref, z_scr, t_scr, r0)


def _inproj_rows(h_ref, w_ref, vn_ref, kvt_refs, q_refs, kvb_refs, u_ref, vs_ref, z_scr, t_scr, r0):
    rows = slice(r0, r0 + IN_SUB)
    h = h_ref[rows, :]
    _gmlp_inputs(h, w_ref, vn_ref, u_ref.at[rows, :], vs_ref.at[rows, :])
    for sec in (2, 1, 0):
        for g in reversed(range(NG)):
            z = _project(h, w_ref, sec, g)
            dst = q_refs[g] if sec == 0 else kvb_refs[g]
            cols = slice(GW, 2 * GW) if sec == 2 else slice(0, GW)
            if g == 0:
                dst[rows, cols] = z.astype(BF16)
            else:
                d = DILS[g]
                n = IN_SUB // d
                base = (sec * (NG - 1) + g - 1) * SLABS
                for sl in range(SLABS):
                    z_scr[base + sl, rows, :] = z[:, sl * LANES:(sl + 1) * LANES]
                src, step, first = z_scr, d, lambda r: r0 + r
                if d == DILS[1] ** 2:
                    step = DILS[1]
                    n4 = IN_SUB // step
                    for sl in range(SLABS):
                        for r4 in range(step):
                            t_scr[sec * SLABS + sl, pl.ds(r0 + r4 * n4, n4), :] = (
                                z_scr[base + sl, pl.ds(r0 + r4, n4, stride=step), :])
                    src, base, first = t_scr, sec * SLABS, lambda r: r0 + (r % step) * n4 + r // step
                for r in range(d):
                    sub = [src[base + sl, pl.ds(first(r), n, stride=step), :] for sl in range(SLABS)]
                    dst[r, pl.ds(r0 // d, n), cols] = jnp.concatenate(sub, axis=1).astype(BF16)
            if sec == 0:
                continue
            ch = slice((sec - 1) * GW, sec * GW)
            if g == 2:
                kvt_refs[g][ch, rows] = z.T
            elif g == 1 and r0 >= TM_IN - WINDOWS[1]:
                kvt_refs[g][ch, pl.ds(r0 - (TM_IN - WINDOWS[1]), IN_SUB)] = z.T
            elif g == 0 and r0 + IN_SUB == TM_IN:
                kvt_refs[g][ch, :] = z[IN_SUB - WINDOWS[0]:, :].T


def _inproj_prompt(h, w, vnorm, layer, prev):
    tiles = S_P // TM_IN
    assert TM_IN >= WINDOWS[1] and WINDOWS[1] % IN_SUB == 0
    n_alias = 0 if prev is None else 3
    in_specs = [pl.BlockSpec((TM_IN, D), lambda i: (i, 0)), _const((D, QKVUV)), _resident((1, GMW), layer)]
    in_specs += [pl.BlockSpec(memory_space=pl.ANY)] * n_alias
    seq = lambda i: (layer * B_P + i // tiles, 0, 0)
    row = lambda i: (i, 0)
    res = lambda i: (i // tiles, 0, lax.rem(i, tiles), 0)
    out_shape = [jax.ShapeDtypeStruct((DEPTH * B_P, 2 * GW, w), F32) for w in WINDOWS]
    out_specs = [pl.BlockSpec((None, 2 * GW, WINDOWS[0]), seq),
                 pl.BlockSpec((None, 2 * GW, WINDOWS[1]), seq),
                 pl.BlockSpec((None, 2 * GW, TM_IN), lambda i: (layer * B_P + i // tiles, 0, lax.rem(i, tiles)))]
    for g, d in enumerate(DILS):
        for width in (GW, 2 * GW):
            if g == 0:
                out_shape.append(jax.ShapeDtypeStruct((N_P, width), BF16))
                out_specs.append(pl.BlockSpec((TM_IN, width), row))
            else:
                out_shape.append(jax.ShapeDtypeStruct((B_P, d, S_P // d, width), BF16))
                out_specs.append(pl.BlockSpec((None, d, TM_IN // d, width), res))
    out_shape += [jax.ShapeDtypeStruct((N_P, GMW), F32), jax.ShapeDtypeStruct((N_P, GMW), BF16)]
    out_specs += [pl.BlockSpec((TM_IN, GMW), row), pl.BlockSpec((TM_IN, GMW), row)]
    args = [h, w, vnorm] + ([] if prev is None else list(prev))
    outs = pl.pallas_call(
        functools.partial(_inproj_prompt_body, n_alias=n_alias),
        out_shape=out_shape,
        grid=(N_P // TM_IN,),
        in_specs=in_specs,
        out_specs=out_specs,
        scratch_shapes=[pltpu.VMEM((3 * (NG - 1) * SLABS, TM_IN, LANES), F32),
                        pltpu.VMEM((3 * SLABS, TM_IN, LANES), F32)],
        input_output_aliases={3 + k: k for k in range(n_alias)},
        compiler_params=_params(1),
        name="inproj_prompt",
    )(*args)
    return tuple(outs[:3]), list(outs[3:9]), outs[9], outs[10]


def _inproj_sample_body(*refs, n_alias):
    x_ref, g_ref, w_ref, vn_ref = refs[:4]
    qkv_ref, kv1_ref, kv2_ref, kv3_ref, u_ref, vs_ref = refs[4 + n_alias:]
    kv_refs = (kv1_ref, kv2_ref, kv3_ref)
    h = _rms(x_ref[...], g_ref[...]).astype(BF16)
    for sec in range(3):
        for g in range(NG):
            z = _project(h, w_ref, sec, g)
            qkv_ref[:, sec * ATT + g * GW:sec * ATT + (g + 1) * GW] = z
            if sec > 0:
                kv_refs[g][:, (sec - 1) * GW:sec * GW] = z
    _gmlp_inputs(h, w_ref, vn_ref, u_ref, vs_ref)


def _inproj_sample(x, gain, w, vnorm, layer, prev):
    n_alias = 0 if prev is None else 4
    in_specs = [pl.BlockSpec((N_S, D), lambda i: (0, 0)),
                _resident((1, D), layer), _const((D, QKVUV)), _resident((1, GMW), layer)]
    in_specs += [pl.BlockSpec(memory_space=pl.ANY)] * n_alias
    stacked = jax.ShapeDtypeStruct((DEPTH * N_S, 2 * GW), F32)
    out_shape = [jax.ShapeDtypeStruct((N_S, 3 * ATT), F32), stacked, stacked, stacked,
                 jax.ShapeDtypeStruct((N_S, GMW), F32),
                 jax.ShapeDtypeStruct((DEPTH * N_S, GMW), F32)]
    lay = lambda i: (layer, 0)
    out_specs = [pl.BlockSpec((N_S, 3 * ATT), lambda i: (0, 0)),
                 pl.BlockSpec((N_S, 2 * GW), lay), pl.BlockSpec((N_S, 2 * GW), lay),
                 pl.BlockSpec((N_S, 2 * GW), lay),
                 pl.BlockSpec((N_S, GMW), lambda i: (0, 0)),
                 pl.BlockSpec((N_S, GMW), lay)]
    args = [x, gain, w, vnorm] + ([] if prev is None else list(prev))
    aliases = {4: 1, 5: 2, 6: 3, 7: 5} if prev is not None else {}
    return pl.pallas_call(
        functools.partial(_inproj_sample_body, n_alias=n_alias),
        out_shape=out_shape,
        grid=(1,),
        in_specs=in_specs,
        out_specs=out_specs,
        input_output_aliases=aliases,
        compiler_params=_params(1),
        name="inproj_sample",
    )(*args)


def _alibi_slopes():
    h = np.arange(1, NG * NH + 1, dtype=np.float32)
    return np.power(np.float32(2.0), -8.0 * h / (NG * NH)).astype(np.float32).reshape(NG, NH)


def _head_masks():
    lane = lax.broadcasted_iota(jnp.int32, (1, GW), 1)
    return [(lane >= h * HD) & (lane < (h + 1) * HD) for h in range(NH)]


def _stack_heads(q, masks):
    return jnp.concatenate([jnp.where(m, q, jnp.zeros_like(q)) for m in masks], axis=0)


def _unstack_heads(x, masks, rows):
    out = jnp.zeros((rows, GW), F32)
    for h, m in enumerate(masks):
        out = jnp.where(m, x[h * rows:(h + 1) * rows], out)
    return out


def _softmax_parts(parts, values, value_dots):
    m = functools.reduce(jnp.maximum, [jnp.max(s, axis=-1, keepdims=True) for s in parts])
    ps = [jnp.exp(s - m) for s in parts]
    l = functools.reduce(jnp.add, [jnp.sum(p, axis=-1, keepdims=True) for p in ps])
    o = functools.reduce(jnp.add, [dot(p.astype(BF16), v) for p, v, dot in zip(ps, values, value_dots)])
    return o / l, m + jnp.log(l)


def _merge_groups(outs, lses):
    m = jnp.maximum(jnp.maximum(lses[0], lses[1]), lses[2])
    es = [jnp.exp(l - m) for l in lses]
    tot = es[0] + es[1] + es[2]
    return (es[0] / tot) * outs[0] + (es[1] / tot) * outs[1] + (es[2] / tot) * outs[2]


def _prompt_bias_table():
    slopes = _alibi_slopes()
    a = np.arange(KB)[:, None]
    c = np.arange(2 * KB)[None, :]
    dist = KB + a - c
    valid = (dist >= 0) & (dist <= KB)
    tab = np.empty((NG, NH * KB, 2 * KB), np.float32)
    for g in range(NG):
        for h in range(NH):
            bias = -slopes[g, h] * (DILS[g] * dist).astype(np.float32)
            tab[g, h * KB:(h + 1) * KB] = np.where(valid, bias, np.float32(NEG))
    return tab


def _sample_bias_tables():
    slopes = _alibi_slopes()
    j = np.arange(SPAD)[:, None]
    tabs = []
    for g in range(NG):
        w, d = WINDOWS[g], DILS[g]
        c = np.arange(w)[None, :]
        cn = np.arange(KB)[None, :]
        dist = np.concatenate([np.broadcast_to(w + j - c, (SPAD, w)),
                               np.broadcast_to(j - cn, (SPAD, KB))], axis=1)
        real = np.concatenate([np.ones((SPAD, w), bool), np.broadcast_to(cn < T_S, (SPAD, KB))], axis=1)
        valid = (j < T_S) & real & (dist >= 0) & (dist <= w) & (dist % d == 0)
        pad_row = np.broadcast_to(j >= T_S, dist.shape)
        tab = np.empty((NH * SPAD, dist.shape[1]), np.float32)
        for h in range(NH):
            bias = -slopes[g, h] * dist.astype(np.float32)
            blk = np.where(valid, bias, np.float32(NEG))
            tab[h * SPAD:(h + 1) * SPAD] = np.where(pad_row, np.float32(0.0), blk)
        tabs.append(tab)
    return tabs


def _attn_sample_body(z_ref, c1_ref, c2_ref, c3_ref, b1_ref, b2_ref, b3_ref, att_ref):
    for s in range(z_ref.shape[0]):
        _attn_sample_one(z_ref.at[s], (c1_ref.at[s], c2_ref.at[s], c3_ref.at[s]),
                         (b1_ref, b2_ref, b3_ref), att_ref.at[s])


def _attn_sample_one(z_ref, c_refs, b_refs, att_ref):
    masks = _head_masks()
    z = z_ref[...]
    pad = jnp.zeros((KB - SPAD, GW), F32)
    outs, lses = [], []
    for g, (c_ref, b_ref) in enumerate(zip(c_refs, b_refs)):
        w = WINDOWS[g]
        q = z[:, g * GW:(g + 1) * GW]
        k_new = jnp.concatenate([z[:, ATT + g * GW:ATT + (g + 1) * GW], pad], axis=0).astype(BF16)
        v_new = jnp.concatenate([z[:, 2 * ATT + g * GW:2 * ATT + (g + 1) * GW], pad], axis=0).astype(BF16)
        k_old = c_ref[:GW, :].astype(BF16)
        v_old = c_ref[GW:, :].astype(BF16)
        qs = _stack_heads(q, masks).astype(BF16)
        parts = [_dot(qs, k_old) + b_ref[:, :w], _dot_nt(qs, k_new) + b_ref[:, w:]]
        o, lse = _softmax_parts(parts, [v_old, v_new], [_dot_nt, _dot])
        outs.append(_unstack_heads(o, masks, SPAD))
        lses.append(_unstack_heads(lse, masks, SPAD))
    att_ref[...] = _merge_groups(outs, lses)


def _attn_sample_operands(z, caches, tabs, layer, n_steps, step=lambda i: i):
    per_step = B_S // n_steps
    assert per_step * n_steps == B_S
    zp = jnp.pad(z.reshape(B_S, T_S, 3 * ATT), ((0, 0), (0, SPAD - T_S), (0, 0)))
    cs = [c.transpose(0, 1, 3, 4, 5, 2).reshape(DEPTH, B_S, 2 * GW, w) for c, w in zip(caches, WINDOWS)]
    in_specs = [pl.BlockSpec((per_step, SPAD, 3 * ATT), lambda *ids: (step(*ids), 0, 0))]
    in_specs += [pl.BlockSpec((None, per_step, 2 * GW, w), lambda *ids: (layer, step(*ids), 0, 0)) for w in WINDOWS]
    in_specs += [_const(t.shape) for t in tabs]
    out_spec = pl.BlockSpec((per_step, SPAD, GW), lambda *ids: (step(*ids), 0, 0))
    return [zp, *cs, *tabs], in_specs, out_spec, jax.ShapeDtypeStruct((B_S, SPAD, GW), F32)


def _mix_out(x, h, att, spat, wgate_ref, pa_ref, ps_ref, wo_ref):
    ga = jax.nn.sigmoid(_dot(h, wgate_ref[:, :D]))
    gb = jax.nn.sigmoid(_dot(h, wgate_ref[:, D:]))
    mix = ga * _dot(att.astype(BF16), pa_ref[...]) + gb * _dot(spat, ps_ref[...])
    return x + _dot(mix.astype(BF16), wo_ref[...])


def _mixer_out_body(x_ref, h_ref, wgate_ref, q1_ref, kv1_ref, kv1p_ref, q2_ref, kv2_ref, kv2p_ref, q3_ref, kv3_ref,
                    bias_ref, u_ref, vs_ref, ws_ref, b_ref, pa_ref, ps_ref, wo_ref, *rest):
    if len(rest) > 4:
        _attn_sample_body(*rest[:7], rest[8])
        out_ref, o_scr, l_scr, spat_scr = rest[7], *rest[9:]
    else:
        out_ref, o_scr, l_scr, spat_scr = rest
    j = pl.program_id(1)
    masks = _head_masks()
    lane = lax.broadcasted_iota(jnp.int32, (1, 2 * KB), 1)
    first_tile_mask = (j > 0) | (lane >= KB)

    def wide(ref, base, rows):
        return jnp.concatenate([ref[base + sl, rows, :] for sl in range(SLABS)], axis=1)

    def band(g, q, kv, bias, rows, n):
        qs = _stack_heads(q, masks)
        s = _dot_nt(qs, kv[:, :GW]) + bias
        m = jnp.max(s, axis=-1, keepdims=True)
        p = jnp.exp(s - m)
        l = jnp.sum(p, axis=-1, keepdims=True)
        o = _unstack_heads(_dot(p.astype(BF16), kv[:, GW:]) / l, masks, n)
        lse = _unstack_heads(m + jnp.log(l), masks, n)
        for sl in range(SLABS):
            o_scr[g * SLABS + sl, rows, :] = o[:, sl * LANES:(sl + 1) * LANES]
            l_scr[g * SLABS + sl, rows, :] = lse[:, sl * LANES:(sl + 1) * LANES]

    bias1 = bias_ref[0]
    for blk in range(TM // KB):
        own = pl.ds(blk * KB, KB)
        if blk == 0:
            kv = jnp.concatenate([kv1p_ref[...], kv1_ref[own, :]], axis=0)
            bias = jnp.where(first_tile_mask, bias1, NEG)
        else:
            kv, bias = kv1_ref[pl.ds((blk - 1) * KB, 2 * KB), :], bias1
        band(0, q1_ref[own, :], kv, bias, own, KB)

    bias2 = jnp.where(first_tile_mask, bias_ref[1], NEG)
    for r in range(DILS[1]):
        kv = jnp.concatenate([kv2p_ref[r], kv2_ref[r]], axis=0)
        band(1, q2_ref[r], kv, bias2, pl.ds(r, KB, stride=DILS[1]), KB)

    nq = TM // DILS[2]
    q0 = pl.multiple_of(j * nq, nq)
    bias3 = jnp.concatenate([bias_ref[2, pl.ds(h * KB + q0, nq), KB:] for h in range(NH)], axis=0)
    for r in range(DILS[2]):
        band(2, q3_ref[r], kv3_ref[r], bias3, pl.ds(r, nq, stride=DILS[2]), nq)

    full = slice(None)
    att = _merge_groups([wide(o_scr, g * SLABS, full) for g in range(NG)],
                        [wide(l_scr, g * SLABS, full) for g in range(NG)])

    x = x_ref[...]
    h = h_ref[...]
    row = lax.broadcasted_iota(jnp.int32, (CHUNK, CHUNK), 0)
    col = lax.broadcasted_iota(jnp.int32, (CHUNK, CHUNK), 1)
    for g in range(GG):
        w = jnp.where(col <= row, ws_ref[g], 0.0).astype(BF16)
        gs = slice(g * GC, (g + 1) * GC)
        for c in range(TM // CHUNK):
            rs = slice(c * CHUNK, (c + 1) * CHUNK)
            zc = _dot(w, vs_ref[rs, gs]) + b_ref[g]
            spat_scr[rs, gs] = (u_ref[rs, gs] * zc).astype(BF16)
    out_ref[...] = _mix_out(x, h, att, spat_scr[...], wgate_ref, pa_ref, ps_ref, wo_ref)


def _mixer_out(x, h, wgate, attn_ops, bias_tab, u, vs, ws, bias, pa, ps, wo, layer, attn=None):
    tiles = S_P // TM
    assert TM == KB * DILS[1] and S_P == KB * DILS[2]
    q1, kv1, q2, kv2, q3, kv3 = attn_ops
    row = lambda b, j: (b * tiles + j, 0)
    res = lambda b, j: (b, 0, j, 0)
    res_before = lambda b, j: (b, 0, jnp.maximum(j - 1, 0), 0)
    row_before = lambda b, j: (jnp.maximum((b * tiles + j) * (TM // KB) - 1, 0), 0)
    d2, d3 = DILS[1], DILS[2]
    in_specs = [pl.BlockSpec((TM, D), row), pl.BlockSpec((TM, D), row), _const((D, 2 * D)),
                pl.BlockSpec((TM, GW), row), pl.BlockSpec((TM, 2 * GW), row),
                pl.BlockSpec((KB, 2 * GW), row_before),
                pl.BlockSpec((None, d2, TM // d2, GW), res), pl.BlockSpec((None, d2, TM // d2, 2 * GW), res),
                pl.BlockSpec((None, d2, TM // d2, 2 * GW), res_before),
                pl.BlockSpec((None, d3, TM // d3, GW), res),
                pl.BlockSpec((None, d3, S_P // d3, 2 * GW), lambda b, j: (b, 0, 0, 0)),
                _const(bias_tab.shape),
                pl.BlockSpec((TM, GMW), row), pl.BlockSpec((TM, GMW), row),
                _resident((GG, CHUNK, CHUNK), layer), _resident((GG, CHUNK, 1), layer),
                _const((GW, D)), _const((GMW, D)), _const((D, D))]
    args = [x, h, wgate, q1, kv1, kv1, q2, kv2, kv2, q3, kv3, bias_tab, u, vs, ws, bias, pa, ps, wo]
    out_specs, out_shape = [pl.BlockSpec((TM, D), row)], [jax.ShapeDtypeStruct((N_P, D), F32)]
    if attn is not None:
        attn_args, attn_in_specs, attn_out_spec, attn_out_shape = attn(B_P * tiles, lambda b, j: b * tiles + j)
        in_specs += attn_in_specs
        args += attn_args
        out_specs.append(attn_out_spec)
        out_shape.append(attn_out_shape)
    outs = pl.pallas_call(
        _mixer_out_body,
        out_shape=out_shape,
        grid=(B_P, tiles),
        in_specs=in_specs,
        out_specs=out_specs,
        scratch_shapes=[pltpu.VMEM((NG * SLABS, TM, LANES), F32), pltpu.VMEM((NG * SLABS, TM, LANES), F32),
                        pltpu.VMEM((TM, GMW), BF16)],
        compiler_params=_params(2),
        name="mixer_out",
    )(*args)
    return outs[0], (outs[1] if attn is not None else None)


def _outproj_sample_body(x_ref, g_ref, wgate_ref, att_ref, u_ref, vs_ref, wt_ref, b_ref,
                         pa_ref, ps_ref, wo_ref, out_ref):
    x = x_ref[...]
    h = _rms(x, g_ref[...]).astype(BF16)
    row = lax.broadcasted_iota(jnp.int32, (N_S, N_S), 0)
    col = lax.broadcasted_iota(jnp.int32, (N_S, N_S), 1)
    keep = (row // T_S == col // T_S) & (col <= row)
    u = u_ref[...]
    vs = vs_ref[...].astype(BF16)
    parts = []
    for g in range(GG):
        corner = wt_ref[g]
        w = jnp.zeros((N_S, N_S), F32)
        for t in range(T_S):
            w = jnp.where(keep & (lax.rem(row, T_S) == t), corner[t:t + 1, :], w)
        w = w.astype(BF16)
        gs = slice(g * GC, (g + 1) * GC)
        parts.append(u[:, gs] * (_dot(w, vs[:, gs]) + b_ref[g]))
    spat = jnp.concatenate(parts, axis=1).astype(BF16)
    out_ref[...] = _mix_out(x, h, att_ref[...], spat, wgate_ref, pa_ref, ps_ref, wo_ref)


def _outproj_sample(x, gain, wgate, att, u, vs_all, wt, bias, pa, ps, wo, layer):
    zero = lambda i: (0, 0)
    in_specs = [pl.BlockSpec((N_S, D), zero), _resident((1, D), layer), _const((D, 2 * D)),
                pl.BlockSpec((N_S, GW), zero), pl.BlockSpec((N_S, GMW), zero),
                pl.BlockSpec((N_S, GMW), lambda i: (layer, 0)),
                _resident((GG, T_S, N_S), layer), _resident((GG, N_S, 1), layer),
                _const((GW, D)), _const((GMW, D)), _const((D, D))]
    return pl.pallas_call(
        _outproj_sample_body,
        out_shape=jax.ShapeDtypeStruct((N_S, D), F32),
        grid=(1,),
        in_specs=in_specs,
        out_specs=pl.BlockSpec((N_S, D), zero),
        compiler_params=_params(1),
        name="outproj_sample",
    )(x, gain, wgate, att, u, vs_all, wt, bias, pa, ps, wo)


def kernel(x_prompt, x_sample, cache_kv_w128, cache_kv_w512, cache_kv_w2048, ffn1_norm, ffn1_gate, ffn1_up, ffn1_down, mix_norm, w_in, gmlp_v_norm, gmlp_ws, gmlp_bias, proj_att, proj_spatial, w_out, ffn2_norm, ffn2_gate, ffn2_up, ffn2_down, final_norm):
    caches = (cache_kv_w128, cache_kv_w512, cache_kv_w2048)
    xp = x_prompt.reshape(N_P, D)
    xs = x_sample.reshape(N_S, D)

    ffn1_jobs = lambda l: [_CastJob(ffn1_gate, l), _CastJob(ffn1_up, l), _CastJob(ffn1_down, l)]
    ffn2_jobs = lambda l: [_CastJob(ffn2_gate, l), _CastJob(ffn2_up, l), _CastJob(ffn2_down, l)]
    mix_jobs = lambda l: [_CastJob(w_in, l, [(0, QKVUV), (QKVUV, QKVUV + 2 * D)]),
                          _CastJob(proj_att, l), _CastJob(proj_spatial, l), _CastJob(w_out, l)]

    n1 = ffn1_norm.reshape(DEPTH, 1, D)
    n2 = ffn2_norm.reshape(DEPTH, 1, D)
    nm = mix_norm.reshape(DEPTH, 1, D)
    nv = gmlp_v_norm.reshape(DEPTH, 1, GMW)
    nf = final_norm.reshape(1, D)
    bias_p = gmlp_bias.reshape(DEPTH, GG, CHUNK, 1)
    wt_s = jnp.tile(gmlp_ws[:, :, :T_S, :T_S], (1, 1, 1, B_S))
    bias_s = jnp.tile(gmlp_bias[:, :, :T_S], (1, 1, B_S)).reshape(DEPTH, GG, N_S, 1)

    attn_bias = jnp.asarray(_prompt_bias_table())
    sample_tabs = [jnp.asarray(t) for t in _sample_bias_tables()]

    f1 = {0: _cast(ffn1_jobs(0))}
    f2, mix = {}, {}
    kv_p = kv_s = None
    for l in range(DEPTH):
        last = l == DEPTH - 1
        fin = nf if last else None
        xp, hp, xs, outs = _ffn(xp, xs, n1, l, f1[l], next_gain=nm,
                                jobs=ffn2_jobs(l) + (mix_jobs(l) if l == 0 else []))
        f2[l] = outs[:3]
        if l == 0:
            mix[l] = outs[3:]
        z_s, ks1, ks2, ks3, u_s, vs_s = _inproj_sample(xs, nm, mix[l][0], nv, l, kv_s)
        kv_s = (ks1, ks2, ks3, vs_s)
        attn = functools.partial(_attn_sample_operands, z_s, caches, sample_tabs, l)
        kv_p, attn_ops, u, vs = _inproj_prompt(hp, mix[l][0], nv, l, kv_p)
        xp, att = _mixer_out(xp, hp, mix[l][1], attn_ops, attn_bias, u, vs, gmlp_ws, bias_p, *mix[l][2:], l,
                             attn=attn)
        att_s = att[:, :T_S].reshape(N_S, GW)
        xs = _outproj_sample(xs, nm, mix[l][1], att_s, u_s, vs_s, wt_s, bias_s, *mix[l][2:], l)
        xp, _, xs, outs = _ffn(xp, xs, n2, l, f2[l], fin, jobs=[] if last else ffn1_jobs(l + 1) + mix_jobs(l + 1))
        if not last:
            f1[l + 1], mix[l + 1] = outs[:3], outs[3:]

    kv_out = [k.reshape(DEPTH, B_P, 2, NH, HD, w).transpose(0, 1, 5, 2, 3, 4) for k, w in zip(kv_p, WINDOWS)]
    ks_shape = (DEPTH, B_S, T_S, 2, NH, HD)
    return (xp.reshape(B_P, S_P, D), xs.reshape(B_S, T_S, D), kv_out[0], kv_out[1], kv_out[2],
            kv_s[0].reshape(ks_shape), kv_s[1].reshape(ks_shape), kv_s[2].reshape(ks_shape),
            kv_s[3].reshape(DEPTH, B_S, T_S, GMW))
```

```python
import functools

import numpy as np
import jax
import jax.numpy as jnp
from jax import lax
from jax.experimental import pallas as pl
from jax.experimental.pallas import tpu as pltpu

F32 = jnp.float32
BF16 = jnp.bfloat16

D = 1024
DEPTH = 2
B_P, S_P = 8, 2048
B_S, T_S = 32, 4
N_P = B_P * S_P
N_S = B_S * T_S
HD, NH, NG = 64, 4, 3
GW = NH * HD
ATT = NG * GW
WINDOWS = (128, 512, 2048)
DILS = (1, 4, 16)
KB = 128
CHUNK = 128
GG, GC = 4, 128
GMW = GG * GC
DFF = 2816
FC = 256
QKVUV = 3 * ATT + 2 * GMW
EPS = 1e-6
NEG = -1e30
TM = 512
TM_FFN = 1024
TM_IN = 1024
FFN_SUB = 256
IN_SUB = 256
SPAD = 8
LANES = 128
SUBLANES_BF16 = 16
SLABS = GW // LANES
VMEM_LIMIT = 60 * 1024 * 1024

_NT = (((1,), (1,)), ((), ()))


def _rms(x, gain):
    r = lax.rsqrt(jnp.mean(x * x, axis=-1, keepdims=True) + EPS)
    return (x * r) * gain


def _dot(a, b):
    return jnp.dot(a, b, preferred_element_type=F32)


def _dot_nt(a, b):
    return lax.dot_general(a, b, _NT, preferred_element_type=F32)


def _params(n_axes):
    return pltpu.CompilerParams(dimension_semantics=("arbitrary",) * n_axes,
                                vmem_limit_bytes=VMEM_LIMIT)


def _resident(shape, layer):
    nd = len(shape)
    return pl.BlockSpec((None,) + shape, lambda *_: (layer,) + (0,) * nd,
                        pipeline_mode=pl.Buffered(1))


def _const(shape):
    nd = len(shape)
    return pl.BlockSpec(shape, lambda *_: (0,) * nd, pipeline_mode=pl.Buffered(1))


class _CastJob:
    def __init__(self, src, layer, col_ranges=None):
        self.src, self.layer = src, layer
        self.rows, self.cols = src.shape[1:]
        self.col_ranges = col_ranges or [(0, self.cols)]

    def specs(self, n_steps):
        rps = self.rows // n_steps
        if rps % SUBLANES_BF16:
            rps = self.rows // (n_steps // 2)
        assert rps % SUBLANES_BF16 == 0 and self.rows % rps == 0
        last = self.rows // rps - 1
        layer = self.layer
        in_spec = pl.BlockSpec((None, rps, self.cols), lambda i: (layer, jnp.minimum(i, last), 0))
        out_specs = [pl.BlockSpec((rps, c1 - c0), lambda i: (jnp.minimum(i, last), 0))
                     for c0, c1 in self.col_ranges]
        out_shapes = [jax.ShapeDtypeStruct((self.rows, c1 - c0), BF16) for c0, c1 in self.col_ranges]
        return in_spec, out_specs, out_shapes


def _run_casts(jobs, src_refs, out_refs):
    out_refs = list(out_refs)
    for job, src_ref in zip(jobs, src_refs):
        x = src_ref[...]
        for c0, c1 in job.col_ranges:
            out_refs.pop(0)[...] = x[:, c0:c1].astype(BF16)


def _cast_specs(jobs, n_steps):
    in_specs, out_specs, out_shapes = [], [], []
    for job in jobs:
        i, o, s = job.specs(n_steps)
        in_specs.append(i)
        out_specs += o
        out_shapes += s
    return in_specs, out_specs, out_shapes


def _cast_body(*refs, jobs):
    _run_casts(jobs, refs[:len(jobs)], refs[len(jobs):])


def _cast(jobs):
    n_steps = 8
    in_specs, out_specs, out_shapes = _cast_specs(jobs, n_steps)
    return pl.pallas_call(
        functools.partial(_cast_body, jobs=jobs),
        out_shape=out_shapes,
        grid=(n_steps,),
        in_specs=in_specs,
        out_specs=out_specs,
        compiler_params=_params(1),
        name="cast_bf16",
    )(*[j.src for j in jobs])


def _ffn_rows(x_ref, o_ref, h_ref, a_scr, g_ref, tg_ref, w_refs, tail, n_rows):
    wg_ref, wu_ref, wd_ref = w_refs
    sub = min(FFN_SUB, n_rows)
    for r0 in range(0, n_rows, sub):
        rows = slice(r0, r0 + sub)
        x = x_ref[rows, :]
        h = _rms(x, g_ref[...]).astype(BF16)
        for c in range(DFF // FC):
            cs = slice(c * FC, (c + 1) * FC)
            gate = _dot(h, wg_ref[:, cs])
            up = _dot(h, wu_ref[:, cs])
            a_scr[rows, cs] = (jax.nn.silu(gate) * up).astype(BF16)
        y = x + 0.5 * _dot(a_scr[rows, :], wd_ref[...])
        if tail == "final":
            y = _rms(y, tg_ref[...])
        o_ref[rows, :] = y
        if tail == "next":
            h_ref[rows, :] = _rms(y, tg_ref[...]).astype(BF16)


def _ffn_body(*refs, tail, jobs, n_main):
    refs = list(refs)
    x_ref, xs_ref, g_ref = refs[:3]
    w_refs = refs[3:6]
    del refs[:6]
    tg_ref = refs.pop(0) if tail else None
    cast_srcs = [refs.pop(0) for _ in jobs]
    o_ref, os_ref = refs.pop(0), refs.pop(0)
    h_ref = refs.pop(0) if tail == "next" else None
    cast_outs = [refs.pop(0) for _ in range(sum(len(j.col_ranges) for j in jobs))]
    a_scr, = refs
    i = pl.program_id(0)

    @pl.when(i < n_main)
    def _():
        _run_casts(jobs, cast_srcs, cast_outs)
        _ffn_rows(x_ref, o_ref, h_ref, a_scr, g_ref, tg_ref, w_refs, tail, x_ref.shape[0])

    @pl.when(i == n_main)
    def _():
        _ffn_rows(xs_ref, os_ref, None, a_scr, g_ref, tg_ref, w_refs,
                  tail if tail == "final" else None, xs_ref.shape[0])


def _ffn(x, xs, gain, layer, weights, final_gain=None, next_gain=None, jobs=()):
    n, ns = x.shape[0], xs.shape[0]
    n_main = n // TM_FFN
    assert n_main * TM_FFN == n and ns <= TM_FFN
    assert final_gain is None or next_gain is None
    tail = "final" if final_gain is not None else "next" if next_gain is not None else None
    jobs = list(jobs)
    row = pl.BlockSpec((TM_FFN, D), lambda i: (jnp.minimum(i, n_main - 1), 0))
    srow = pl.BlockSpec((ns, D), lambda i: (0, 0))
    in_specs = [row, srow, _resident((1, D), layer), _const((D, DFF)), _const((D, DFF)), _const((DFF, D))]
    args = [x, xs, gain, *weights]
    out_specs = [row, srow]
    out_shape = [jax.ShapeDtypeStruct((n, D), F32), jax.ShapeDtypeStruct((ns, D), F32)]
    if tail == "final":
        in_specs.append(_const((1, D)))
        args.append(final_gain)
    elif tail == "next":
        in_specs.append(_resident((1, D), layer))
        args.append(next_gain)
        out_specs.append(row)
        out_shape.append(jax.ShapeDtypeStruct((n, D), BF16))
    cast_in, cast_out, cast_shapes = _cast_specs(jobs, n_main)
    in_specs += cast_in
    args += [j.src for j in jobs]
    n_head = len(out_shape)
    out_specs += cast_out
    out_shape += cast_shapes
    outs = pl.pallas_call(
        functools.partial(_ffn_body, tail=tail, jobs=jobs, n_main=n_main),
        out_shape=out_shape,
        grid=(n_main + 1,),
        in_specs=in_specs,
        out_specs=out_specs,
        scratch_shapes=[pltpu.VMEM((TM_FFN, DFF), BF16)],
        compiler_params=_params(1),
        name="ffn_" + tail if tail else "ffn",
    )(*args)
    return outs[0], (outs[2] if tail == "next" else None), outs[1], list(outs[n_head:])


def _project(h, w_ref, sec, g):
    c0 = sec * ATT + g * GW
    z = _dot(h, w_ref[:, c0:c0 + GW])
    if sec == 0:
        z = z * (HD ** -0.5)
    return z


def _gmlp_inputs(h, w_ref, vn_ref, u_ref, vs_ref):
    u_ref[...] = jax.nn.gelu(_dot(h, w_ref[:, 3 * ATT:3 * ATT + GMW]))
    gv = jax.nn.gelu(_dot(h, w_ref[:, 3 * ATT + GMW:QKVUV]))
    vs_ref[...] = _rms(gv, vn_ref[...]).astype(vs_ref.dtype)


def _inproj_prompt_body(*refs, n_alias):
    refs = list(refs)
    h_ref, w_ref, vn_ref = refs[:3]
    del refs[:3 + n_alias]
    kv1_ref, kv2_ref, kv3_ref = refs[:3]
    q_refs, kvb_refs = refs[3:9:2], refs[4:9:2]
    u_ref, vs_ref, z_scr, t_scr = refs[9:]
    for r0 in range(0, TM_IN, IN_SUB):
        _inproj_rows(h_ref, w_ref, vn_ref, (kv1_ref, kv2_ref, kv3_ref), q_refs, kvb_refs,
                     u_ref, vs_ref, z_scr, t_scr, r0)


def _inproj_rows(h_ref, w_ref, vn_ref, kvt_refs, q_refs, kvb_refs, u_ref, vs_ref, z_scr, t_scr, r0):
    rows = slice(r0, r0 + IN_SUB)
    h = h_ref[rows, :]
    _gmlp_inputs(h, w_ref, vn_ref, u_ref.at[rows, :], vs_ref.at[rows, :])
    for sec in (2, 1, 0):
        for g in reversed(range(NG)):
            z = _project(h, w_ref, sec, g)
            dst = q_refs[g] if sec == 0 else kvb_refs[g]
            cols = slice(GW, 2 * GW) if sec == 2 else slice(0, GW)
            if g == 0:
                dst[rows, cols] = z.astype(BF16)
            else:
                d = DILS[g]
                n = IN_SUB // d
                base = (sec * (NG - 1) + g - 1) * SLABS
                for sl in range(SLABS):
                    z_scr[base + sl, rows, :] = z[:, sl * LANES:(sl + 1) * LANES]
                src, step, first = z_scr, d, lambda r: r0 + r
                if d == DILS[1] ** 2:
                    step = DILS[1]
                    n4 = IN_SUB // step
                    for sl in range(SLABS):
                        for r4 in range(step):
                            t_scr[sec * SLABS + sl, pl.ds(r0 + r4 * n4, n4), :] = (
                                z_scr[base + sl, pl.ds(r0 + r4, n4, stride=step), :])
                    src, base, first = t_scr, sec * SLABS, lambda r: r0 + (r % step) * n4 + r // step
                for r in range(d):
                    sub = [src[base + sl, pl.ds(first(r), n, stride=step), :] for sl in range(SLABS)]
                    dst[r, pl.ds(r0 // d, n), cols] = jnp.concatenate(sub, axis=1).astype(BF16)
            if sec == 0:
                continue
            ch = slice((sec - 1) * GW, sec * GW)
            if g == 2:
                kvt_refs[g][ch, rows] = z.T
            elif g == 1 and r0 >= TM_IN - WINDOWS[1]:
                kvt_refs[g][ch, pl.ds(r0 - (TM_IN - WINDOWS[1]), IN_SUB)] = z.T
            elif g == 0 and r0 + IN_SUB == TM_IN:
                kvt_refs[g][ch, :] = z[IN_SUB - WINDOWS[0]:, :].T


def _inproj_prompt(h, w, vnorm, layer, prev):
    tiles = S_P // TM_IN
    assert TM_IN >= WINDOWS[1] and WINDOWS[1] % IN_SUB == 0
    n_alias = 0 if prev is None else 3
    in_specs = [pl.BlockSpec((TM_IN, D), lambda i: (i, 0)), _const((D, QKVUV)), _resident((1, GMW), layer)]
    in_specs += [pl.BlockSpec(memory_space=pl.ANY)] * n_alias
    seq = lambda i: (layer * B_P + i // tiles, 0, 0)
    row = lambda i: (i, 0)
    res = lambda i: (i // tiles, 0, lax.rem(i, tiles), 0)
    out_shape = [jax.ShapeDtypeStruct((DEPTH * B_P, 2 * GW, w), F32) for w in WINDOWS]
    out_specs = [pl.BlockSpec((None, 2 * GW, WINDOWS[0]), seq),
                 pl.BlockSpec((None, 2 * GW, WINDOWS[1]), seq),
                 pl.BlockSpec((None, 2 * GW, TM_IN), lambda i: (layer * B_P + i // tiles, 0, lax.rem(i, tiles)))]
    for g, d in enumerate(DILS):
        for width in (GW, 2 * GW):
            if g == 0:
                out_shape.append(jax.ShapeDtypeStruct((N_P, width), BF16))
                out_specs.append(pl.BlockSpec((TM_IN, width), row))
            else:
                out_shape.append(jax.ShapeDtypeStruct((B_P, d, S_P // d, width), BF16))
                out_specs.append(pl.BlockSpec((None, d, TM_IN // d, width), res))
    out_shape += [jax.ShapeDtypeStruct((N_P, GMW), F32), jax.ShapeDtypeStruct((N_P, GMW), BF16)]
    out_specs += [pl.BlockSpec((TM_IN, GMW), row), pl.BlockSpec((TM_IN, GMW), row)]
    args = [h, w, vnorm] + ([] if prev is None else list(prev))
    outs = pl.pallas_call(
        functools.partial(_inproj_prompt_body, n_alias=n_alias),
        out_shape=out_shape,
        grid=(N_P // TM_IN,),
        in_specs=in_specs,
        out_specs=out_specs,
        scratch_shapes=[pltpu.VMEM((3 * (NG - 1) * SLABS, TM_IN, LANES), F32),
                        pltpu.VMEM((3 * SLABS, TM_IN, LANES), F32)],
        input_output_aliases={3 + k: k for k in range(n_alias)},
        compiler_params=_params(1),
        name="inproj_prompt",
    )(*args)
    return tuple(outs[:3]), list(outs[3:9]), outs[9], outs[10]


def _inproj_sample_body(*refs, n_alias):
    x_ref, g_ref, w_ref, vn_ref = refs[:4]
    qkv_ref, kv1_ref, kv2_ref, kv3_ref, u_ref, vs_ref = refs[4 + n_alias:]
    kv_refs = (kv1_ref, kv2_ref, kv3_ref)
    h = _rms(x_ref[...], g_ref[...]).astype(BF16)
    for sec in range(3):
        for g in range(NG):
            z = _project(h, w_ref, sec, g)
            qkv_ref[:, sec * ATT + g * GW:sec * ATT + (g + 1) * GW] = z
            if sec > 0:
                kv_refs[g][:, (sec - 1) * GW:sec * GW] = z
    _gmlp_inputs(h, w_ref, vn_ref, u_ref, vs_ref)


def _inproj_sample(x, gain, w, vnorm, layer, prev):
    n_alias = 0 if prev is None else 4
    in_specs = [pl.BlockSpec((N_S, D), lambda i: (0, 0)),
                _resident((1, D), layer), _const((D, QKVUV)), _resident((1, GMW), layer)]
    in_specs += [pl.BlockSpec(memory_space=pl.ANY)] * n_alias
    stacked = jax.ShapeDtypeStruct((DEPTH * N_S, 2 * GW), F32)
    out_shape = [jax.ShapeDtypeStruct((N_S, 3 * ATT), F32), stacked, stacked, stacked,
                 jax.ShapeDtypeStruct((N_S, GMW), F32),
                 jax.ShapeDtypeStruct((DEPTH * N_S, GMW), F32)]
    lay = lambda i: (layer, 0)
    out_specs = [pl.BlockSpec((N_S, 3 * ATT), lambda i: (0, 0)),
                 pl.BlockSpec((N_S, 2 * GW), lay), pl.BlockSpec((N_S, 2 * GW), lay),
                 pl.BlockSpec((N_S, 2 * GW), lay),
                 pl.BlockSpec((N_S, GMW), lambda i: (0, 0)),
                 pl.BlockSpec((N_S, GMW), lay)]
    args = [x, gain, w, vnorm] + ([] if prev is None else list(prev))
    aliases = {4: 1, 5: 2, 6: 3, 7: 5} if prev is not None else {}
    return pl.pallas_call(
        functools.partial(_inproj_sample_body, n_alias=n_alias),
        out_shape=out_shape,
        grid=(1,),
        in_specs=in_specs,
        out_specs=out_specs,
        input_output_aliases=aliases,
        compiler_params=_params(1),
        name="inproj_sample",
    )(*args)


def _alibi_slopes():
    h = np.arange(1, NG * NH + 1, dtype=np.float32)
    return np.power(np.float32(2.0), -8.0 * h / (NG * NH)).astype(np.float32).reshape(NG, NH)


def _head_masks():
    lane = lax.broadcasted_iota(jnp.int32, (1, GW), 1)
    return [(lane >= h * HD) & (lane < (h + 1) * HD) for h in range(NH)]


def _stack_heads(q, masks):
    return jnp.concatenate([jnp.where(m, q, jnp.zeros_like(q)) for m in masks], axis=0)


def _unstack_heads(x, masks, rows):
    out = jnp.zeros((rows, GW), F32)
    for h, m in enumerate(masks):
        out = jnp.where(m, x[h * rows:(h + 1) * rows], out)
    return out


def _softmax_parts(parts, values, value_dots):
    m = functools.reduce(jnp.maximum, [jnp.max(s, axis=-1, keepdims=True) for s in parts])
    ps = [jnp.exp(s - m) for s in parts]
    l = functools.reduce(jnp.add, [jnp.sum(p, axis=-1, keepdims=True) for p in ps])
    o = functools.reduce(jnp.add, [dot(p.astype(BF16), v) for p, v, dot in zip(ps, values, value_dots)])
    return o / l, m + jnp.log(l)


def _merge_groups(outs, lses):
    m = jnp.maximum(jnp.maximum(lses[0], lses[1]), lses[2])
    es = [jnp.exp(l - m) for l in lses]
    tot = es[0] + es[1] + es[2]
    return (es[0] / tot) * outs[0] + (es[1] / tot) * outs[1] + (es[2] / tot) * outs[2]


def _prompt_bias_table():
    slopes = _alibi_slopes()
    a = np.arange(KB)[:, None]
    c = np.arange(2 * KB)[None, :]
    dist = KB + a - c
    valid = (dist >= 0) & (dist <= KB)
    tab = np.empty((NG, NH * KB, 2 * KB), np.float32)
    for g in range(NG):
        for h in range(NH):
            bias = -slopes[g, h] * (DILS[g] * dist).astype(np.float32)
            tab[g, h * KB:(h + 1) * KB] = np.where(valid, bias, np.float32(NEG))
    return tab


def _sample_bias_tables():
    slopes = _alibi_slopes()
    j = np.arange(SPAD)[:, None]
    tabs = []
    for g in range(NG):
        w, d = WINDOWS[g], DILS[g]
        c = np.arange(w)[None, :]
        cn = np.arange(KB)[None, :]
        dist = np.concatenate([np.broadcast_to(w + j - c, (SPAD, w)),
                               np.broadcast_to(j - cn, (SPAD, KB))], axis=1)
        real = np.concatenate([np.ones((SPAD, w), bool), np.broadcast_to(cn < T_S, (SPAD, KB))], axis=1)
        valid = (j < T_S) & real & (dist >= 0) & (dist <= w) & (dist % d == 0)
        pad_row = np.broadcast_to(j >= T_S, dist.shape)
        tab = np.empty((NH * SPAD, dist.shape[1]), np.float32)
        for h in range(NH):
            bias = -slopes[g, h] * dist.astype(np.float32)
            blk = np.where(valid, bias, np.float32(NEG))
            tab[h * SPAD:(h + 1) * SPAD] = np.where(pad_row, np.float32(0.0), blk)
        tabs.append(tab)
    return tabs


def _attn_sample_body(z_ref, c1_ref, c2_ref, c3_ref, b1_ref, b2_ref, b3_ref, att_ref):
    for s in range(z_ref.shape[0]):
        _attn_sample_one(z_ref.at[s], (c1_ref.at[s], c2_ref.at[s], c3_ref.at[s]),
                         (b1_ref, b2_ref, b3_ref), att_ref.at[s])


def _attn_sample_one(z_ref, c_refs, b_refs, att_ref):
    masks = _head_masks()
    z = z_ref[...]
    pad = jnp.zeros((KB - SPAD, GW), F32)
    outs, lses = [], []
    for g, (c_ref, b_ref) in enumerate(zip(c_refs, b_refs)):
        w = WINDOWS[g]
        q = z[:, g * GW:(g + 1) * GW]
        k_new = jnp.concatenate([z[:, ATT + g * GW:ATT + (g + 1) * GW], pad], axis=0)
        v_new = jnp.concatenate([z[:, 2 * ATT + g * GW:2 * ATT + (g + 1) * GW], pad], axis=0)
        k_all = jnp.concatenate([c_ref[:GW, :], k_new.T], axis=1).astype(BF16)
        v_all = jnp.concatenate([c_ref[GW:, :], v_new.T], axis=1).astype(BF16)
        qs = _stack_heads(q, masks).astype(BF16)
        o, lse = _softmax_parts([_dot(qs, k_all) + b_ref[...]], [v_all], [_dot_nt])
        outs.append(_unstack_heads(o, masks, SPAD))
        lses.append(_unstack_heads(lse, masks, SPAD))
    att_ref[...] = _merge_groups(outs, lses)


def _attn_sample_operands(z, caches, tabs, layer, n_steps, step=lambda i: i):
    per_step = B_S // n_steps
    assert per_step * n_steps == B_S
    zp = jnp.pad(z.reshape(B_S, T_S, 3 * ATT), ((0, 0), (0, SPAD - T_S), (0, 0)))
    cs = [c.transpose(0, 1, 3, 4, 5, 2).reshape(DEPTH, B_S, 2 * GW, w) for c, w in zip(caches, WINDOWS)]
    in_specs = [pl.BlockSpec((per_step, SPAD, 3 * ATT), lambda *ids: (step(*ids), 0, 0))]
    in_specs += [pl.BlockSpec((None, per_step, 2 * GW, w), lambda *ids: (layer, step(*ids), 0, 0)) for w in WINDOWS]
    in_specs += [_const(t.shape) for t in tabs]
    out_spec = pl.BlockSpec((per_step, SPAD, GW), lambda *ids: (step(*ids), 0, 0))
    return [zp, *cs, *tabs], in_specs, out_spec, jax.ShapeDtypeStruct((B_S, SPAD, GW), F32)


def _mix_out(x, h, att, spat, wgate_ref, pa_ref, ps_ref, wo_ref):
    ga = jax.nn.sigmoid(_dot(h, wgate_ref[:, :D]))
    gb = jax.nn.sigmoid(_dot(h, wgate_ref[:, D:]))
    mix = ga * _dot(att.astype(BF16), pa_ref[...]) + gb * _dot(spat, ps_ref[...])
    return x + _dot(mix.astype(BF16), wo_ref[...])


def _mixer_out_body(x_ref, h_ref, wgate_ref, q1_ref, kv1_ref, kv1p_ref, q2_ref, kv2_ref, kv2p_ref, q3_ref, kv3_ref,
                    bias_ref, u_ref, vs_ref, ws_ref, b_ref, pa_ref, ps_ref, wo_ref, *rest):
    if len(rest) > 4:
        _attn_sample_body(*rest[:7], rest[8])
        out_ref, o_scr, l_scr, spat_scr = rest[7], *rest[9:]
    else:
        out_ref, o_scr, l_scr, spat_scr = rest
    j = pl.program_id(1)
    masks = _head_masks()
    lane = lax.broadcasted_iota(jnp.int32, (1, 2 * KB), 1)
    first_tile_mask = (j > 0) | (lane >= KB)

    def wide(ref, base, rows):
        return jnp.concatenate([ref[base + sl, rows, :] for sl in range(SLABS)], axis=1)

    def band(g, q, kv, bias, rows, n):
        qs = _stack_heads(q, masks)
        s = _dot_nt(qs, kv[:, :GW]) + bias
        m = jnp.max(s, axis=-1, keepdims=True)
        p = jnp.exp(s - m)
        l = jnp.sum(p, axis=-1, keepdims=True)
        o = _unstack_heads(_dot(p.astype(BF16), kv[:, GW:]) / l, masks, n)
        lse = _unstack_heads(m + jnp.log(l), masks, n)
        for sl in range(SLABS):
            o_scr[g * SLABS + sl, rows, :] = o[:, sl * LANES:(sl + 1) * LANES]
            l_scr[g * SLABS + sl, rows, :] = lse[:, sl * LANES:(sl + 1) * LANES]

    bias1 = bias_ref[0]
    for blk in range(TM // KB):
        own = pl.ds(blk * KB, KB)
        if blk == 0:
            kv = jnp.concatenate([kv1p_ref[...], kv1_ref[own, :]], axis=0)
            bias = jnp.where(first_tile_mask, bias1, NEG)
        else:
            kv, bias = kv1_ref[pl.ds((blk - 1) * KB, 2 * KB), :], bias1
        band(0, q1_ref[own, :], kv, bias, own, KB)

    bias2 = jnp.where(first_tile_mask, bias_ref[1], NEG)
    for r in range(DILS[1]):
        kv = jnp.concatenate([kv2p_ref[r], kv2_ref[r]], axis=0)
        band(1, q2_ref[r], kv, bias2, pl.ds(r, KB, stride=DILS[1]), KB)

    nq = TM // DILS[2]
    q0 = pl.multiple_of(j * nq, nq)
    bias3 = jnp.concatenate([bias_ref[2, pl.ds(h * KB + q0, nq), KB:] for h in range(NH)], axis=0)
    for r in range(DILS[2]):
        band(2, q3_ref[r], kv3_ref[r], bias3, pl.ds(r, nq, stride=DILS[2]), nq)

    full = slice(None)
    att = _merge_groups([wide(o_scr, g * SLABS, full) for g in range(NG)],
                        [wide(l_scr, g * SLABS, full) for g in range(NG)])

    x = x_ref[...]
    h = h_ref[...]
    row = lax.broadcasted_iota(jnp.int32, (CHUNK, CHUNK), 0)
    col = lax.broadcasted_iota(jnp.int32, (CHUNK, CHUNK), 1)
    for g in range(GG):
        w = jnp.where(col <= row, ws_ref[g], 0.0).astype(BF16)
        gs = slice(g * GC, (g + 1) * GC)
        for c in range(TM // CHUNK):
            rs = slice(c * CHUNK, (c + 1) * CHUNK)
            zc = _dot(w, vs_ref[rs, gs]) + b_ref[g]
            spat_scr[rs, gs] = (u_ref[rs, gs] * zc).astype(BF16)
    out_ref[...] = _mix_out(x, h, att, spat_scr[...], wgate_ref, pa_ref, ps_ref, wo_ref)


def _mixer_out(x, h, wgate, attn_ops, bias_tab, u, vs, ws, bias, pa, ps, wo, layer, attn=None):
    tiles = S_P // TM
    assert TM == KB * DILS[1] and S_P == KB * DILS[2]
    q1, kv1, q2, kv2, q3, kv3 = attn_ops
    row = lambda b, j: (b * tiles + j, 0)
    res = lambda b, j: (b, 0, j, 0)
    res_before = lambda b, j: (b, 0, jnp.maximum(j - 1, 0), 0)
    row_before = lambda b, j: (jnp.maximum((b * tiles + j) * (TM // KB) - 1, 0), 0)
    d2, d3 = DILS[1], DILS[2]
    in_specs = [pl.BlockSpec((TM, D), row), pl.BlockSpec((TM, D), row), _const((D, 2 * D)),
                pl.BlockSpec((TM, GW), row), pl.BlockSpec((TM, 2 * GW), row),
                pl.BlockSpec((KB, 2 * GW), row_before),
                pl.BlockSpec((None, d2, TM // d2, GW), res), pl.BlockSpec((None, d2, TM // d2, 2 * GW), res),
                pl.BlockSpec((None, d2, TM // d2, 2 * GW), res_before),
                pl.BlockSpec((None, d3, TM // d3, GW), res),
                pl.BlockSpec((None, d3, S_P // d3, 2 * GW), lambda b, j: (b, 0, 0, 0)),
                _const(bias_tab.shape),
                pl.BlockSpec((TM, GMW), row), pl.BlockSpec((TM, GMW), row),
                _resident((GG, CHUNK, CHUNK), layer), _resident((GG, CHUNK, 1), layer),
                _const((GW, D)), _const((GMW, D)), _const((D, D))]
    args = [x, h, wgate, q1, kv1, kv1, q2, kv2, kv2, q3, kv3, bias_tab, u, vs, ws, bias, pa, ps, wo]
    out_specs, out_shape = [pl.BlockSpec((TM, D), row)], [jax.ShapeDtypeStruct((N_P, D), F32)]
    if attn is not None:
        attn_args, attn_in_specs, attn_out_spec, attn_out_shape = attn(B_P * tiles, lambda b, j: b * tiles + j)
        in_specs += attn_in_specs
        args += attn_args
        out_specs.append(attn_out_spec)
        out_shape.append(attn_out_shape)
    outs = pl.pallas_call(
        _mixer_out_body,
        out_shape=out_shape,
        grid=(B_P, tiles),
        in_specs=in_specs,
        out_specs=out_specs,
        scratch_shapes=[pltpu.VMEM((NG * SLABS, TM, LANES), F32), pltpu.VMEM((NG * SLABS, TM, LANES), F32),
                        pltpu.VMEM((TM, GMW), BF16)],
        compiler_params=_params(2),
        name="mixer_out",
    )(*args)
    return outs[0], (outs[1] if attn is not None else None)


def _outproj_sample_body(x_ref, g_ref, wgate_ref, att_ref, u_ref, vs_ref, wt_ref, b_ref,
                         pa_ref, ps_ref, wo_ref, out_ref):
    x = x_ref[...]
    h = _rms(x, g_ref[...]).astype(BF16)
    row = lax.broadcasted_iota(jnp.int32, (N_S, N_S), 0)
    col = lax.broadcasted_iota(jnp.int32, (N_S, N_S), 1)
    keep = (row // T_S == col // T_S) & (col <= row)
    u = u_ref[...]
    vs = vs_ref[...].astype(BF16)
    parts = []
    for g in range(GG):
        corner = wt_ref[g]
        w = jnp.zeros((N_S, N_S), F32)
        for t in range(T_S):
            w = jnp.where(keep & (lax.rem(row, T_S) == t), corner[t:t + 1, :], w)
        w = w.astype(BF16)
        gs = slice(g * GC, (g + 1) * GC)
        parts.append(u[:, gs] * (_dot(w, vs[:, gs]) + b_ref[g]))
    spat = jnp.concatenate(parts, axis=1).astype(BF16)
    out_ref[...] = _mix_out(x, h, att_ref[...], spat, wgate_ref, pa_ref, ps_ref, wo_ref)


def _outproj_sample(x, gain, wgate, att, u, vs_all, wt, bias, pa, ps, wo, layer):
    zero = lambda i: (0, 0)
    in_specs = [pl.BlockSpec((N_S, D), zero), _resident((1, D), layer), _const((D, 2 * D)),
                pl.BlockSpec((N_S, GW), zero), pl.BlockSpec((N_S, GMW), zero),
                pl.BlockSpec((N_S, GMW), lambda i: (layer, 0)),
                _resident((GG, T_S, N_S), layer), _resident((GG, N_S, 1), layer),
                _const((GW, D)), _const((GMW, D)), _const((D, D))]
    return pl.pallas_call(
        _outproj_sample_body,
        out_shape=jax.ShapeDtypeStruct((N_S, D), F32),
        grid=(1,),
        in_specs=in_specs,
        out_specs=pl.BlockSpec((N_S, D), zero),
        compiler_params=_params(1),
        name="outproj_sample",
    )(x, gain, wgate, att, u, vs_all, wt, bias, pa, ps, wo)


def kernel(x_prompt, x_sample, cache_kv_w128, cache_kv_w512, cache_kv_w2048, ffn1_norm, ffn1_gate, ffn1_up, ffn1_down, mix_norm, w_in, gmlp_v_norm, gmlp_ws, gmlp_bias, proj_att, proj_spatial, w_out, ffn2_norm, ffn2_gate, ffn2_up, ffn2_down, final_norm):
    caches = (cache_kv_w128, cache_kv_w512, cache_kv_w2048)
    xp = x_prompt.reshape(N_P, D)
    xs = x_sample.reshape(N_S, D)

    ffn1_jobs = lambda l: [_CastJob(ffn1_gate, l), _CastJob(ffn1_up, l), _CastJob(ffn1_down, l)]
    ffn2_jobs = lambda l: [_CastJob(ffn2_gate, l), _CastJob(ffn2_up, l), _CastJob(ffn2_down, l)]
    mix_jobs = lambda l: [_CastJob(w_in, l, [(0, QKVUV), (QKVUV, QKVUV + 2 * D)]),
                          _CastJob(proj_att, l), _CastJob(proj_spatial, l), _CastJob(w_out, l)]

    n1 = ffn1_norm.reshape(DEPTH, 1, D)
    n2 = ffn2_norm.reshape(DEPTH, 1, D)
    nm = mix_norm.reshape(DEPTH, 1, D)
    nv = gmlp_v_norm.reshape(DEPTH, 1, GMW)
    nf = final_norm.reshape(1, D)
    bias_p = gmlp_bias.reshape(DEPTH, GG, CHUNK, 1)
    wt_s = jnp.tile(gmlp_ws[:, :, :T_S, :T_S], (1, 1, 1, B_S))
    bias_s = jnp.tile(gmlp_bias[:, :, :T_S], (1, 1, B_S)).reshape(DEPTH, GG, N_S, 1)

    attn_bias = jnp.asarray(_prompt_bias_table())
    sample_tabs = [jnp.asarray(t) for t in _sample_bias_tables()]

    f1 = {0: _cast(ffn1_jobs(0))}
    f2, mix = {}, {}
    kv_p = kv_s = None
    for l in range(DEPTH):
        last = l == DEPTH - 1
        fin = nf if last else None
        xp, hp, xs, outs = _ffn(xp, xs, n1, l, f1[l], next_gain=nm,
                                jobs=ffn2_jobs(l) + (mix_jobs(l) if l == 0 else []))
        f2[l] = outs[:3]
        if l == 0:
            mix[l] = outs[3:]
        z_s, ks1, ks2, ks3, u_s, vs_s = _inproj_sample(xs, nm, mix[l][0], nv, l, kv_s)
        kv_s = (ks1, ks2, ks3, vs_s)
        attn = functools.partial(_attn_sample_operands, z_s, caches, sample_tabs, l)
        kv_p, attn_ops, u, vs = _inproj_prompt(hp, mix[l][0], nv, l, kv_p)
        xp, att = _mixer_out(xp, hp, mix[l][1], attn_ops, attn_bias, u, vs, gmlp_ws, bias_p, *mix[l][2:], l,
                             attn=attn)
        att_s = att[:, :T_S].reshape(N_S, GW)
        xs = _outproj_sample(xs, nm, mix[l][1], att_s, u_s, vs_s, wt_s, bias_s, *mix[l][2:], l)
        xp, _, xs, outs = _ffn(xp, xs, n2, l, f2[l], fin, jobs=[] if last else ffn1_jobs(l + 1) + mix_jobs(l + 1))
        if not last:
            f1[l + 1], mix[l + 1] = outs[:3], outs[3:]

    kv_out = [k.reshape(DEPTH, B_P, 2, NH, HD, w).transpose(0, 1, 5, 2, 3, 4) for k, w in zip(kv_p, WINDOWS)]
    ks_shape = (DEPTH, B_S, T_S, 2, NH, HD)
    return (xp.reshape(B_P, S_P, D), xs.reshape(B_S, T_S, D), kv_out[0], kv_out[1], kv_out[2],
            kv_s[0].reshape(ks_shape), kv_s[1].reshape(ks_shape), kv_s[2].reshape(ks_shape),
            kv_s[3].reshape(DEPTH, B_S, T_S, GMW))
```
